```python
import math
import jax, jax.numpy as jnp
from jax import lax
import numpy as np

D_MODEL = 1024
BATCH = 16
SEQ = 256
DEPTH = 4
DEC_BATCH = 2
DEC_SEQ = 4096
PAST_LEN = 512

GRID_W = 64
HEAD_DIM = 64
SSM_WIDTH = D_MODEL // 4
SSM_GROUP_CH = 16
SSM_GROUPS = SSM_WIDTH // SSM_GROUP_CH
SSM_STATE = 64
DT_MIN = 0.001
DT_MAX = 0.1
NA_HEADS = (3 * D_MODEL // 8) // HEAD_DIM
NA_WIDTH = NA_HEADS * HEAD_DIM
NA_KH = 8
NA_KW = 16
GQA_HEADS = (D_MODEL - SSM_WIDTH - NA_WIDTH) // HEAD_DIM
GQA_KV_HEADS = 2
GQA_REP = GQA_HEADS // GQA_KV_HEADS
GQA_WIDTH = GQA_HEADS * HEAD_DIM
KV_WIDTH = GQA_KV_HEADS * HEAD_DIM
IN_WIDTH = SSM_WIDTH + 3 * NA_WIDTH + GQA_WIDTH + 2 * KV_WIDTH
MIX_WIDTH = SSM_WIDTH + NA_WIDTH + GQA_WIDTH
ROPE_THETA = 10000.0
ROPE_F = HEAD_DIM // 4
Q_BLOCK = 128
N_GROUPS = 4
EXPERTS_PER_GROUP = 4
N_EXPERTS = N_GROUPS * EXPERTS_PER_GROUP
TOP_K = 2
EXPERT_HIDDEN = D_MODEL // 4
EPS = 1e-6
NEG_INF = -1e30

kernel_name = 'hymba_dit_s5_natten_gqa_hmoe_step'


def rms_norm(x, g):
    xf = x.astype(jnp.float32)
    y = xf * lax.rsqrt(jnp.mean(xf * xf, axis=-1, keepdims=True) + EPS)
    return (y * g.astype(jnp.float32)).astype(x.dtype)


def adaln(cvec, w, b):
    m = jax.nn.silu(cvec) @ w + b
    return [t[:, None, :] for t in jnp.split(m, 6, axis=-1)]


def split_projection(h, w_in):
    B, N, _ = h.shape
    proj = h @ w_in
    cuts = np.cumsum([SSM_WIDTH, NA_WIDTH, NA_WIDTH, NA_WIDTH, GQA_WIDTH, KV_WIDTH]).tolist()
    u, qn, kn, vn, qg, kg, vg = jnp.split(proj, cuts, axis=-1)

    def heads(t, n_heads):
        return t.reshape(B, N, n_heads, HEAD_DIM).transpose(0, 2, 1, 3)

    qg = qg.reshape(B, N, GQA_KV_HEADS, GQA_REP, HEAD_DIM).transpose(0, 2, 3, 1, 4)
    return (u, heads(qn, NA_HEADS), heads(kn, NA_HEADS), heads(vn, NA_HEADS),
            qg, heads(kg, GQA_KV_HEADS), heads(vg, GQA_KV_HEADS))


def s5_discretize(a_re, a_im, log_dt, b_re, b_im):
    f32 = jnp.float32
    ar, ai = a_re.astype(f32), a_im.astype(f32)
    dt = jnp.exp(log_dt.astype(f32))[..., None]
    mag = jnp.exp(ar * dt)
    abar_re = mag * jnp.cos(ai * dt)
    abar_im = mag * jnp.sin(ai * dt)
    zr, zi = abar_re - 1.0, abar_im
    den = ar * ar + ai * ai
    coef_re = (zr * ar + zi * ai) / den
    coef_im = (zi * ar - zr * ai) / den
    br, bi = b_re.astype(f32), b_im.astype(f32)
    bbar_re = coef_re[..., None] * br - coef_im[..., None] * bi
    bbar_im = coef_re[..., None] * bi + coef_im[..., None] * br
    return abar_re, abar_im, bbar_re, bbar_im


def s5_combine(e1, e2):
    a1r, a1i, b1r, b1i = e1
    a2r, a2i, b2r, b2i = e2
    return (a2r * a1r - a2i * a1i, a2r * a1i + a2i * a1r,
            a2r * b1r - a2i * b1i + b2r, a2r * b1i + a2i * b1r + b2i)


def s5_scan(abar_re, abar_im, bu_re, bu_im, h0_re, h0_im, reverse):
    first = -1 if reverse else 0
    bu_re = bu_re.at[:, first].add(abar_re * h0_re - abar_im * h0_im)
    bu_im = bu_im.at[:, first].add(abar_re * h0_im + abar_im * h0_re)
    a_re = jnp.broadcast_to(abar_re, bu_re.shape)
    a_im = jnp.broadcast_to(abar_im, bu_im.shape)
    _, _, h_re, h_im = lax.associative_scan(s5_combine, (a_re, a_im, bu_re, bu_im),
                                            reverse=reverse, axis=1)
    return h_re, h_im


def s5_mixer(u, h0, p):
    B, N, _ = u.shape
    f32 = jnp.float32
    uf = u.astype(f32)
    ug = uf.reshape(B, N, SSM_GROUPS, SSM_GROUP_CH)
    abar_re, abar_im, bbar_re, bbar_im = s5_discretize(
        p['ssm_a_re'], p['ssm_a_im'], p['ssm_log_dt'], p['ssm_b_re'], p['ssm_b_im'])
    h0 = h0.astype(f32)
    y = uf * p['ssm_d'].astype(f32)
    finals = []
    for direction in range(2):
        rev = direction == 1
        bu_re = jnp.einsum('gpc,bngc->bngp', bbar_re[direction], ug)
        bu_im = jnp.einsum('gpc,bngc->bngp', bbar_im[direction], ug)
        h_re, h_im = s5_scan(abar_re[direction], abar_im[direction], bu_re, bu_im,
                             h0[:, direction, ..., 0], h0[:, direction, ..., 1], rev)
        c_re = p['ssm_c_re'][direction].astype(f32)
        c_im = p['ssm_c_im'][direction].astype(f32)
        y_dir = jnp.einsum('gcp,bngp->bngc', c_re, h_re) - jnp.einsum('gcp,bngp->bngc', c_im, h_im)
        y = y + y_dir.reshape(B, N, SSM_WIDTH)
        last = 0 if rev else N - 1
        finals.append(jnp.stack([h_re[:, last], h_im[:, last]], axis=-1))
    yg = jax.nn.gelu(y)
    out = yg * jax.nn.sigmoid(yg @ p['w_glu'].astype(f32) + p['b_glu'].astype(f32))
    return out.astype(u.dtype), jnp.stack(finals, axis=1)


def axial_rope_tables(n):
    t = jnp.arange(n)
    row = (t // GRID_W).astype(jnp.float32)
    col = (t % GRID_W).astype(jnp.float32)
    freqs = ROPE_THETA ** (-jnp.arange(ROPE_F, dtype=jnp.float32) / ROPE_F)
    ang = jnp.stack([row[:, None] * freqs, col[:, None] * freqs], axis=1)
    return jnp.cos(ang), jnp.sin(ang)


def apply_axial_rope(x, cos, sin):
    xs = x.reshape(*x.shape[:-1], 2, 2, ROPE_F)
    x1, x2 = xs[..., 0, :], xs[..., 1, :]
    o1 = x1 * cos - x2 * sin
    o2 = x2 * cos + x1 * sin
    return jnp.stack([o1, o2], axis=-2).reshape(x.shape).astype(x.dtype)


def block_attention(q, k, v):
    B, G, R, N, dh = q.shape
    scale = dh ** -0.5
    qb = jnp.moveaxis(q.reshape(B, G, R, N // Q_BLOCK, Q_BLOCK, dh), 3, 0)

    def one_block(qblk):
        s = jnp.einsum('bgrqd,bgkd->bgrqk', qblk, k, preferred_element_type=jnp.float32) * scale
        pr = jax.nn.softmax(s, axis=-1)
        return jnp.einsum('bgrqk,bgkd->bgrqd', pr.astype(v.dtype), v)

    o = lax.map(one_block, qb)
    return jnp.moveaxis(o, 0, 3).reshape(B, G, R, N, dh)


def na_latent(q, k, v, ctx_k, ctx_v, rpb):
    B, H, N, dh = q.shape
    rows = N // GRID_W
    kh = min(NA_KH, rows)
    scale = dh ** -0.5
    r = jnp.arange(rows)
    rs = jnp.clip(r - kh // 2, 0, rows - kh)
    row_idx = rs[:, None] + jnp.arange(kh)[None, :]
    cq = jnp.arange(GRID_W)
    cs = jnp.clip(cq - NA_KW // 2, 0, GRID_W - NA_KW)
    ck = jnp.arange(GRID_W)
    col_in = (ck[None, :] >= cs[:, None]) & (ck[None, :] < cs[:, None] + NA_KW)
    dr = row_idx - r[:, None] + (NA_KH - 1)
    dc = jnp.clip(ck[None, :] - cq[:, None] + NA_KW - 1, 0, 2 * NA_KW - 2)
    bias = rpb.astype(jnp.float32)[:, dr][:, :, :, dc]
    bias = jnp.transpose(bias, (0, 1, 3, 2, 4))
    bias = jnp.where(col_in[None, None, :, None, :], bias, NEG_INF)
    qg = q.reshape(B, H, rows, GRID_W, dh)
    kg = k.reshape(B, H, rows, GRID_W, dh)[:, :, row_idx]
    vg = v.reshape(B, H, rows, GRID_W, dh)[:, :, row_idx]
    s_lat = jnp.einsum('bhrqd,bhrikd->bhrqik', qg, kg, preferred_element_type=jnp.float32) * scale + bias[None]
    s_ctx = jnp.einsum('bhrqd,bhmd->bhrqm', qg, ctx_k, preferred_element_type=jnp.float32) * scale
    n_lat = kh * GRID_W
    s = jnp.concatenate([s_lat.reshape(B, H, rows, GRID_W, n_lat), s_ctx], axis=-1)
    pr = jax.nn.softmax(s, axis=-1).astype(v.dtype)
    p_lat = pr[..., :n_lat].reshape(B, H, rows, GRID_W, kh, GRID_W)
    p_ctx = pr[..., n_lat:]
    o = jnp.einsum('bhrqik,bhrikd->bhrqd', p_lat, vg) + jnp.einsum('bhrqm,bhmd->bhrqd', p_ctx, ctx_v)
    return o.reshape(B, H, N, dh)


def hier_moe(h, p):
    B, N, D = h.shape
    f32 = jnp.float32
    t = h.reshape(B * N, D)
    p_group = jax.nn.softmax((t @ p['moe_w_group'] + p['moe_b_group']).astype(f32), axis=-1)
    pg_top, g_idx = lax.top_k(p_group, 1)
    g_onehot = jax.nn.one_hot(g_idx[:, 0], N_GROUPS, dtype=f32)
    e_logits = (t @ p['moe_w_expert'] + p['moe_b_expert']).astype(f32)
    e_logits = e_logits.reshape(-1, N_GROUPS, EXPERTS_PER_GROUP)
    e_sel = jnp.sum(e_logits * g_onehot[:, :, None], axis=1)
    pe_top, e_idx = lax.top_k(jax.nn.softmax(e_sel, axis=-1), TOP_K)
    pe_top = pe_top / jnp.sum(pe_top, axis=-1, keepdims=True)
    within = jnp.sum(jax.nn.one_hot(e_idx, EXPERTS_PER_GROUP, dtype=f32) * pe_top[..., None], axis=1)
    gates = (g_onehot[:, :, None] * (pg_top * within)[:, None, :]).reshape(-1, N_EXPERTS)
    a = jnp.einsum('td,edf->tef', t, p['moe_w_gate'])
    b = jnp.einsum('td,edf->tef', t, p['moe_w_up'])
    act = jax.nn.silu(a) * b * gates[..., None].astype(t.dtype)
    return jnp.einsum('tef,efd->td', act, p['moe_w_down']).reshape(B, N, D)


def merge_and_ffn(x, y_ssm, y_na, y_g, mods, p):
    shift1, scale1, gate1, shift2, scale2, gate2 = mods
    B, N, _ = x.shape
    y_na = y_na.transpose(0, 2, 1, 3).reshape(B, N, NA_WIDTH)
    y_g = y_g.transpose(0, 3, 1, 2, 4).reshape(B, N, GQA_WIDTH)
    mix = jnp.concatenate([y_ssm, y_na, y_g], axis=-1) @ p['w_out']
    x = x + gate1 * mix
    h = rms_norm(x, p['norm_ffn_g']) * (1.0 + scale2) + shift2
    return x + gate2 * hier_moe(h, p)


def context_layer(x, cvec, p):
    B = x.shape[0]
    mods = adaln(cvec, p['w_ada'], p['b_ada'])
    h = rms_norm(x, p['norm_mix_g']) * (1.0 + mods[1]) + mods[0]
    u, qn, kn, vn, qg, kg, vg = split_projection(h, p['w_in'])
    h0 = jnp.zeros((B, 2, SSM_GROUPS, SSM_STATE, 2), jnp.float32)
    y_ssm, ssm_state = s5_mixer(u, h0, p)
    y_na = block_attention(qn[:, :, None], kn, vn)[:, :, 0]
    qg = rms_norm(qg, p['q_norm_g'])
    kg = rms_norm(kg, p['k_norm_g'])
    y_g = block_attention(qg, kg, vg)
    x = merge_and_ffn(x, y_ssm, y_na, y_g, mods, p)
    return x, kn, vn, kg, vg, ssm_state


def latent_layer(x, cvec, ctx_na_k, ctx_na_v, ctx_g_k, ctx_g_v, ssm_h0, rope_cos, rope_sin, p):
    mods = adaln(cvec, p['w_ada'], p['b_ada'])
    h = rms_norm(x, p['norm_mix_g']) * (1.0 + mods[1]) + mods[0]
    u, qn, kn, vn, qg, kg, vg = split_projection(h, p['w_in'])
    y_ssm, _ = s5_mixer(u, ssm_h0, p)
    y_na = na_latent(qn, kn, vn, ctx_na_k, ctx_na_v, p['na_rpb'])
    qg = apply_axial_rope(rms_norm(qg, p['q_norm_g']), rope_cos, rope_sin)
    kg = apply_axial_rope(rms_norm(kg, p['k_norm_g']), rope_cos, rope_sin)
    k_all = jnp.concatenate([ctx_g_k.astype(kg.dtype), kg], axis=2)
    v_all = jnp.concatenate([ctx_g_v.astype(vg.dtype), vg], axis=2)
    y_g = block_attention(qg, k_all, v_all)
    return merge_and_ffn(x, y_ssm, y_na, y_g, mods, p)


def setup_inputs(seed: int = 0) -> dict:
    key = jax.random.key(seed)
    keys = iter(jax.random.split(key, 64))
    f32 = jnp.float32
    L, G, P, C, E, F = DEPTH, SSM_GROUPS, SSM_STATE, SSM_GROUP_CH, N_EXPERTS, EXPERT_HIDDEN

    def nrm(shape, scale):
        return jax.random.normal(next(keys), shape, f32) * scale

    state_idx = jnp.arange(P, dtype=f32)
    return {
        'x_prompt': nrm((BATCH, SEQ, D_MODEL), 1.0),
        'x_sample': nrm((DEC_BATCH, DEC_SEQ, D_MODEL), 1.0),
        'cache_na_k': nrm((DEC_BATCH, L, NA_HEADS, PAST_LEN, HEAD_DIM), 1.0),
        'cache_na_v': nrm((DEC_BATCH, L, NA_HEADS, PAST_LEN, HEAD_DIM), 1.0),
        'cache_gqa_k': nrm((DEC_BATCH, L, GQA_KV_HEADS, PAST_LEN, HEAD_DIM), 1.0),
        'cache_gqa_v': nrm((DEC_BATCH, L, GQA_KV_HEADS, PAST_LEN, HEAD_DIM), 1.0),
        'state_ssm': nrm((DEC_BATCH, L, 2, G, P, 2), 0.1),
        'c': nrm((DEC_BATCH, D_MODEL), 1.0),
        'c_ctx': nrm((D_MODEL,), 1.0),
        'norm_mix_g': 1.0 + nrm((L, D_MODEL), 0.01),
        'norm_ffn_g': 1.0 + nrm((L, D_MODEL), 0.01),
        'w_ada': nrm((L, D_MODEL, 6 * D_MODEL), 0.5 * D_MODEL ** -0.5),
        'b_ada': nrm((L, 6 * D_MODEL), 0.01),
        'w_in': nrm((L, D_MODEL, IN_WIDTH), D_MODEL ** -0.5),
        'ssm_a_re': -0.5 + nrm((L, 2, G, P), 0.01),
        'ssm_a_im': math.pi * state_idx + nrm((L, 2, G, P), 0.01),
        'ssm_log_dt': jax.random.uniform(next(keys), (L, 2, G), f32, math.log(DT_MIN), math.log(DT_MAX)),
        'ssm_b_re': nrm((L, 2, G, P, C), (2 * C) ** -0.5),
        'ssm_b_im': nrm((L, 2, G, P, C), (2 * C) ** -0.5),
        'ssm_c_re': nrm((L, 2, G, C, P), P ** -0.5),
        'ssm_c_im': nrm((L, 2, G, C, P), P ** -0.5),
        'ssm_d': nrm((L, SSM_WIDTH), 1.0),
        'w_glu': nrm((L, SSM_WIDTH, SSM_WIDTH), SSM_WIDTH ** -0.5),
        'b_glu': nrm((L, SSM_WIDTH), 0.01),
        'na_rpb': nrm((L, NA_HEADS, 2 * NA_KH - 1, 2 * NA_KW - 1), 0.02),
        'q_norm_g': 1.0 + nrm((L, HEAD_DIM), 0.01),
        'k_norm_g': 1.0 + nrm((L, HEAD_DIM), 0.01),
        'w_out': nrm((L, MIX_WIDTH, D_MODEL), MIX_WIDTH ** -0.5),
        'moe_w_group': nrm((L, D_MODEL, N_GROUPS), D_MODEL ** -0.5),
        'moe_b_group': nrm((L, N_GROUPS), 0.01),
        'moe_w_expert': nrm((L, D_MODEL, E), D_MODEL ** -0.5),
        'moe_b_expert': nrm((L, E), 0.01),
        'moe_w_gate': nrm((L, E, D_MODEL, F), D_MODEL ** -0.5),
        'moe_w_up': nrm((L, E, D_MODEL, F), D_MODEL ** -0.5),
        'moe_w_down': nrm((L, E, F, D_MODEL), F ** -0.5),
        'final_norm_g': 1.0 + nrm((D_MODEL,), 0.01),
    }


def reference(x_prompt, x_sample, cache_na_k, cache_na_v, cache_gqa_k, cache_gqa_v, state_ssm,
              c, c_ctx, norm_mix_g, norm_ffn_g, w_ada, b_ada, w_in, ssm_a_re, ssm_a_im, ssm_log_dt,
              ssm_b_re, ssm_b_im, ssm_c_re, ssm_c_im, ssm_d, w_glu, b_glu, na_rpb, q_norm_g, k_norm_g,
              w_out, moe_w_group, moe_b_group, moe_w_expert, moe_b_expert, moe_w_gate, moe_w_up,
              moe_w_down, final_norm_g):
    xp = x_prompt
    xs = x_sample
    rope_cos, rope_sin = axial_rope_tables(x_sample.shape[1])
    na_k_l, na_v_l, g_k_l, g_v_l, st_l = [], [], [], [], []
    for l in range(DEPTH):
        p = {
            'norm_mix_g': norm_mix_g[l], 'norm_ffn_g': norm_ffn_g[l],
            'w_ada': w_ada[l], 'b_ada': b_ada[l], 'w_in': w_in[l],
            'ssm_a_re': ssm_a_re[l], 'ssm_a_im': ssm_a_im[l], 'ssm_log_dt': ssm_log_dt[l],
            'ssm_b_re': ssm_b_re[l], 'ssm_b_im': ssm_b_im[l],
            'ssm_c_re': ssm_c_re[l], 'ssm_c_im': ssm_c_im[l], 'ssm_d': ssm_d[l],
            'w_glu': w_glu[l], 'b_glu': b_glu[l], 'na_rpb': na_rpb[l],
            'q_norm_g': q_norm_g[l], 'k_norm_g': k_norm_g[l], 'w_out': w_out[l],
            'moe_w_group': moe_w_group[l], 'moe_b_group': moe_b_group[l],
            'moe_w_expert': moe_w_expert[l], 'moe_b_expert': moe_b_expert[l],
            'moe_w_gate': moe_w_gate[l], 'moe_w_up': moe_w_up[l], 'moe_w_down': moe_w_down[l],
        }
        xp, nk, nv, gk, gv, st = context_layer(xp, c_ctx[None, :], p)
        na_k_l.append(nk)
        na_v_l.append(nv)
        g_k_l.append(gk)
        g_v_l.append(gv)
        st_l.append(st)
        xs = latent_layer(xs, c, cache_na_k[:, l], cache_na_v[:, l], cache_gqa_k[:, l],
                          cache_gqa_v[:, l], state_ssm[:, l], rope_cos, rope_sin, p)
    y_prompt = rms_norm(xp, final_norm_g)
    y_sample = rms_norm(xs, final_norm_g)
    new_na_k = jnp.stack(na_k_l, axis=1)
    new_na_v = jnp.stack(na_v_l, axis=1)
    new_gqa_k = jnp.stack(g_k_l, axis=1)
    new_gqa_v = jnp.stack(g_v_l, axis=1)
    new_state_ssm = jnp.stack(st_l, axis=1)
    return (y_prompt, y_sample, new_na_k, new_na_v, new_gqa_k, new_gqa_v, new_state_ssm)
```

```python
import functools
import math

import jax
import jax.numpy as jnp
import numpy as np
from jax import lax
from jax.experimental import pallas as pl
from jax.experimental.pallas import tpu as pltpu

D_MODEL = 1024
BATCH = 16
SEQ = 256
DEPTH = 4
DEC_BATCH = 2
DEC_SEQ = 4096
PAST_LEN = 512
GRID_W = 64
GRID_H = DEC_SEQ // GRID_W
HEAD_DIM = 64
SSM_WIDTH = 256
SSM_GROUP_CH = 16
SSM_GROUPS = 16
SSM_STATE = 64
NA_HEADS = 6
NA_WIDTH = 384
NA_KH = 8
NA_KW = 16
GQA_HEADS = 6
GQA_KV_HEADS = 2
GQA_REP = 3
GQA_WIDTH = 384
KV_WIDTH = 128
IN_WIDTH = 2048
ROPE_THETA = 10000.0
ROPE_F = 16
N_GROUPS = 4
EXPERTS_PER_GROUP = 4
N_EXPERTS = 16
EXPERT_HIDDEN = 256
EPS = 1e-6
NEG_INF = -1e30

T_CTX = BATCH * SEQ
T_LAT = DEC_BATCH * DEC_SEQ
T_ALL = T_CTX + T_LAT
N_MODSETS = 1 + DEC_BATCH
LANES = 128
SUBLANES = 8
PAIR = 2 * HEAD_DIM
ATTN_SCALE = HEAD_DIM ** -0.5

S5_CHUNK = 256
S5_SEQS = T_ALL // S5_CHUNK
S5_ROWS = 56
S5_POW_ROW = 48
S5_LAT0 = BATCH
S5_NLAT = S5_SEQS - BATCH
S5_TB = 16
S5_NBLK = S5_CHUNK // S5_TB
S5_CPLX = 2 * SSM_GROUPS * SSM_STATE
S5_HALF = SSM_GROUPS * SSM_STATE

TM = 512
VMEM_LIMIT = 56 * 1024 * 1024


def _cparams(*sem):
    return pltpu.CompilerParams(dimension_semantics=sem, vmem_limit_bytes=VMEM_LIMIT)


def _modset(i):
    return jnp.where(i < T_CTX // TM, 0, 1 + (i - T_CTX // TM) // (DEC_SEQ // TM))


def _bdot(a, b):
    return jnp.dot(a, b, preferred_element_type=jnp.float32)


def _bdot_t(a, b):
    return lax.dot_general(a, b, (((1,), (1,)), ((), ())), preferred_element_type=jnp.float32)


def _ada_kernel(c_ref, w_ref, b_ref, o_ref):
    c = c_ref[...]
    s = c / (1.0 + jnp.exp(-c))
    o_ref[0] = _bdot(s.astype(jnp.bfloat16), w_ref[0].astype(jnp.bfloat16)) + b_ref[0]


def _adaln_all(cvec8, w_ada, b_ada):
    tn = 1024
    return pl.pallas_call(
        _ada_kernel,
        grid=(DEPTH, 6 * D_MODEL // tn),
        in_specs=[pl.BlockSpec((SUBLANES, D_MODEL), lambda l, j: (0, 0)),
                  pl.BlockSpec((1, D_MODEL, tn), lambda l, j: (l, 0, j)),
                  pl.BlockSpec((1, 1, tn), lambda l, j: (l, 0, j))],
        out_specs=pl.BlockSpec((1, SUBLANES, tn), lambda l, j: (l, 0, j)),
        out_shape=jax.ShapeDtypeStruct((DEPTH, SUBLANES, 6 * D_MODEL), jnp.float32),
        compiler_params=_cparams("arbitrary", "arbitrary"),
        name="adaln",
    )(cvec8, w_ada, b_ada.reshape(DEPTH, 1, 6 * D_MODEL))


def _s5prep_kernel(ar_ref, ai_ref, ldt_ref, br_ref, bi_ref, abar_re, abar_im, bbar_re, bbar_im):
    ar, ai = ar_ref[0], ai_ref[0]
    dt = jnp.exp(ldt_ref[0])
    mag = jnp.exp(ar * dt)
    are = mag * jnp.cos(ai * dt)
    aim = mag * jnp.sin(ai * dt)
    zr, zi = are - 1.0, aim
    den = ar * ar + ai * ai
    cre = (zr * ar + zi * ai) / den
    cim = (zi * ar - zr * ai) / den
    abar_re[0] = are
    abar_im[0] = aim
    br, bi = br_ref[0], bi_ref[0]
    bbar_re[0] = cre * br - cim * bi
    bbar_im[0] = cre * bi + cim * br


def _s5_prep(a_re, a_im, log_dt, b_re, b_im):
    n = DEPTH * 2
    ar = a_re.reshape(n, 1, S5_HALF)
    ai = a_im.reshape(n, 1, S5_HALF)
    ldt = jnp.broadcast_to(log_dt[..., None], (DEPTH, 2, SSM_GROUPS, SSM_STATE)).reshape(n, 1, S5_HALF)
    br = b_re.transpose(0, 1, 4, 2, 3).reshape(n, SSM_GROUP_CH, S5_HALF)
    bi = b_im.transpose(0, 1, 4, 2, 3).reshape(n, SSM_GROUP_CH, S5_HALF)
    vec = jax.ShapeDtypeStruct((n, 1, S5_HALF), jnp.float32)
    mat = jax.ShapeDtypeStruct((n, SSM_GROUP_CH, S5_HALF), jnp.float32)
    vspec = pl.BlockSpec((1, 1, S5_HALF), lambda j: (j, 0, 0))
    mspec = pl.BlockSpec((1, SSM_GROUP_CH, S5_HALF), lambda j: (j, 0, 0))
    return pl.pallas_call(
        _s5prep_kernel,
        grid=(n,),
        in_specs=[vspec, vspec, vspec, mspec, mspec],
        out_specs=[vspec, vspec, mspec, mspec],
        out_shape=(vec, vec, mat, mat),
        compiler_params=_cparams("arbitrary"),
        name="s5prep",
    )(ar, ai, ldt, br, bi)


def _head_rms(blk, gsum, gain):
    sq = blk * blk
    hi = sq.astype(jnp.bfloat16)
    lo = (sq - hi.astype(jnp.float32)).astype(jnp.bfloat16)
    ms = (_bdot(hi, gsum) + _bdot(lo, gsum)) * (1.0 / HEAD_DIM)
    return blk * lax.rsqrt(ms + EPS) * gain


def _rope(blk, cos, sin_signed, first_half):
    partner = jnp.where(first_half, pltpu.roll(blk, LANES - ROPE_F, 1), pltpu.roll(blk, ROPE_F, 1))
    return blk * cos + partner * sin_signed


def _inproj_kernel(x_ref, mod_ref, g_ref, w_ref, qg_ref, kg_ref, gsum_ref, cos_ref, sin_ref,
                   u_ref, qkv_ref, kv32_ref):
    i = pl.program_id(0)
    x = x_ref[...]
    shift = mod_ref[0, :, 0:D_MODEL]
    scale = mod_ref[0, :, D_MODEL:2 * D_MODEL]
    ms = jnp.mean(x * x, axis=-1, keepdims=True)
    h = (x * lax.rsqrt(ms + EPS) * g_ref[...]) * (1.0 + scale) + shift
    proj = _bdot(h.astype(jnp.bfloat16), w_ref[...])
    u_ref[...] = proj[:, 0:SSM_WIDTH]

    c0 = SSM_WIDTH
    qn = proj[:, c0:c0 + NA_WIDTH] * ATTN_SCALE
    kn = proj[:, c0 + NA_WIDTH:c0 + 2 * NA_WIDTH]
    vn = proj[:, c0 + 2 * NA_WIDTH:c0 + 3 * NA_WIDTH]
    c1 = c0 + 3 * NA_WIDTH
    vg = proj[:, c1 + GQA_WIDTH + KV_WIDTH:c1 + GQA_WIDTH + 2 * KV_WIDTH]

    gsum = gsum_ref[...]
    cos, sin = cos_ref[...], sin_ref[...]
    lane = lax.broadcasted_iota(jnp.int32, (1, LANES), 1)
    first_half = (lane & (2 * ROPE_F - 1)) < ROPE_F
    qg_blocks = []
    for p in range(GQA_WIDTH // PAIR):
        blk = proj[:, c1 + p * PAIR:c1 + (p + 1) * PAIR]
        blk = _rope(_head_rms(blk, gsum, qg_ref[...]), cos, sin, first_half)
        qg_blocks.append(blk * ATTN_SCALE)
    kg = proj[:, c1 + GQA_WIDTH:c1 + GQA_WIDTH + KV_WIDTH]
    kg = _rope(_head_rms(kg, gsum, kg_ref[...]), cos, sin, first_half)

    o = 0
    for piece in [qn, kn, vn] + qg_blocks + [kg, vg]:
        w = piece.shape[1]
        qkv_ref[:, o:o + w] = piece.astype(jnp.bfloat16)
        o += w

    @pl.when(i < T_CTX // TM)
    def _():
        kv32_ref[:, 0:NA_WIDTH] = kn
        kv32_ref[:, NA_WIDTH:2 * NA_WIDTH] = vn
        kv32_ref[:, 2 * NA_WIDTH:2 * NA_WIDTH + KV_WIDTH] = kg
        kv32_ref[:, 2 * NA_WIDTH + KV_WIDTH:2 * NA_WIDTH + 2 * KV_WIDTH] = vg


QKV_WIDTH = 3 * NA_WIDTH + GQA_WIDTH + 2 * KV_WIDTH
KV32_WIDTH = 2 * NA_WIDTH + 2 * KV_WIDTH
QN_BLK, KN_BLK, VN_BLK, QG_BLK, KG_BLK, VG_BLK = 0, 3, 6, 9, 12, 13


def _rope_block(i):
    n_ctx = T_CTX // TM
    return jnp.where(i < n_ctx, 0, 1 + (i - n_ctx) % (DEC_SEQ // TM))


def _in_proj(x, mods3, norm_g, w_in_bf, q_gain, k_gain, gsum, rope_cos, rope_sin):
    n_ctx = T_CTX // TM
    return pl.pallas_call(
        _inproj_kernel,
        grid=(T_ALL // TM,),
        in_specs=[pl.BlockSpec((TM, D_MODEL), lambda i: (i, 0)),
                  pl.BlockSpec((1, 1, 6 * D_MODEL), lambda i: (_modset(i), 0, 0)),
                  pl.BlockSpec((1, D_MODEL), lambda i: (0, 0)),
                  pl.BlockSpec((D_MODEL, IN_WIDTH), lambda i: (0, 0)),
                  pl.BlockSpec((1, LANES), lambda i: (0, 0)),
                  pl.BlockSpec((1, LANES), lambda i: (0, 0)),
                  pl.BlockSpec((LANES, LANES), lambda i: (0, 0)),
                  pl.BlockSpec((TM, LANES), lambda i: (_rope_block(i), 0)),
                  pl.BlockSpec((TM, LANES), lambda i: (_rope_block(i), 0))],
        out_specs=[pl.BlockSpec((TM, SSM_WIDTH), lambda i: (i, 0)),
                   pl.BlockSpec((TM, QKV_WIDTH), lambda i: (i, 0)),
                   pl.BlockSpec((TM, KV32_WIDTH), lambda i: (jnp.minimum(i, n_ctx - 1), 0))],
        out_shape=(jax.ShapeDtypeStruct((T_ALL, SSM_WIDTH), jnp.float32),
                   jax.ShapeDtypeStruct((T_ALL, QKV_WIDTH), jnp.bfloat16),
                   jax.ShapeDtypeStruct((T_CTX, KV32_WIDTH), jnp.float32)),
        compiler_params=_cparams("arbitrary"),
        name="in_proj",
    )(x, mods3, norm_g, w_in_bf, q_gain, k_gain, gsum, rope_cos, rope_sin)


def _s5scan_kernel(u_ref, b_ref, a_ref, c_ref, y_ref, p_ref, fin_ref, hbuf):
    d = pl.program_id(0)
    k = pl.program_id(1)
    rows = S5_TB * S5_ROWS
    a_re = jnp.broadcast_to(a_ref[0, :, 0:S5_HALF], (SUBLANES, S5_HALF))
    a_im = jnp.broadcast_to(a_ref[0, :, S5_HALF:S5_CPLX], (SUBLANES, S5_HALF))
    ub = u_ref[...].astype(jnp.bfloat16)

    def run(forward):
        data0 = S5_ROWS if forward else 0
        prev0 = 0 if forward else rows
        last0 = rows if forward else 0

        @pl.when(k == 0)
        def _():
            r = lax.broadcasted_iota(jnp.int32, (S5_ROWS, S5_CPLX), 0)
            l = lax.broadcasted_iota(jnp.int32, (S5_ROWS, S5_CPLX), 1)
            hbuf[prev0:prev0 + S5_ROWS, :] = jnp.where((r == S5_POW_ROW) & (l < S5_HALF), 1.0, 0.0)

        @pl.when(k > 0)
        def _():
            hbuf[prev0:prev0 + S5_ROWS, :] = hbuf[last0:last0 + S5_ROWS, :]

        hbuf[data0:data0 + rows, :] = _bdot(ub, b_ref[0])

        def step(i, carry):
            t = i if forward else S5_TB - 1 - i
            cur = pl.multiple_of(data0 + t * S5_ROWS, SUBLANES)
            prv = pl.multiple_of(cur - S5_ROWS if forward else cur + S5_ROWS, SUBLANES)
            for st in range(S5_ROWS // SUBLANES):
                rc = pl.ds(cur + st * SUBLANES, SUBLANES)
                rp = pl.ds(prv + st * SUBLANES, SUBLANES)
                hr = hbuf[rp, 0:S5_HALF]
                hi = hbuf[rp, S5_HALF:S5_CPLX]
                nr = a_re * hr - a_im * hi + hbuf[rc, 0:S5_HALF]
                ni = a_re * hi + a_im * hr + hbuf[rc, S5_HALF:S5_CPLX]
                hbuf[rc, 0:S5_HALF] = nr
                hbuf[rc, S5_HALF:S5_CPLX] = ni
            return carry

        lax.fori_loop(0, S5_TB, step, 0)
        y_ref[0] = _bdot(hbuf[data0:data0 + rows, :].astype(jnp.bfloat16), c_ref[0])
        for t in range(S5_TB):
            r0 = data0 + t * S5_ROWS + S5_POW_ROW
            p_ref[0, t:t + 1, :] = hbuf[r0:r0 + 1, :]

        @pl.when(k == S5_NBLK - 1)
        def _():
            fin_ref[0] = hbuf[last0:last0 + S5_ROWS, :]

    @pl.when(d == 0)
    def _():
        run(True)

    @pl.when(d == 1)
    def _():
        run(False)


def _s5_scan(u_tm, bblk, abar, cblk):
    rows = S5_TB * S5_ROWS

    def tblk(d, k):
        return jnp.where(d == 0, k, S5_NBLK - 1 - k)

    return pl.pallas_call(
        _s5scan_kernel,
        grid=(2, S5_NBLK),
        in_specs=[pl.BlockSpec((rows, SSM_WIDTH), lambda d, k: (tblk(d, k), 0)),
                  pl.BlockSpec((1, SSM_WIDTH, S5_CPLX), lambda d, k: (d, 0, 0)),
                  pl.BlockSpec((1, 1, S5_CPLX), lambda d, k: (d, 0, 0)),
                  pl.BlockSpec((1, S5_CPLX, SSM_WIDTH), lambda d, k: (d, 0, 0))],
        out_specs=[pl.BlockSpec((1, rows, SSM_WIDTH), lambda d, k: (d, tblk(d, k), 0)),
                   pl.BlockSpec((1, S5_TB, S5_CPLX), lambda d, k: (d, tblk(d, k), 0)),
                   pl.BlockSpec((1, S5_ROWS, S5_CPLX), lambda d, k: (d, 0, 0))],
        out_shape=(jax.ShapeDtypeStruct((2, S5_CHUNK * S5_ROWS, SSM_WIDTH), jnp.float32),
                   jax.ShapeDtypeStruct((2, S5_CHUNK, S5_CPLX), jnp.float32),
                   jax.ShapeDtypeStruct((2, S5_ROWS, S5_CPLX), jnp.float32)),
        scratch_shapes=[pltpu.VMEM((rows + S5_ROWS, S5_CPLX), jnp.float32)],
        compiler_params=_cparams("arbitrary", "arbitrary"),
        name="s5_scan",
    )(u_tm, bblk, abar, cblk)


def _cmul(pr, pi, cr, ci):
    return pr * cr - pi * ci, pr * ci + pi * cr


def _s5fin_kernel(u_ref, yf_ref, yb_ref, p_ref, fin_ref, h0_ref, c_ref, d_ref, wg_ref, bg_ref,
                  o_ref, carry, hc, ybuf):
    k = pl.program_id(0)
    nchunk = DEC_SEQ // S5_CHUNK

    @pl.when(k == 0)
    def _():
        for d in range(2):
            t_full = S5_CHUNK - 1 if d == 0 else 0
            pr = p_ref[d, t_full:t_full + 1, 0:S5_HALF]
            pi = p_ref[d, t_full:t_full + 1, S5_HALF:S5_CPLX]
            for b in range(DEC_BATCH):
                cr = h0_ref[d, b:b + 1, 0:S5_HALF]
                ci = h0_ref[d, b:b + 1, S5_HALF:S5_CPLX]
                order = range(nchunk) if d == 0 else range(nchunk - 1, -1, -1)
                for c in order:
                    row = b * nchunk + c
                    carry[d, row:row + 1, 0:S5_HALF] = cr
                    carry[d, row:row + 1, S5_HALF:S5_CPLX] = ci
                    fr = fin_ref[d, S5_LAT0 + row:S5_LAT0 + row + 1, 0:S5_HALF]
                    fi = fin_ref[d, S5_LAT0 + row:S5_LAT0 + row + 1, S5_HALF:S5_CPLX]
                    mr, mi = _cmul(pr, pi, cr, ci)
                    cr, ci = fr + mr, fi + mi

    ybuf[...] = yf_ref[0] + yb_ref[0] + u_ref[...] * d_ref[...]
    for d in range(2):
        cr = carry[d, :, 0:S5_HALF]
        ci = carry[d, :, S5_HALF:S5_CPLX]
        for t in range(S5_TB):
            pr = p_ref[d, pl.ds(k * S5_TB + t, 1), 0:S5_HALF]
            pi = p_ref[d, pl.ds(k * S5_TB + t, 1), S5_HALF:S5_CPLX]
            mr, mi = _cmul(pr, pi, cr, ci)
            hc[t * S5_NLAT:(t + 1) * S5_NLAT, 0:S5_HALF] = mr.astype(jnp.bfloat16)
            hc[t * S5_NLAT:(t + 1) * S5_NLAT, S5_HALF:S5_CPLX] = mi.astype(jnp.bfloat16)
        corr = _bdot(hc[...], c_ref[d])
        ybuf[:, S5_LAT0:S5_LAT0 + S5_NLAT, :] += corr.reshape(S5_TB, S5_NLAT, SSM_WIDTH)

    y = ybuf[...].reshape(S5_TB * S5_ROWS, SSM_WIDTH)
    yg = 0.5 * y * (1.0 + jnp.tanh(math.sqrt(2.0 / math.pi) * (y + 0.044715 * (y * y * y))))
    z = _bdot(yg.astype(jnp.bfloat16), wg_ref[...]) + bg_ref[...]
    out = yg / (1.0 + jnp.exp(-z))
    o_ref[...] = out.reshape(S5_TB, S5_ROWS, SSM_WIDTH).astype(jnp.bfloat16)


def _s5_finish(u_tm3, y_dir4, powers, finals, h0, cblk, ssm_d, w_glu_bf, b_glu):
    blk3 = (S5_TB, S5_ROWS, SSM_WIDTH)
    return pl.pallas_call(
        _s5fin_kernel,
        grid=(S5_NBLK,),
        in_specs=[pl.BlockSpec(blk3, lambda k: (k, 0, 0)),
                  pl.BlockSpec((1,) + blk3, lambda k: (0, k, 0, 0)),
                  pl.BlockSpec((1,) + blk3, lambda k: (1, k, 0, 0)),
                  pl.BlockSpec((2, S5_CHUNK, S5_CPLX), lambda k: (0, 0, 0)),
                  pl.BlockSpec((2, S5_ROWS, S5_CPLX), lambda k: (0, 0, 0)),
                  pl.BlockSpec((2, DEC_BATCH, S5_CPLX), lambda k: (0, 0, 0)),
                  pl.BlockSpec((2, S5_CPLX, SSM_WIDTH), lambda k: (0, 0, 0)),
                  pl.BlockSpec((1, SSM_WIDTH), lambda k: (0, 0)),
                  pl.BlockSpec((SSM_WIDTH, SSM_WIDTH), lambda k: (0, 0)),
                  pl.BlockSpec((1, SSM_WIDTH), lambda k: (0, 0))],
        out_specs=pl.BlockSpec(blk3, lambda k: (k, 0, 0)),
        out_shape=jax.ShapeDtypeStruct((S5_CHUNK, S5_ROWS, SSM_WIDTH), jnp.bfloat16),
        scratch_shapes=[pltpu.VMEM((2, S5_NLAT, S5_CPLX), jnp.float32),
                        pltpu.VMEM((S5_TB * S5_NLAT, S5_CPLX), jnp.bfloat16),
                        pltpu.VMEM(blk3, jnp.float32)],
        compiler_params=_cparams("arbitrary"),
        name="s5_finish",
    )(u_tm3, y_dir4, y_dir4, powers, finals, h0, cblk, ssm_d, w_glu_bf, b_glu)


def _pair_attention(q, kvs, biases):
    lane = lax.broadcasted_iota(jnp.int32, (1, PAIR), 1)
    outs = []
    for a in range(2):
        sel = (lane < HEAD_DIM) if a == 0 else (lane >= HEAD_DIM)
        qa = jnp.where(sel, q, jnp.zeros_like(q))
        ss = []
        for (kk, _), bias in zip(kvs, biases):
            s = _bdot_t(qa, kk)
            if bias is not None:
                s = s + bias[a]
            ss.append(s)
        m = ss[0].max(axis=-1, keepdims=True)
        for s in ss[1:]:
            m = jnp.maximum(m, s.max(axis=-1, keepdims=True))
        den = None
        acc = None
        for s, (_, vv) in zip(ss, kvs):
            p = jnp.exp(s - m)
            l = p.sum(axis=-1, keepdims=True)
            o = _bdot(p.astype(jnp.bfloat16), vv)
            den = l if den is None else den + l
            acc = o if acc is None else acc + o
        outs.append(acc / den)
    return jnp.where(lane < HEAD_DIM, outs[0], outs[1])


def _attn_ctx_kernel(q_ref, k_ref, v_ref, o_ref):
    o_ref[...] = _pair_attention(q_ref[...], [(k_ref[...], v_ref[...])], [None]).astype(jnp.bfloat16)


def _attn_ctx(qkv):
    npair = NA_WIDTH // PAIR
    return pl.pallas_call(
        _attn_ctx_kernel,
        grid=(BATCH, 2 * npair),
        in_specs=[pl.BlockSpec((SEQ, PAIR), lambda b, p: (b, jnp.where(p < npair, QN_BLK + p, QG_BLK + p - npair))),
                  pl.BlockSpec((SEQ, PAIR), lambda b, p: (b, jnp.where(p < npair, KN_BLK + p, KG_BLK))),
                  pl.BlockSpec((SEQ, PAIR), lambda b, p: (b, jnp.where(p < npair, VN_BLK + p, VG_BLK)))],
        out_specs=pl.BlockSpec((SEQ, PAIR), lambda b, p: (b, p)),
        out_shape=jax.ShapeDtypeStruct((T_CTX, NA_WIDTH + GQA_WIDTH), jnp.bfloat16),
        compiler_params=_cparams("arbitrary", "arbitrary"),
        name="attn_ctx",
    )(qkv, qkv, qkv)


TQ = 256


def _gqa_lat_kernel(q_ref, k_ref, v_ref, ck_ref, cv_ref, o_ref):
    kvs = [(ck_ref[0], cv_ref[0]), (k_ref[...], v_ref[...])]
    o_ref[...] = _pair_attention(q_ref[...], kvs, [None, None]).astype(jnp.bfloat16)


def _gqa_lat(qkv, ck, cv):
    nq = DEC_SEQ // TQ
    ctx_blocks = T_CTX // DEC_SEQ
    return pl.pallas_call(
        _gqa_lat_kernel,
        grid=(DEC_BATCH, GQA_WIDTH // PAIR, nq),
        in_specs=[pl.BlockSpec((TQ, PAIR), lambda b, p, i: (T_CTX // TQ + b * nq + i, QG_BLK + p)),
                  pl.BlockSpec((DEC_SEQ, PAIR), lambda b, p, i: (ctx_blocks + b, KG_BLK)),
                  pl.BlockSpec((DEC_SEQ, PAIR), lambda b, p, i: (ctx_blocks + b, VG_BLK)),
                  pl.BlockSpec((1, PAST_LEN, PAIR), lambda b, p, i: (b, 0, 0)),
                  pl.BlockSpec((1, PAST_LEN, PAIR), lambda b, p, i: (b, 0, 0))],
        out_specs=pl.BlockSpec((TQ, PAIR), lambda b, p, i: (b * nq + i, p)),
        out_shape=jax.ShapeDtypeStruct((T_LAT, GQA_WIDTH), jnp.bfloat16),
        compiler_params=_cparams("arbitrary", "arbitrary", "arbitrary"),
        name="gqa_lat",
    )(qkv, qkv, qkv, ck, cv)


NA_WIN = NA_KH * GRID_W


def _na_lat_kernel(q_ref, k_ref, v_ref, ck_ref, cv_ref, bias_ref, o_ref):
    ck, cv = ck_ref[0, 0], cv_ref[0, 0]

    def row(r, carry):
        rs = jnp.clip(r - NA_KH // 2, 0, GRID_H - NA_KH)
        off = r - rs
        q = q_ref[pl.ds(pl.multiple_of(r * GRID_W, GRID_W), GRID_W), :]
        k0 = pl.multiple_of(rs * GRID_W, GRID_W)
        ks = k_ref[pl.ds(k0, NA_WIN), :]
        vs = v_ref[pl.ds(k0, NA_WIN), :]
        bias = [bias_ref[0, 0, off], bias_ref[0, 1, off]]
        o = _pair_attention(q, [(ks, vs), (ck, cv)], [bias, None])
        o_ref[pl.ds(pl.multiple_of(r * GRID_W, GRID_W), GRID_W), :] = o.astype(jnp.bfloat16)
        return carry

    lax.fori_loop(0, GRID_H, row, 0)


def _na_lat(qkv, ck, cv, bias):
    ctx_blocks = T_CTX // DEC_SEQ
    npair = NA_WIDTH // PAIR
    return pl.pallas_call(
        _na_lat_kernel,
        grid=(DEC_BATCH, npair),
        in_specs=[pl.BlockSpec((DEC_SEQ, PAIR), lambda b, p: (ctx_blocks + b, QN_BLK + p)),
                  pl.BlockSpec((DEC_SEQ, PAIR), lambda b, p: (ctx_blocks + b, KN_BLK + p)),
                  pl.BlockSpec((DEC_SEQ, PAIR), lambda b, p: (ctx_blocks + b, VN_BLK + p)),
                  pl.BlockSpec((1, 1, PAST_LEN, PAIR), lambda b, p: (b, p, 0, 0)),
                  pl.BlockSpec((1, 1, PAST_LEN, PAIR), lambda b, p: (b, p, 0, 0)),
                  pl.BlockSpec((1, 2, NA_KH, GRID_W, NA_WIN), lambda b, p: (p, 0, 0, 0, 0))],
        out_specs=pl.BlockSpec((DEC_SEQ, PAIR), lambda b, p: (b, p)),
        out_shape=jax.ShapeDtypeStruct((T_LAT, NA_WIDTH), jnp.bfloat16),
        compiler_params=_cparams("arbitrary", "arbitrary"),
        name="na_lat",
    )(qkv, qkv, qkv, ck, cv, bias)


def _na_bias_table(rpb):
    off = jnp.arange(NA_KH)
    i = jnp.arange(NA_KH)
    dr = i[None, :] - off[:, None] + (NA_KH - 1)
    cq = jnp.arange(GRID_W)
    ck = jnp.arange(GRID_W)
    cs = jnp.clip(cq - NA_KW // 2, 0, GRID_W - NA_KW)
    col_in = (ck[None, :] >= cs[:, None]) & (ck[None, :] < cs[:, None] + NA_KW)
    dc = jnp.clip(ck[None, :] - cq[:, None] + NA_KW - 1, 0, 2 * NA_KW - 2)
    b = rpb.astype(jnp.float32)[:, dr][:, :, :, dc]
    b = jnp.where(col_in[None, None, None], b, NEG_INF)
    b = b.transpose(0, 1, 3, 2, 4).reshape(NA_HEADS, NA_KH, GRID_W, NA_WIN)
    return b.reshape(NA_HEADS // 2, 2, NA_KH, GRID_W, NA_WIN)


GATE_LANE0 = N_GROUPS
_EPG_SHIFT = EXPERTS_PER_GROUP.bit_length() - 1
assert 1 << _EPG_SHIFT == EXPERTS_PER_GROUP


def _route(logits):
    lane_i = lax.broadcasted_iota(jnp.int32, logits.shape, 1)
    lane = lane_i.astype(jnp.float32)
    big = float(LANES)
    is_g = lane_i < N_GROUPS
    lg = jnp.where(is_g, logits, NEG_INF)
    mg = lg.max(axis=-1, keepdims=True)
    pg_top = 1.0 / jnp.where(is_g, jnp.exp(lg - mg), 0.0).sum(axis=-1, keepdims=True)
    g_idx = jnp.where(is_g & (lg == mg), lane, big).min(axis=-1, keepdims=True)
    in_e = (lane_i >= GATE_LANE0) & (lane_i < GATE_LANE0 + N_EXPERTS)
    lane_group = jnp.where(in_e, ((lane_i - GATE_LANE0) >> _EPG_SHIFT).astype(jnp.float32), -1.0)
    sel = lane_group == g_idx
    le = jnp.where(sel, logits, NEG_INF)
    me = le.max(axis=-1, keepdims=True)
    ex = jnp.where(sel, jnp.exp(le - me), 0.0)
    pe = ex / ex.sum(axis=-1, keepdims=True)
    p1 = pe.max(axis=-1, keepdims=True)
    i1 = jnp.where(sel & (pe == p1), lane, big).min(axis=-1, keepdims=True)
    rest = sel & (lane != i1)
    pr = jnp.where(rest, pe, -1.0)
    p2 = pr.max(axis=-1, keepdims=True)
    i2 = jnp.where(rest & (pr == p2), lane, big).min(axis=-1, keepdims=True)
    tot = p1 + p2
    within = jnp.where(lane == i1, p1 / tot, 0.0) + jnp.where(lane == i2, p2 / tot, 0.0)
    return pg_top * within


def _outproj_kernel(x_ref, ys_ref, yn_ref, yg_ref, mod_ref, g_ref, w_ref, wr_hi_ref, wr_lo_ref, br_ref,
                    x1_ref, h2_ref, gates_ref):
    mix = (_bdot(ys_ref[...], w_ref[0:SSM_WIDTH, :])
           + _bdot(yn_ref[...], w_ref[SSM_WIDTH:SSM_WIDTH + NA_WIDTH, :])
           + _bdot(yg_ref[...], w_ref[SSM_WIDTH + NA_WIDTH:, :]))
    gate1 = mod_ref[0, :, 2 * D_MODEL:3 * D_MODEL]
    shift2 = mod_ref[0, :, 3 * D_MODEL:4 * D_MODEL]
    scale2 = mod_ref[0, :, 4 * D_MODEL:5 * D_MODEL]
    x1 = x_ref[...] + gate1 * mix
    x1_ref[...] = x1
    ms = jnp.mean(x1 * x1, axis=-1, keepdims=True)
    h2 = (x1 * lax.rsqrt(ms + EPS) * g_ref[...]) * (1.0 + scale2) + shift2
    h_hi = h2.astype(jnp.bfloat16)
    h2_ref[...] = h_hi
    h_lo = (h2 - h_hi.astype(jnp.float32)).astype(jnp.bfloat16)
    logits = (_bdot(h_hi, wr_hi_ref[...]) + _bdot(h_lo, wr_hi_ref[...]) + _bdot(h_hi, wr_lo_ref[...])
              + br_ref[...])
    gates_ref[...] = _route(logits)


def _out_proj(x, y_ssm, y_na, y_g, mods3, norm_g, w_out_bf, wr_hi, wr_lo, b_router):
    return pl.pallas_call(
        _outproj_kernel,
        grid=(T_ALL // TM,),
        in_specs=[pl.BlockSpec((TM, D_MODEL), lambda i: (i, 0)),
                  pl.BlockSpec((TM, SSM_WIDTH), lambda i: (i, 0)),
                  pl.BlockSpec((TM, NA_WIDTH), lambda i: (i, 0)),
                  pl.BlockSpec((TM, GQA_WIDTH), lambda i: (i, 0)),
                  pl.BlockSpec((1, 1, 6 * D_MODEL), lambda i: (_modset(i), 0, 0)),
                  pl.BlockSpec((1, D_MODEL), lambda i: (0, 0)),
                  pl.BlockSpec((D_MODEL, D_MODEL), lambda i: (0, 0)),
                  pl.BlockSpec((D_MODEL, LANES), lambda i: (0, 0)),
                  pl.BlockSpec((D_MODEL, LANES), lambda i: (0, 0)),
                  pl.BlockSpec((1, LANES), lambda i: (0, 0))],
        out_specs=[pl.BlockSpec((TM, D_MODEL), lambda i: (i, 0)),
                   pl.BlockSpec((TM, D_MODEL), lambda i: (i, 0)),
                   pl.BlockSpec((TM, LANES), lambda i: (i, 0))],
        out_shape=(jax.ShapeDtypeStruct((T_ALL, D_MODEL), jnp.float32),
                   jax.ShapeDtypeStruct((T_ALL, D_MODEL), jnp.bfloat16),
                   jax.ShapeDtypeStruct((T_ALL, LANES), jnp.float32)),
        compiler_params=_cparams("arbitrary"),
        name="out_proj",
    )(x, y_ssm, y_na, y_g, mods3, norm_g, w_out_bf, wr_hi, wr_lo, b_router)


def _moe_kernel(h_ref, gates_ref, x1_ref, mod_ref, wg_ref, wu_ref, wd_ref, o_ref, acc):
    e = pl.program_id(1)

    @pl.when(e == 0)
    def _():
        acc[...] = jnp.zeros_like(acc)

    h = h_ref[...]
    a = _bdot(h, wg_ref[0])
    b = _bdot(h, wu_ref[0])
    lane = lax.broadcasted_iota(jnp.int32, (1, LANES), 1)
    gate = jnp.where(lane == GATE_LANE0 + e, gates_ref[...], 0.0).sum(axis=-1, keepdims=True)
    act = (a / (1.0 + jnp.exp(-a))) * b * gate
    acc[...] += _bdot(act.astype(jnp.bfloat16), wd_ref[0])

    @pl.when(e == N_EXPERTS - 1)
    def _():
        gate2 = mod_ref[0, :, 5 * D_MODEL:6 * D_MODEL]
        o_ref[...] = x1_ref[...] + gate2 * acc[...]


def _moe(h2, gates, x1, mods3, wg_bf, wu_bf, wd_bf):
    return pl.pallas_call(
        _moe_kernel,
        grid=(T_ALL // TM, N_EXPERTS),
        in_specs=[pl.BlockSpec((TM, D_MODEL), lambda i, e: (i, 0)),
                  pl.BlockSpec((TM, LANES), lambda i, e: (i, 0)),
                  pl.BlockSpec((TM, D_MODEL), lambda i, e: (i, 0)),
                  pl.BlockSpec((1, 1, 6 * D_MODEL), lambda i, e: (_modset(i), 0, 0)),
                  pl.BlockSpec((1, D_MODEL, EXPERT_HIDDEN), lambda i, e: (e, 0, 0)),
                  pl.BlockSpec((1, D_MODEL, EXPERT_HIDDEN), lambda i, e: (e, 0, 0)),
                  pl.BlockSpec((1, EXPERT_HIDDEN, D_MODEL), lambda i, e: (e, 0, 0))],
        out_specs=pl.BlockSpec((TM, D_MODEL), lambda i, e: (i, 0)),
        out_shape=jax.ShapeDtypeStruct((T_ALL, D_MODEL), jnp.float32),
        scratch_shapes=[pltpu.VMEM((TM, D_MODEL), jnp.float32)],
        compiler_params=_cparams("arbitrary", "arbitrary"),
        name="moe",
    )(h2, gates, x1, mods3, wg_bf, wu_bf, wd_bf)


def _final_kernel(x_ref, g_ref, o_ref):
    x = x_ref[...]
    ms = jnp.mean(x * x, axis=-1, keepdims=True)
    o_ref[...] = x * lax.rsqrt(ms + EPS) * g_ref[...]


def _final_norm(x, g):
    return pl.pallas_call(
        _final_kernel,
        grid=(T_ALL // TM,),
        in_specs=[pl.BlockSpec((TM, D_MODEL), lambda i: (i, 0)),
                  pl.BlockSpec((1, D_MODEL), lambda i: (0, 0))],
        out_specs=pl.BlockSpec((TM, D_MODEL), lambda i: (i, 0)),
        out_shape=jax.ShapeDtypeStruct((T_ALL, D_MODEL), jnp.float32),
        compiler_params=_cparams("arbitrary"),
        name="final_norm",
    )(x, g)


_QG_ORDER = np.array([0, 3, 1, 4, 2, 5])


def _permute_qg_columns(w_in):
    c1 = SSM_WIDTH + 3 * NA_WIDTH
    qg = w_in[..., c1:c1 + GQA_WIDTH].reshape(*w_in.shape[:-1], GQA_HEADS, HEAD_DIM)
    qg = qg[..., _QG_ORDER, :].reshape(*w_in.shape[:-1], GQA_WIDTH)
    return jnp.concatenate([w_in[..., :c1], qg, w_in[..., c1 + GQA_WIDTH:]], axis=-1)


def _permute_yg_rows(w_out):
    r0 = SSM_WIDTH + NA_WIDTH
    yg = w_out[:, r0:, :].reshape(DEPTH, GQA_HEADS, HEAD_DIM, D_MODEL)[:, _QG_ORDER]
    return jnp.concatenate([w_out[:, :r0, :], yg.reshape(DEPTH, GQA_WIDTH, D_MODEL)], axis=1)


def _rope_tables():
    t = jnp.arange(DEC_SEQ)
    row = (t // GRID_W).astype(jnp.float32)
    col = (t % GRID_W).astype(jnp.float32)
    freqs = ROPE_THETA ** (-jnp.arange(ROPE_F, dtype=jnp.float32) / ROPE_F)
    ang_r = row[:, None] * freqs
    ang_c = col[:, None] * freqs
    ang = jnp.concatenate([ang_r, ang_r, ang_c, ang_c], axis=1)
    cos = jnp.cos(ang)
    sin = jnp.sin(ang)
    sign = jnp.tile(jnp.concatenate([-jnp.ones(ROPE_F), jnp.ones(ROPE_F)]), 2).astype(jnp.float32)
    sin = sin * sign
    cos = jnp.tile(cos, (1, 2))
    sin = jnp.tile(sin, (1, 2))
    cos = jnp.concatenate([jnp.ones((TM, LANES), jnp.float32), cos], axis=0)
    sin = jnp.concatenate([jnp.zeros((TM, LANES), jnp.float32), sin], axis=0)
    return cos, sin


def _pair_layout(cache):
    b, l, h, n, dh = cache.shape
    c = cache.reshape(b, l, h // 2, 2, n, dh).transpose(1, 0, 2, 4, 3, 5)
    return c.reshape(l, b, h // 2, n, 2 * dh).astype(jnp.bfloat16)


def _block_diag_b(bbar):
    bb = bbar.reshape(DEPTH, 2, SSM_GROUP_CH, SSM_GROUPS, SSM_STATE)
    eye = jnp.eye(SSM_GROUPS, dtype=bbar.dtype)
    out = eye[None, None, :, None, :, None] * bb.transpose(0, 1, 3, 2, 4)[:, :, :, :, None, :]
    return out.reshape(DEPTH, 2, SSM_WIDTH, S5_HALF)


def _block_diag_c(c):
    eye = jnp.eye(SSM_GROUPS, dtype=c.dtype)
    out = eye[None, None, :, None, :, None] * c.transpose(0, 1, 2, 4, 3)[:, :, :, :, None, :]
    return out.reshape(DEPTH, 2, S5_HALF, SSM_WIDTH)


def kernel(x_prompt, x_sample, cache_na_k, cache_na_v, cache_gqa_k, cache_gqa_v, state_ssm, c, c_ctx, norm_mix_g, norm_ffn_g, w_ada, b_ada, w_in, ssm_a_re, ssm_a_im, ssm_log_dt, ssm_b_re, ssm_b_im, ssm_c_re, ssm_c_im, ssm_d, w_glu, b_glu, na_rpb, q_norm_g, k_norm_g, w_out, moe_w_group, moe_b_group, moe_w_expert, moe_b_expert, moe_w_gate, moe_w_up, moe_w_down, final_norm_g):
    f32, bf16 = jnp.float32, jnp.bfloat16
    x = jnp.concatenate([x_prompt.reshape(T_CTX, D_MODEL), x_sample.reshape(T_LAT, D_MODEL)], axis=0)

    cvec8 = jnp.concatenate([c_ctx[None, :], c, jnp.zeros((SUBLANES - N_MODSETS, D_MODEL), f32)], axis=0)
    mods_all = _adaln_all(cvec8, w_ada, b_ada)

    w_in_bf = _permute_qg_columns(w_in).astype(bf16)
    w_out_bf = _permute_yg_rows(w_out).astype(bf16)
    w_glu_bf = w_glu.astype(bf16)
    wg_bf, wu_bf, wd_bf = moe_w_gate.astype(bf16), moe_w_up.astype(bf16), moe_w_down.astype(bf16)
    w_router = jnp.concatenate(
        [moe_w_group, moe_w_expert, jnp.zeros((DEPTH, D_MODEL, LANES - N_GROUPS - N_EXPERTS), f32)], axis=-1)
    wr_hi = w_router.astype(bf16)
    wr_lo = (w_router - wr_hi.astype(f32)).astype(bf16)
    b_router = jnp.concatenate(
        [moe_b_group, moe_b_expert, jnp.zeros((DEPTH, LANES - N_GROUPS - N_EXPERTS), f32)], axis=-1)
    q_gain = jnp.tile(q_norm_g, (1, 2))
    k_gain = jnp.tile(k_norm_g, (1, 2))
    lane = np.arange(LANES)
    gsum = jnp.asarray((lane[:, None] // HEAD_DIM) == (lane[None, :] // HEAD_DIM), bf16)
    rope_cos, rope_sin = _rope_tables()

    abar_re, abar_im, bbar_re, bbar_im = _s5_prep(ssm_a_re, ssm_a_im, ssm_log_dt, ssm_b_re, ssm_b_im)
    abar = jnp.concatenate([abar_re, abar_im], axis=-1).reshape(DEPTH, 2, 1, S5_CPLX)
    bblk = jnp.concatenate([_block_diag_b(bbar_re), _block_diag_b(bbar_im)], axis=-1).astype(bf16)
    cblk = jnp.concatenate([_block_diag_c(ssm_c_re), -_block_diag_c(ssm_c_im)], axis=2).astype(bf16)
    h0 = state_ssm.transpose(1, 2, 0, 5, 3, 4).reshape(DEPTH, 2, DEC_BATCH, S5_CPLX)

    ck_na, cv_na = _pair_layout(cache_na_k), _pair_layout(cache_na_v)
    ck_g, cv_g = _pair_layout(cache_gqa_k)[:, :, 0], _pair_layout(cache_gqa_v)[:, :, 0]
    na_bias = jax.vmap(_na_bias_table)(na_rpb)

    kv32_l, fin_l = [], []
    for l in range(DEPTH):
        mods3 = mods_all[l, :N_MODSETS].reshape(N_MODSETS, 1, 6 * D_MODEL)
        u, qkv, kv32 = _in_proj(x, mods3, norm_mix_g[l][None], w_in_bf[l], q_gain[l][None], k_gain[l][None],
                                gsum, rope_cos, rope_sin)
        kv32_l.append(kv32)

        u_tm = jnp.pad(u.reshape(S5_SEQS, S5_CHUNK, SSM_WIDTH).transpose(1, 0, 2),
                       ((0, 0), (0, S5_ROWS - S5_SEQS), (0, 0)))
        y_dir, powers, finals = _s5_scan(u_tm.reshape(S5_CHUNK * S5_ROWS, SSM_WIDTH), bblk[l], abar[l], cblk[l])
        fin_l.append(finals[:, :BATCH])
        y_tm = _s5_finish(u_tm, y_dir.reshape(2, S5_CHUNK, S5_ROWS, SSM_WIDTH), powers, finals, h0[l], cblk[l],
                          ssm_d[l][None], w_glu_bf[l], b_glu[l][None])
        y_ssm = y_tm[:, :S5_SEQS].transpose(1, 0, 2).reshape(T_ALL, SSM_WIDTH)

        y_ctx = _attn_ctx(qkv)
        y_g_lat = _gqa_lat(qkv, ck_g[l], cv_g[l])
        y_na_lat = _na_lat(qkv, ck_na[l], cv_na[l], na_bias[l])
        y_na = jnp.concatenate([y_ctx[:, :NA_WIDTH], y_na_lat], axis=0)
        y_g = jnp.concatenate([y_ctx[:, NA_WIDTH:], y_g_lat], axis=0)

        x1, h2, gates = _out_proj(x, y_ssm, y_na, y_g, mods3, norm_ffn_g[l][None], w_out_bf[l],
                                  wr_hi[l], wr_lo[l], b_router[l][None])
        x = _moe(h2, gates, x1, mods3, wg_bf[l], wu_bf[l], wd_bf[l])

    y = _final_norm(x, final_norm_g[None])
    y_prompt = y[:T_CTX].reshape(BATCH, SEQ, D_MODEL)
    y_sample = y[T_CTX:].reshape(DEC_BATCH, DEC_SEQ, D_MODEL)

    kv32 = jnp.stack(kv32_l, axis=0).reshape(DEPTH, BATCH, SEQ, KV32_WIDTH)

    def heads(t, n_heads):
        return t.reshape(DEPTH, BATCH, SEQ, n_heads, HEAD_DIM).transpose(1, 0, 3, 2, 4)

    new_na_k = heads(kv32[..., 0:NA_WIDTH], NA_HEADS)
    new_na_v = heads(kv32[..., NA_WIDTH:2 * NA_WIDTH], NA_HEADS)
    new_gqa_k = heads(kv32[..., 2 * NA_WIDTH:2 * NA_WIDTH + KV_WIDTH], GQA_KV_HEADS)
    new_gqa_v = heads(kv32[..., 2 * NA_WIDTH + KV_WIDTH:], GQA_KV_HEADS)
    fin = jnp.stack(fin_l, axis=0).reshape(DEPTH, 2, BATCH, 2, SSM_GROUPS, SSM_STATE)
    new_state = fin.transpose(2, 0, 1, 4, 5, 3)
    return (y_prompt, y_sample, new_na_k, new_na_v, new_gqa_k, new_gqa_v, new_state)
```

```python
import functools
import math

import jax
import jax.numpy as jnp
import numpy as np
from jax import lax
from jax.experimental import pallas as pl
from jax.experimental.pallas import tpu as pltpu

D_MODEL = 1024
BATCH = 16
SEQ = 256
DEPTH = 4
DEC_BATCH = 2
DEC_SEQ = 4096
PAST_LEN = 512
GRID_W = 64
GRID_H = DEC_SEQ // GRID_W
HEAD_DIM = 64
SSM_WIDTH = 256
SSM_GROUP_CH = 16
SSM_GROUPS = 16
SSM_STATE = 64
NA_HEADS = 6
NA_WIDTH = 384
NA_KH = 8
NA_KW = 16
GQA_HEADS = 6
GQA_KV_HEADS = 2
GQA_REP = 3
GQA_WIDTH = 384
KV_WIDTH = 128
IN_WIDTH = 2048
ROPE_THETA = 10000.0
ROPE_F = 16
N_GROUPS = 4
EXPERTS_PER_GROUP = 4
N_EXPERTS = 16
EXPERT_HIDDEN = 256
EPS = 1e-6
NEG_INF = -1e30

T_CTX = BATCH * SEQ
T_LAT = DEC_BATCH * DEC_SEQ
T_ALL = T_CTX + T_LAT
N_MODSETS = 1 + DEC_BATCH
LANES = 128
SUBLANES = 8
PAIR = 2 * HEAD_DIM
ATTN_SCALE = HEAD_DIM ** -0.5

S5_CHUNK = 256
S5_SEQS = T_ALL // S5_CHUNK
S5_ROWS = 56
S5_POW_ROW = 48
S5_LAT0 = BATCH
S5_NLAT = S5_SEQS - BATCH
S5_TB = 16
S5_NBLK = S5_CHUNK // S5_TB
S5_CPLX = 2 * SSM_GROUPS * SSM_STATE
S5_HALF = SSM_GROUPS * SSM_STATE

TM = 512
VMEM_LIMIT = 56 * 1024 * 1024


def _cparams(*sem):
    return pltpu.CompilerParams(dimension_semantics=sem, vmem_limit_bytes=VMEM_LIMIT)


def _modset(i):
    return jnp.where(i < T_CTX // TM, 0, 1 + (i - T_CTX // TM) // (DEC_SEQ // TM))


def _bdot(a, b):
    return jnp.dot(a, b, preferred_element_type=jnp.float32)


def _bdot_t(a, b):
    return lax.dot_general(a, b, (((1,), (1,)), ((), ())), preferred_element_type=jnp.float32)


def _ada_kernel(c_ref, w_ref, b_ref, o_ref):
    c = c_ref[...]
    s = c / (1.0 + jnp.exp(-c))
    o_ref[0] = _bdot(s.astype(jnp.bfloat16), w_ref[0].astype(jnp.bfloat16)) + b_ref[0]


def _adaln_all(cvec8, w_ada, b_ada):
    tn = 1024
    return pl.pallas_call(
        _ada_kernel,
        grid=(DEPTH, 6 * D_MODEL // tn),
        in_specs=[pl.BlockSpec((SUBLANES, D_MODEL), lambda l, j: (0, 0)),
                  pl.BlockSpec((1, D_MODEL, tn), lambda l, j: (l, 0, j)),
                  pl.BlockSpec((1, 1, tn), lambda l, j: (l, 0, j))],
        out_specs=pl.BlockSpec((1, SUBLANES, tn), lambda l, j: (l, 0, j)),
        out_shape=jax.ShapeDtypeStruct((DEPTH, SUBLANES, 6 * D_MODEL), jnp.float32),
        compiler_params=_cparams("arbitrary", "arbitrary"),
        name="adaln",
    )(cvec8, w_ada, b_ada.reshape(DEPTH, 1, 6 * D_MODEL))


def _s5prep_kernel(ar_ref, ai_ref, ldt_ref, br_ref, bi_ref, abar_re, abar_im, bbar_re, bbar_im):
    ar, ai = ar_ref[0], ai_ref[0]
    dt = jnp.exp(ldt_ref[0])
    mag = jnp.exp(ar * dt)
    are = mag * jnp.cos(ai * dt)
    aim = mag * jnp.sin(ai * dt)
    zr, zi = are - 1.0, aim
    den = ar * ar + ai * ai
    cre = (zr * ar + zi * ai) / den
    cim = (zi * ar - zr * ai) / den
    abar_re[0] = are
    abar_im[0] = aim
    br, bi = br_ref[0], bi_ref[0]
    bbar_re[0] = cre * br - cim * bi
    bbar_im[0] = cre * bi + cim * br


def _s5_prep(a_re, a_im, log_dt, b_re, b_im):
    n = DEPTH * 2
    ar = a_re.reshape(n, 1, S5_HALF)
    ai = a_im.reshape(n, 1, S5_HALF)
    ldt = jnp.broadcast_to(log_dt[..., None], (DEPTH, 2, SSM_GROUPS, SSM_STATE)).reshape(n, 1, S5_HALF)
    br = b_re.transpose(0, 1, 4, 2, 3).reshape(n, SSM_GROUP_CH, S5_HALF)
    bi = b_im.transpose(0, 1, 4, 2, 3).reshape(n, SSM_GROUP_CH, S5_HALF)
    vec = jax.ShapeDtypeStruct((n, 1, S5_HALF), jnp.float32)
    mat = jax.ShapeDtypeStruct((n, SSM_GROUP_CH, S5_HALF), jnp.float32)
    vspec = pl.BlockSpec((1, 1, S5_HALF), lambda j: (j, 0, 0))
    mspec = pl.BlockSpec((1, SSM_GROUP_CH, S5_HALF), lambda j: (j, 0, 0))
    return pl.pallas_call(
        _s5prep_kernel,
        grid=(n,),
        in_specs=[vspec, vspec, vspec, mspec, mspec],
        out_specs=[vspec, vspec, mspec, mspec],
        out_shape=(vec, vec, mat, mat),
        compiler_params=_cparams("arbitrary"),
        name="s5prep",
    )(ar, ai, ldt, br, bi)


def _head_rms(blk, gsum, gain):
    sq = blk * blk
    hi = sq.astype(jnp.bfloat16)
    lo = (sq - hi.astype(jnp.float32)).astype(jnp.bfloat16)
    ms = (_bdot(hi, gsum) + _bdot(lo, gsum)) * (1.0 / HEAD_DIM)
    return blk * lax.rsqrt(ms + EPS) * gain


def _rope(blk, cos, sin_signed, first_half):
    partner = jnp.where(first_half, pltpu.roll(blk, LANES - ROPE_F, 1), pltpu.roll(blk, ROPE_F, 1))
    return blk * cos + partner * sin_signed


def _inproj_kernel(has_moe_residual, *refs):
    if has_moe_residual:
        (x_ref, y_ref, pmod_ref, mod_ref, g_ref, w_ref, qg_ref, kg_ref, gsum_ref, cos_ref, sin_ref,
         u_ref, qkv_ref, kv32_ref, xo_ref) = refs
        x = x_ref[...] + pmod_ref[0, :, 5 * D_MODEL:6 * D_MODEL] * y_ref[...]
        xo_ref[...] = x
    else:
        (x_ref, mod_ref, g_ref, w_ref, qg_ref, kg_ref, gsum_ref, cos_ref, sin_ref,
         u_ref, qkv_ref, kv32_ref) = refs
        x = x_ref[...]
    i = pl.program_id(0)
    shift = mod_ref[0, :, 0:D_MODEL]
    scale = mod_ref[0, :, D_MODEL:2 * D_MODEL]
    ms = jnp.mean(x * x, axis=-1, keepdims=True)
    h = (x * lax.rsqrt(ms + EPS) * g_ref[...]) * (1.0 + scale) + shift
    proj = _bdot(h.astype(jnp.bfloat16), w_ref[...])
    u_ref[...] = proj[:, 0:SSM_WIDTH]

    c0 = SSM_WIDTH
    qn = proj[:, c0:c0 + NA_WIDTH] * ATTN_SCALE
    kn = proj[:, c0 + NA_WIDTH:c0 + 2 * NA_WIDTH]
    vn = proj[:, c0 + 2 * NA_WIDTH:c0 + 3 * NA_WIDTH]
    c1 = c0 + 3 * NA_WIDTH
    vg = proj[:, c1 + GQA_WIDTH + KV_WIDTH:c1 + GQA_WIDTH + 2 * KV_WIDTH]

    gsum = gsum_ref[...]
    cos, sin = cos_ref[...], sin_ref[...]
    lane = lax.broadcasted_iota(jnp.int32, (1, LANES), 1)
    first_half = (lane & (2 * ROPE_F - 1)) < ROPE_F
    qg_blocks = []
    for p in range(GQA_WIDTH // PAIR):
        blk = proj[:, c1 + p * PAIR:c1 + (p + 1) * PAIR]
        blk = _rope(_head_rms(blk, gsum, qg_ref[...]), cos, sin, first_half)
        qg_blocks.append(blk * ATTN_SCALE)
    kg = proj[:, c1 + GQA_WIDTH:c1 + GQA_WIDTH + KV_WIDTH]
    kg = _rope(_head_rms(kg, gsum, kg_ref[...]), cos, sin, first_half)

    o = 0
    for piece in [qn, kn, vn] + qg_blocks + [kg, vg]:
        w = piece.shape[1]
        qkv_ref[:, o:o + w] = piece.astype(jnp.bfloat16)
        o += w

    @pl.when(i < T_CTX // TM)
    def _():
        kv32_ref[:, 0:NA_WIDTH] = kn
        kv32_ref[:, NA_WIDTH:2 * NA_WIDTH] = vn
        kv32_ref[:, 2 * NA_WIDTH:2 * NA_WIDTH + KV_WIDTH] = kg
        kv32_ref[:, 2 * NA_WIDTH + KV_WIDTH:2 * NA_WIDTH + 2 * KV_WIDTH] = vg


QKV_WIDTH = 3 * NA_WIDTH + GQA_WIDTH + 2 * KV_WIDTH
KV32_WIDTH = 2 * NA_WIDTH + 2 * KV_WIDTH
QN_BLK, KN_BLK, VN_BLK, QG_BLK, KG_BLK, VG_BLK = 0, 3, 6, 9, 12, 13


def _rope_block(i):
    n_ctx = T_CTX // TM
    return jnp.where(i < n_ctx, 0, 1 + (i - n_ctx) % (DEC_SEQ // TM))


def _in_proj(x, moe_residual, mods3, norm_g, w_in_bf, q_gain, k_gain, gsum, rope_cos, rope_sin):
    n_ctx = T_CTX // TM
    row_spec = pl.BlockSpec((TM, D_MODEL), lambda i: (i, 0))
    mod_spec = pl.BlockSpec((1, 1, 6 * D_MODEL), lambda i: (_modset(i), 0, 0))
    args, in_specs = [x], [row_spec]
    out_specs = [pl.BlockSpec((TM, SSM_WIDTH), lambda i: (i, 0)),
                 pl.BlockSpec((TM, QKV_WIDTH), lambda i: (i, 0)),
                 pl.BlockSpec((TM, KV32_WIDTH), lambda i: (jnp.minimum(i, n_ctx - 1), 0))]
    out_shape = [jax.ShapeDtypeStruct((T_ALL, SSM_WIDTH), jnp.float32),
                 jax.ShapeDtypeStruct((T_ALL, QKV_WIDTH), jnp.bfloat16),
                 jax.ShapeDtypeStruct((T_CTX, KV32_WIDTH), jnp.float32)]
    if moe_residual is not None:
        args += list(moe_residual)
        in_specs += [row_spec, mod_spec]
        out_specs.append(row_spec)
        out_shape.append(jax.ShapeDtypeStruct((T_ALL, D_MODEL), jnp.float32))
    args += [mods3, norm_g, w_in_bf, q_gain, k_gain, gsum, rope_cos, rope_sin]
    in_specs += [mod_spec,
                 pl.BlockSpec((1, D_MODEL), lambda i: (0, 0)),
                 pl.BlockSpec((D_MODEL, IN_WIDTH), lambda i: (0, 0)),
                 pl.BlockSpec((1, LANES), lambda i: (0, 0)),
                 pl.BlockSpec((1, LANES), lambda i: (0, 0)),
                 pl.BlockSpec((LANES, LANES), lambda i: (0, 0)),
                 pl.BlockSpec((TM, LANES), lambda i: (_rope_block(i), 0)),
                 pl.BlockSpec((TM, LANES), lambda i: (_rope_block(i), 0))]
    outs = pl.pallas_call(
        functools.partial(_inproj_kernel, moe_residual is not None),
        grid=(T_ALL // TM,),
        in_specs=in_specs,
        out_specs=out_specs,
        out_shape=out_shape,
        compiler_params=_cparams("arbitrary"),
        name="in_proj",
    )(*args)
    return outs if moe_residual is not None else (*outs, x)


def _s5scan_kernel(u_ref, b_ref, a_ref, c_ref, y_ref, p_ref, fin_ref, hbuf):
    d = pl.program_id(0)
    k = pl.program_id(1)
    rows = S5_TB * S5_ROWS
    a_re = jnp.broadcast_to(a_ref[0, :, 0:S5_HALF], (SUBLANES, S5_HALF))
    a_im = jnp.broadcast_to(a_ref[0, :, S5_HALF:S5_CPLX], (SUBLANES, S5_HALF))
    ub = u_ref[...].astype(jnp.bfloat16)

    def run(forward):
        data0 = S5_ROWS if forward else 0
        prev0 = 0 if forward else rows
        last0 = rows if forward else 0

        @pl.when(k == 0)
        def _():
            r = lax.broadcasted_iota(jnp.int32, (S5_ROWS, S5_CPLX), 0)
            l = lax.broadcasted_iota(jnp.int32, (S5_ROWS, S5_CPLX), 1)
            hbuf[prev0:prev0 + S5_ROWS, :] = jnp.where((r == S5_POW_ROW) & (l < S5_HALF), 1.0, 0.0)

        @pl.when(k > 0)
        def _():
            hbuf[prev0:prev0 + S5_ROWS, :] = hbuf[last0:last0 + S5_ROWS, :]

        hbuf[data0:data0 + rows, :] = _bdot(ub, b_ref[0])

        def step(i, carry):
            t = i if forward else S5_TB - 1 - i
            cur = pl.multiple_of(data0 + t * S5_ROWS, SUBLANES)
            prv = pl.multiple_of(cur - S5_ROWS if forward else cur + S5_ROWS, SUBLANES)
            for st in range(S5_ROWS // SUBLANES):
                rc = pl.ds(cur + st * SUBLANES, SUBLANES)
                rp = pl.ds(prv + st * SUBLANES, SUBLANES)
                hr = hbuf[rp, 0:S5_HALF]
                hi = hbuf[rp, S5_HALF:S5_CPLX]
                nr = a_re * hr - a_im * hi + hbuf[rc, 0:S5_HALF]
                ni = a_re * hi + a_im * hr + hbuf[rc, S5_HALF:S5_CPLX]
                hbuf[rc, 0:S5_HALF] = nr
                hbuf[rc, S5_HALF:S5_CPLX] = ni
            return carry

        lax.fori_loop(0, S5_TB, step, 0)
        y_ref[0] = _bdot(hbuf[data0:data0 + rows, :].astype(jnp.bfloat16), c_ref[0])
        for t in range(S5_TB):
            r0 = data0 + t * S5_ROWS + S5_POW_ROW
            p_ref[0, t:t + 1, :] = hbuf[r0:r0 + 1, :]

        @pl.when(k == S5_NBLK - 1)
        def _():
            fin_ref[0] = hbuf[last0:last0 + S5_ROWS, :]

    @pl.when(d == 0)
    def _():
        run(True)

    @pl.when(d == 1)
    def _():
        run(False)


def _s5_scan(u_tm, bblk, abar, cblk):
    rows = S5_TB * S5_ROWS

    def tblk(d, k):
        return jnp.where(d == 0, k, S5_NBLK - 1 - k)

    return pl.pallas_call(
        _s5scan_kernel,
        grid=(2, S5_NBLK),
        in_specs=[pl.BlockSpec((rows, SSM_WIDTH), lambda d, k: (tblk(d, k), 0)),
                  pl.BlockSpec((1, SSM_WIDTH, S5_CPLX), lambda d, k: (d, 0, 0)),
                  pl.BlockSpec((1, 1, S5_CPLX), lambda d, k: (d, 0, 0)),
                  pl.BlockSpec((1, S5_CPLX, SSM_WIDTH), lambda d, k: (d, 0, 0))],
        out_specs=[pl.BlockSpec((1, rows, SSM_WIDTH), lambda d, k: (d, tblk(d, k), 0)),
                   pl.BlockSpec((1, S5_TB, S5_CPLX), lambda d, k: (d, tblk(d, k), 0)),
                   pl.BlockSpec((1, S5_ROWS, S5_CPLX), lambda d, k: (d, 0, 0))],
        out_shape=(jax.ShapeDtypeStruct((2, S5_CHUNK * S5_ROWS, SSM_WIDTH), jnp.float32),
                   jax.ShapeDtypeStruct((2, S5_CHUNK, S5_CPLX), jnp.float32),
                   jax.ShapeDtypeStruct((2, S5_ROWS, S5_CPLX), jnp.float32)),
        scratch_shapes=[pltpu.VMEM((rows + S5_ROWS, S5_CPLX), jnp.float32)],
        compiler_params=_cparams("arbitrary", "arbitrary"),
        name="s5_scan",
    )(u_tm, bblk, abar, cblk)


def _cmul(pr, pi, cr, ci):
    return pr * cr - pi * ci, pr * ci + pi * cr


def _s5fin_kernel(u_ref, yf_ref, yb_ref, p_ref, fin_ref, h0_ref, c_ref, d_ref, wg_ref, bg_ref,
                  o_ref, carry, hc, ybuf):
    k = pl.program_id(0)
    nchunk = DEC_SEQ // S5_CHUNK

    @pl.when(k == 0)
    def _():
        for d in range(2):
            t_full = S5_CHUNK - 1 if d == 0 else 0
            pr = p_ref[d, t_full:t_full + 1, 0:S5_HALF]
            pi = p_ref[d, t_full:t_full + 1, S5_HALF:S5_CPLX]
            for b in range(DEC_BATCH):
                cr = h0_ref[d, b:b + 1, 0:S5_HALF]
                ci = h0_ref[d, b:b + 1, S5_HALF:S5_CPLX]
                order = range(nchunk) if d == 0 else range(nchunk - 1, -1, -1)
                for c in order:
                    row = b * nchunk + c
                    carry[d, row:row + 1, 0:S5_HALF] = cr
                    carry[d, row:row + 1, S5_HALF:S5_CPLX] = ci
                    fr = fin_ref[d, S5_LAT0 + row:S5_LAT0 + row + 1, 0:S5_HALF]
                    fi = fin_ref[d, S5_LAT0 + row:S5_LAT0 + row + 1, S5_HALF:S5_CPLX]
                    mr, mi = _cmul(pr, pi, cr, ci)
                    cr, ci = fr + mr, fi + mi

    ybuf[...] = yf_ref[0] + yb_ref[0] + u_ref[...] * d_ref[...]
    for d in range(2):
        cr = carry[d, :, 0:S5_HALF]
        ci = carry[d, :, S5_HALF:S5_CPLX]
        for t in range(S5_TB):
            pr = p_ref[d, pl.ds(k * S5_TB + t, 1), 0:S5_HALF]
            pi = p_ref[d, pl.ds(k * S5_TB + t, 1), S5_HALF:S5_CPLX]
            mr, mi = _cmul(pr, pi, cr, ci)
            hc[t * S5_NLAT:(t + 1) * S5_NLAT, 0:S5_HALF] = mr.astype(jnp.bfloat16)
            hc[t * S5_NLAT:(t + 1) * S5_NLAT, S5_HALF:S5_CPLX] = mi.astype(jnp.bfloat16)
        corr = _bdot(hc[...], c_ref[d])
        ybuf[:, S5_LAT0:S5_LAT0 + S5_NLAT, :] += corr.reshape(S5_TB, S5_NLAT, SSM_WIDTH)

    y = ybuf[...].reshape(S5_TB * S5_ROWS, SSM_WIDTH)
    yg = 0.5 * y * (1.0 + jnp.tanh(math.sqrt(2.0 / math.pi) * (y + 0.044715 * (y * y * y))))
    z = _bdot(yg.astype(jnp.bfloat16), wg_ref[...]) + bg_ref[...]
    out = yg / (1.0 + jnp.exp(-z))
    o_ref[...] = out.reshape(S5_TB, S5_ROWS, SSM_WIDTH).astype(jnp.bfloat16)


def _s5_finish(u_tm3, y_dir4, powers, finals, h0, cblk, ssm_d, w_glu_bf, b_glu):
    blk3 = (S5_TB, S5_ROWS, SSM_WIDTH)
    return pl.pallas_call(
        _s5fin_kernel,
        grid=(S5_NBLK,),
        in_specs=[pl.BlockSpec(blk3, lambda k: (k, 0, 0)),
                  pl.BlockSpec((1,) + blk3, lambda k: (0, k, 0, 0)),
                  pl.BlockSpec((1,) + blk3, lambda k: (1, k, 0, 0)),
                  pl.BlockSpec((2, S5_CHUNK, S5_CPLX), lambda k: (0, 0, 0)),
                  pl.BlockSpec((2, S5_ROWS, S5_CPLX), lambda k: (0, 0, 0)),
                  pl.BlockSpec((2, DEC_BATCH, S5_CPLX), lambda k: (0, 0, 0)),
                  pl.BlockSpec((2, S5_CPLX, SSM_WIDTH), lambda k: (0, 0, 0)),
                  pl.BlockSpec((1, SSM_WIDTH), lambda k: (0, 0)),
                  pl.BlockSpec((SSM_WIDTH, SSM_WIDTH), lambda k: (0, 0)),
                  pl.BlockSpec((1, SSM_WIDTH), lambda k: (0, 0))],
        out_specs=pl.BlockSpec(blk3, lambda k: (k, 0, 0)),
        out_shape=jax.ShapeDtypeStruct((S5_CHUNK, S5_ROWS, SSM_WIDTH), jnp.bfloat16),
        scratch_shapes=[pltpu.VMEM((2, S5_NLAT, S5_CPLX), jnp.float32),
                        pltpu.VMEM((S5_TB * S5_NLAT, S5_CPLX), jnp.bfloat16),
                        pltpu.VMEM(blk3, jnp.float32)],
        compiler_params=_cparams("arbitrary"),
        name="s5_finish",
    )(u_tm3, y_dir4, y_dir4, powers, finals, h0, cblk, ssm_d, w_glu_bf, b_glu)


def _pair_attention(q, kvs, biases):
    lane = lax.broadcasted_iota(jnp.int32, (1, PAIR), 1)
    outs = []
    for a in range(2):
        sel = (lane < HEAD_DIM) if a == 0 else (lane >= HEAD_DIM)
        qa = jnp.where(sel, q, jnp.zeros_like(q))
        ss = []
        for (kk, _), bias in zip(kvs, biases):
            s = _bdot_t(qa, kk)
            if bias is not None:
                s = s + bias[a]
            ss.append(s)
        m = ss[0].max(axis=-1, keepdims=True)
        for s in ss[1:]:
            m = jnp.maximum(m, s.max(axis=-1, keepdims=True))
        den = None
        acc = None
        for s, (_, vv) in zip(ss, kvs):
            p = jnp.exp(s - m)
            l = p.sum(axis=-1, keepdims=True)
            o = _bdot(p.astype(jnp.bfloat16), vv)
            den = l if den is None else den + l
            acc = o if acc is None else acc + o
        outs.append(acc / den)
    return jnp.where(lane < HEAD_DIM, outs[0], outs[1])


ATTN_WIDTH = NA_WIDTH + GQA_WIDTH
NPAIR = NA_WIDTH // PAIR


def _attn_ctx_kernel(qkv_ref, o_ref):
    def blk(j):
        return qkv_ref[:, j * PAIR:(j + 1) * PAIR]

    for p in range(NPAIR):
        o = _pair_attention(blk(QN_BLK + p), [(blk(KN_BLK + p), blk(VN_BLK + p))], [None])
        o_ref[:, p * PAIR:(p + 1) * PAIR] = o.astype(jnp.bfloat16)
    for p in range(NPAIR):
        o = _pair_attention(blk(QG_BLK + p), [(blk(KG_BLK), blk(VG_BLK))], [None])
        o_ref[:, (NPAIR + p) * PAIR:(NPAIR + p + 1) * PAIR] = o.astype(jnp.bfloat16)


def _attn_ctx(qkv):
    return pl.pallas_call(
        _attn_ctx_kernel,
        grid=(BATCH,),
        in_specs=[pl.BlockSpec((SEQ, QKV_WIDTH), lambda b: (b, 0))],
        out_specs=pl.BlockSpec((SEQ, ATTN_WIDTH), lambda b: (b, 0)),
        out_shape=jax.ShapeDtypeStruct((T_CTX, ATTN_WIDTH), jnp.bfloat16),
        compiler_params=_cparams("arbitrary"),
        name="attn_ctx",
    )(qkv)


TQ = 256


def _gqa_lat_kernel(q_ref, k_ref, v_ref, ck_ref, cv_ref, o_ref):
    kvs = [(ck_ref[0], cv_ref[0]), (k_ref[...], v_ref[...])]
    o_ref[...] = _pair_attention(q_ref[...], kvs, [None, None]).astype(jnp.bfloat16)


def _gqa_lat(qkv, ck, cv):
    nq = DEC_SEQ // TQ
    ctx_blocks = T_CTX // DEC_SEQ
    return pl.pallas_call(
        _gqa_lat_kernel,
        grid=(DEC_BATCH, NPAIR, nq),
        in_specs=[pl.BlockSpec((TQ, PAIR), lambda b, p, i: (T_CTX // TQ + b * nq + i, QG_BLK + p)),
                  pl.BlockSpec((DEC_SEQ, PAIR), lambda b, p, i: (ctx_blocks + b, KG_BLK)),
                  pl.BlockSpec((DEC_SEQ, PAIR), lambda b, p, i: (ctx_blocks + b, VG_BLK)),
                  pl.BlockSpec((1, PAST_LEN, PAIR), lambda b, p, i: (b, 0, 0)),
                  pl.BlockSpec((1, PAST_LEN, PAIR), lambda b, p, i: (b, 0, 0))],
        out_specs=pl.BlockSpec((TQ, PAIR), lambda b, p, i: (b * nq + i, p)),
        out_shape=jax.ShapeDtypeStruct((T_LAT, GQA_WIDTH), jnp.bfloat16),
        compiler_params=_cparams("arbitrary", "arbitrary", "arbitrary"),
        name="gqa_lat",
    )(qkv, qkv, qkv, ck, cv)


NA_WIN = NA_KH * GRID_W


NA_ROWS_PER_ITER = 4


def _na_lat_kernel(q_ref, k_ref, v_ref, ck_ref, cv_ref, bias_ref, o_ref):
    ck, cv = ck_ref[0, 0], cv_ref[0, 0]

    def rows(j, carry):
        for rr in range(NA_ROWS_PER_ITER):
            r = j * NA_ROWS_PER_ITER + rr
            rs = jnp.clip(r - NA_KH // 2, 0, GRID_H - NA_KH)
            off = r - rs
            q0 = pl.multiple_of(r * GRID_W, GRID_W)
            k0 = pl.multiple_of(rs * GRID_W, GRID_W)
            q = q_ref[pl.ds(q0, GRID_W), :]
            ks = k_ref[pl.ds(k0, NA_WIN), :]
            vs = v_ref[pl.ds(k0, NA_WIN), :]
            bias = [bias_ref[0, 0, off], bias_ref[0, 1, off]]
            o = _pair_attention(q, [(ks, vs), (ck, cv)], [bias, None])
            o_ref[pl.ds(q0, GRID_W), :] = o.astype(jnp.bfloat16)
        return carry

    lax.fori_loop(0, GRID_H // NA_ROWS_PER_ITER, rows, 0)


def _na_lat(qkv, ck, cv, bias):
    ctx_blocks = T_CTX // DEC_SEQ
    return pl.pallas_call(
        _na_lat_kernel,
        grid=(DEC_BATCH, NPAIR),
        in_specs=[pl.BlockSpec((DEC_SEQ, PAIR), lambda b, p: (ctx_blocks + b, QN_BLK + p)),
                  pl.BlockSpec((DEC_SEQ, PAIR), lambda b, p: (ctx_blocks + b, KN_BLK + p)),
                  pl.BlockSpec((DEC_SEQ, PAIR), lambda b, p: (ctx_blocks + b, VN_BLK + p)),
                  pl.BlockSpec((1, 1, PAST_LEN, PAIR), lambda b, p: (b, p, 0, 0)),
                  pl.BlockSpec((1, 1, PAST_LEN, PAIR), lambda b, p: (b, p, 0, 0)),
                  pl.BlockSpec((1, 2, NA_KH, GRID_W, NA_WIN), lambda b, p: (p, 0, 0, 0, 0))],
        out_specs=pl.BlockSpec((DEC_SEQ, PAIR), lambda b, p: (b, p)),
        out_shape=jax.ShapeDtypeStruct((T_LAT, NA_WIDTH), jnp.bfloat16),
        compiler_params=_cparams("arbitrary", "arbitrary"),
        name="na_lat",
    )(qkv, qkv, qkv, ck, cv, bias)


def _na_bias_table(rpb):
    off = jnp.arange(NA_KH)
    i = jnp.arange(NA_KH)
    dr = i[None, :] - off[:, None] + (NA_KH - 1)
    cq = jnp.arange(GRID_W)
    ck = jnp.arange(GRID_W)
    cs = jnp.clip(cq - NA_KW // 2, 0, GRID_W - NA_KW)
    col_in = (ck[None, :] >= cs[:, None]) & (ck[None, :] < cs[:, None] + NA_KW)
    dc = jnp.clip(ck[None, :] - cq[:, None] + NA_KW - 1, 0, 2 * NA_KW - 2)
    sel_r = jax.nn.one_hot(dr, 2 * NA_KH - 1, dtype=jnp.float32)
    sel_c = jax.nn.one_hot(dc, 2 * NA_KW - 1, dtype=jnp.float32)
    b = jnp.einsum("hrc,oir,qkc->hoiqk", rpb.astype(jnp.float32), sel_r, sel_c,
                   precision=lax.Precision.HIGHEST)
    b = jnp.where(col_in[None, None, None], b, NEG_INF)
    b = b.transpose(0, 1, 3, 2, 4).reshape(NA_HEADS, NA_KH, GRID_W, NA_WIN)
    return b.reshape(NA_HEADS // 2, 2, NA_KH, GRID_W, NA_WIN)


GATE_LANE0 = N_GROUPS
GROUP_LANE = 0
_EPG_SHIFT = EXPERTS_PER_GROUP.bit_length() - 1
assert 1 << _EPG_SHIFT == EXPERTS_PER_GROUP


def _route(logits):
    lane_i = lax.broadcasted_iota(jnp.int32, logits.shape, 1)
    lane = lane_i.astype(jnp.float32)
    big = float(LANES)
    is_g = lane_i < N_GROUPS
    lg = jnp.where(is_g, logits, NEG_INF)
    mg = lg.max(axis=-1, keepdims=True)
    pg_top = 1.0 / jnp.where(is_g, jnp.exp(lg - mg), 0.0).sum(axis=-1, keepdims=True)
    g_idx = jnp.where(is_g & (lg == mg), lane, big).min(axis=-1, keepdims=True)
    in_e = (lane_i >= GATE_LANE0) & (lane_i < GATE_LANE0 + N_EXPERTS)
    lane_group = jnp.where(in_e, ((lane_i - GATE_LANE0) >> _EPG_SHIFT).astype(jnp.float32), -1.0)
    sel = lane_group == g_idx
    le = jnp.where(sel, logits, NEG_INF)
    me = le.max(axis=-1, keepdims=True)
    ex = jnp.where(sel, jnp.exp(le - me), 0.0)
    pe = ex / ex.sum(axis=-1, keepdims=True)
    p1 = pe.max(axis=-1, keepdims=True)
    i1 = jnp.where(sel & (pe == p1), lane, big).min(axis=-1, keepdims=True)
    rest = sel & (lane != i1)
    pr = jnp.where(rest, pe, -1.0)
    p2 = pr.max(axis=-1, keepdims=True)
    i2 = jnp.where(rest & (pr == p2), lane, big).min(axis=-1, keepdims=True)
    tot = p1 + p2
    within = jnp.where(lane == i1, p1 / tot, 0.0) + jnp.where(lane == i2, p2 / tot, 0.0)
    return pg_top * within + jnp.where(lane_i == GROUP_LANE, g_idx, 0.0)


HG_WIDTH = D_MODEL + LANES


def _outproj_kernel(x_ref, ys_ref, yc_ref, yn_ref, yg_ref, mod_ref, g_ref, w_ref, wr_hi_ref, wr_lo_ref, br_ref,
                    x1_ref, hg_ref):
    is_ctx = pl.program_id(0) < T_CTX // TM
    y_na = jnp.where(is_ctx, yc_ref[:, 0:NA_WIDTH], yn_ref[...])
    y_g = jnp.where(is_ctx, yc_ref[:, NA_WIDTH:ATTN_WIDTH], yg_ref[...])
    mix = (_bdot(ys_ref[...], w_ref[0:SSM_WIDTH, :])
           + _bdot(y_na, w_ref[SSM_WIDTH:SSM_WIDTH + NA_WIDTH, :])
           + _bdot(y_g, w_ref[SSM_WIDTH + NA_WIDTH:, :]))
    gate1 = mod_ref[0, :, 2 * D_MODEL:3 * D_MODEL]
    shift2 = mod_ref[0, :, 3 * D_MODEL:4 * D_MODEL]
    scale2 = mod_ref[0, :, 4 * D_MODEL:5 * D_MODEL]
    x1 = x_ref[...] + gate1 * mix
    x1_ref[...] = x1
    ms = jnp.mean(x1 * x1, axis=-1, keepdims=True)
    h2 = (x1 * lax.rsqrt(ms + EPS) * g_ref[...]) * (1.0 + scale2) + shift2
    hg_ref[:, 0:D_MODEL] = h2
    h_hi = h2.astype(jnp.bfloat16)
    h_lo = (h2 - h_hi.astype(jnp.float32)).astype(jnp.bfloat16)
    logits = (_bdot(h_hi, wr_hi_ref[...]) + _bdot(h_lo, wr_hi_ref[...]) + _bdot(h_hi, wr_lo_ref[...])
              + br_ref[...])
    hg_ref[:, D_MODEL:HG_WIDTH] = _route(logits)


def _out_proj(x, y_ssm, y_ctx, y_na_lat, y_g_lat, mods3, norm_g, w_out_bf, wr_hi, wr_lo, b_router):
    n_ctx = T_CTX // TM
    return pl.pallas_call(
        _outproj_kernel,
        grid=(T_ALL // TM,),
        in_specs=[pl.BlockSpec((TM, D_MODEL), lambda i: (i, 0)),
                  pl.BlockSpec((TM, SSM_WIDTH), lambda i: (i, 0)),
                  pl.BlockSpec((TM, ATTN_WIDTH), lambda i: (jnp.minimum(i, n_ctx - 1), 0)),
                  pl.BlockSpec((TM, NA_WIDTH), lambda i: (jnp.maximum(i - n_ctx, 0), 0)),
                  pl.BlockSpec((TM, GQA_WIDTH), lambda i: (jnp.maximum(i - n_ctx, 0), 0)),
                  pl.BlockSpec((1, 1, 6 * D_MODEL), lambda i: (_modset(i), 0, 0)),
                  pl.BlockSpec((1, D_MODEL), lambda i: (0, 0)),
                  pl.BlockSpec((D_MODEL, D_MODEL), lambda i: (0, 0)),
                  pl.BlockSpec((D_MODEL, LANES), lambda i: (0, 0)),
                  pl.BlockSpec((D_MODEL, LANES), lambda i: (0, 0)),
                  pl.BlockSpec((1, LANES), lambda i: (0, 0))],
        out_specs=[pl.BlockSpec((TM, D_MODEL), lambda i: (i, 0)),
                   pl.BlockSpec((TM, HG_WIDTH), lambda i: (i, 0))],
        out_shape=(jax.ShapeDtypeStruct((T_ALL, D_MODEL), jnp.float32),
                   jax.ShapeDtypeStruct((T_ALL, HG_WIDTH), jnp.float32)),
        compiler_params=_cparams("arbitrary"),
        name="out_proj",
    )(x, y_ssm, y_ctx, y_na_lat, y_g_lat, mods3, norm_g, w_out_bf, wr_hi, wr_lo, b_router)


TMOE = 256
NT_MOE = T_ALL // TMOE + N_GROUPS
MOE_SLOTS = NT_MOE * TMOE
GROUP_HIDDEN = EXPERTS_PER_GROUP * EXPERT_HIDDEN
DMA_UNROLL = 8


def _moe_dispatch(group_f):
    i32 = jnp.int32
    g = group_f.astype(i32)
    onehot = (g[:, None] == jnp.arange(N_GROUPS, dtype=i32)[None, :]).astype(i32)
    csum = jnp.cumsum(onehot, axis=0)
    counts = csum[-1]
    rank = jnp.sum(onehot * csum, axis=1) - 1
    tiles = (counts + TMOE - 1) // TMOE
    tile_end = jnp.cumsum(tiles)
    tile_start = tile_end - tiles
    slot = jnp.sum(onehot * tile_start[None, :], axis=1) * TMOE + rank
    tok = jnp.full((MOE_SLOTS,), -1, i32).at[slot].set(jnp.arange(T_ALL, dtype=i32))
    pad = tok < 0
    pad_rank = jnp.cumsum(pad.astype(i32)) - 1
    row_of_slot = jnp.where(pad, T_ALL + pad_rank, tok)
    tile_group = jnp.minimum(
        jnp.sum(jnp.arange(NT_MOE, dtype=i32)[:, None] >= tile_end[None, :], axis=1), N_GROUPS - 1).astype(i32)
    return tile_group, row_of_slot


def _moe_kernel(tg_ref, row_ref, hg_hbm, wgu_ref, wd_ref, y_hbm, gbuf, ybuf, gsem, ssem):
    i = pl.program_id(0)
    slot = lax.rem(i, 2)
    other = 1 - slot

    def gather_copy(row, buf_slot, r):
        return pltpu.make_async_copy(hg_hbm.at[pl.ds(row, 1)], gbuf.at[buf_slot, pl.ds(r, 1)], gsem.at[buf_slot])

    def scatter_copy(row, buf_slot, r):
        return pltpu.make_async_copy(ybuf.at[buf_slot, pl.ds(r, 1)], y_hbm.at[pl.ds(row, 1)], ssem.at[buf_slot])

    def start_gather(tile, buf_slot):
        def body(r, c):
            row = row_ref[tile * TMOE + r]
            gather_copy(jnp.where(row < T_ALL, row, 0), buf_slot, r).start()
            return c
        lax.fori_loop(0, TMOE, body, 0, unroll=DMA_UNROLL)

    def wait_rows(copy_fn, buf_slot):
        def body(r, c):
            copy_fn(0, buf_slot, r).wait()
            return c
        lax.fori_loop(0, TMOE, body, 0, unroll=DMA_UNROLL)

    @pl.when(i == 0)
    def _():
        start_gather(0, 0)

    @pl.when(i + 1 < NT_MOE)
    def _():
        start_gather(i + 1, other)

    wait_rows(gather_copy, slot)

    h = gbuf[slot, :, 0:D_MODEL].astype(jnp.bfloat16)
    routing = gbuf[slot, :, D_MODEL:HG_WIDTH]
    a = _bdot(h, wgu_ref[0, :, 0:GROUP_HIDDEN])
    b = _bdot(h, wgu_ref[0, :, GROUP_HIDDEN:2 * GROUP_HIDDEN])
    lane = lax.broadcasted_iota(jnp.int32, (1, LANES), 1)
    lane0 = GATE_LANE0 + tg_ref[i] * EXPERTS_PER_GROUP
    acts = []
    for j in range(EXPERTS_PER_GROUP):
        gate = jnp.where(lane == lane0 + j, routing, 0.0).sum(axis=-1, keepdims=True)
        aj = a[:, j * EXPERT_HIDDEN:(j + 1) * EXPERT_HIDDEN]
        bj = b[:, j * EXPERT_HIDDEN:(j + 1) * EXPERT_HIDDEN]
        acts.append(((aj / (1.0 + jnp.exp(-aj))) * bj * gate).astype(jnp.bfloat16))
    y = _bdot(jnp.concatenate(acts, axis=1), wd_ref[0])

    @pl.when(i >= 2)
    def _():
        wait_rows(scatter_copy, slot)

    ybuf[slot] = y

    def sbody(r, c):
        scatter_copy(row_ref[i * TMOE + r], slot, r).start()
        return c
    lax.fori_loop(0, TMOE, sbody, 0, unroll=DMA_UNROLL)

    @pl.when(i == NT_MOE - 1)
    def _():
        wait_rows(scatter_copy, other)
        wait_rows(scatter_copy, slot)


def _moe(hg, tile_group, row_of_slot, wgu_bf, wd_bf):
    grid_spec = pltpu.PrefetchScalarGridSpec(
        num_scalar_prefetch=2,
        grid=(NT_MOE,),
        in_specs=[pl.BlockSpec(memory_space=pl.ANY),
                  pl.BlockSpec((1, D_MODEL, 2 * GROUP_HIDDEN), lambda i, tg, rows: (tg[i], 0, 0)),
                  pl.BlockSpec((1, GROUP_HIDDEN, D_MODEL), lambda i, tg, rows: (tg[i], 0, 0))],
        out_specs=pl.BlockSpec(memory_space=pl.ANY),
        scratch_shapes=[pltpu.VMEM((2, TMOE, HG_WIDTH), jnp.float32),
                        pltpu.VMEM((2, TMOE, D_MODEL), jnp.float32),
                        pltpu.SemaphoreType.DMA((2,)),
                        pltpu.SemaphoreType.DMA((2,))])
    return pl.pallas_call(
        _moe_kernel,
        grid_spec=grid_spec,
        out_shape=jax.ShapeDtypeStruct((MOE_SLOTS, D_MODEL), jnp.float32),
        compiler_params=_cparams("arbitrary"),
        name="moe",
    )(tile_group, row_of_slot, hg, wgu_bf, wd_bf)


def _final_kernel(x1_ref, y_ref, mod_ref, g_ref, o_ref):
    x = x1_ref[...] + mod_ref[0, :, 5 * D_MODEL:6 * D_MODEL] * y_ref[...]
    ms = jnp.mean(x * x, axis=-1, keepdims=True)
    o_ref[...] = x * lax.rsqrt(ms + EPS) * g_ref[...]


def _final_norm(x1, y, mods3, g):
    return pl.pallas_call(
        _final_kernel,
        grid=(T_ALL // TM,),
        in_specs=[pl.BlockSpec((TM, D_MODEL), lambda i: (i, 0)),
                  pl.BlockSpec((TM, D_MODEL), lambda i: (i, 0)),
                  pl.BlockSpec((1, 1, 6 * D_MODEL), lambda i: (_modset(i), 0, 0)),
                  pl.BlockSpec((1, D_MODEL), lambda i: (0, 0))],
        out_specs=pl.BlockSpec((TM, D_MODEL), lambda i: (i, 0)),
        out_shape=jax.ShapeDtypeStruct((T_ALL, D_MODEL), jnp.float32),
        compiler_params=_cparams("arbitrary"),
        name="final_norm",
    )(x1, y, mods3, g)


_QG_ORDER = np.array([0, 3, 1, 4, 2, 5])


def _permute_qg_columns(w_in):
    c1 = SSM_WIDTH + 3 * NA_WIDTH
    qg = w_in[..., c1:c1 + GQA_WIDTH].reshape(*w_in.shape[:-1], GQA_HEADS, HEAD_DIM)
    qg = qg[..., _QG_ORDER, :].reshape(*w_in.shape[:-1], GQA_WIDTH)
    return jnp.concatenate([w_in[..., :c1], qg, w_in[..., c1 + GQA_WIDTH:]], axis=-1)


def _permute_yg_rows(w_out):
    r0 = SSM_WIDTH + NA_WIDTH
    yg = w_out[:, r0:, :].reshape(DEPTH, GQA_HEADS, HEAD_DIM, D_MODEL)[:, _QG_ORDER]
    return jnp.concatenate([w_out[:, :r0, :], yg.reshape(DEPTH, GQA_WIDTH, D_MODEL)], axis=1)


def _rope_tables():
    t = jnp.arange(DEC_SEQ)
    row = (t // GRID_W).astype(jnp.float32)
    col = (t % GRID_W).astype(jnp.float32)
    freqs = ROPE_THETA ** (-jnp.arange(ROPE_F, dtype=jnp.float32) / ROPE_F)
    ang_r = row[:, None] * freqs
    ang_c = col[:, None] * freqs
    ang = jnp.concatenate([ang_r, ang_r, ang_c, ang_c], axis=1)
    cos = jnp.cos(ang)
    sin = jnp.sin(ang)
    sign = jnp.tile(jnp.concatenate([-jnp.ones(ROPE_F), jnp.ones(ROPE_F)]), 2).astype(jnp.float32)
    sin = sin * sign
    cos = jnp.tile(cos, (1, 2))
    sin = jnp.tile(sin, (1, 2))
    cos = jnp.concatenate([jnp.ones((TM, LANES), jnp.float32), cos], axis=0)
    sin = jnp.concatenate([jnp.zeros((TM, LANES), jnp.float32), sin], axis=0)
    return cos, sin


def _pair_layout(cache):
    b, l, h, n, dh = cache.shape
    c = cache.reshape(b, l, h // 2, 2, n, dh).transpose(1, 0, 2, 4, 3, 5)
    return c.reshape(l, b, h // 2, n, 2 * dh).astype(jnp.bfloat16)


def _block_diag_b(bbar):
    bb = bbar.reshape(DEPTH, 2, SSM_GROUP_CH, SSM_GROUPS, SSM_STATE)
    eye = jnp.eye(SSM_GROUPS, dtype=bbar.dtype)
    out = eye[None, None, :, None, :, None] * bb.transpose(0, 1, 3, 2, 4)[:, :, :, :, None, :]
    return out.reshape(DEPTH, 2, SSM_WIDTH, S5_HALF)


def _block_diag_c(c):
    eye = jnp.eye(SSM_GROUPS, dtype=c.dtype)
    out = eye[None, None, :, None, :, None] * c.transpose(0, 1, 2, 4, 3)[:, :, :, :, None, :]
    return out.reshape(DEPTH, 2, S5_HALF, SSM_WIDTH)


def kernel(x_prompt, x_sample, cache_na_k, cache_na_v, cache_gqa_k, cache_gqa_v, state_ssm, c, c_ctx, norm_mix_g, norm_ffn_g, w_ada, b_ada, w_in, ssm_a_re, ssm_a_im, ssm_log_dt, ssm_b_re, ssm_b_im, ssm_c_re, ssm_c_im, ssm_d, w_glu, b_glu, na_rpb, q_norm_g, k_norm_g, w_out, moe_w_group, moe_b_group, moe_w_expert, moe_b_expert, moe_w_gate, moe_w_up, moe_w_down, final_norm_g):
    f32, bf16 = jnp.float32, jnp.bfloat16
    x = jnp.concatenate([x_prompt.reshape(T_CTX, D_MODEL), x_sample.reshape(T_LAT, D_MODEL)], axis=0)

    cvec8 = jnp.concatenate([c_ctx[None, :], c, jnp.zeros((SUBLANES - N_MODSETS, D_MODEL), f32)], axis=0)
    mods_all = _adaln_all(cvec8, w_ada, b_ada)

    w_in_bf = _permute_qg_columns(w_in).astype(bf16)
    w_out_bf = _permute_yg_rows(w_out).astype(bf16)
    w_glu_bf = w_glu.astype(bf16)
    def group_cols(w):
        w = w.reshape(DEPTH, N_GROUPS, EXPERTS_PER_GROUP, D_MODEL, EXPERT_HIDDEN).transpose(0, 1, 3, 2, 4)
        return w.reshape(DEPTH, N_GROUPS, D_MODEL, GROUP_HIDDEN)

    wgu_bf = jnp.concatenate([group_cols(moe_w_gate).astype(bf16), group_cols(moe_w_up).astype(bf16)], axis=-1)
    wd_bf = moe_w_down.reshape(DEPTH, N_GROUPS, GROUP_HIDDEN, D_MODEL).astype(bf16)
    w_router = jnp.concatenate(
        [moe_w_group, moe_w_expert, jnp.zeros((DEPTH, D_MODEL, LANES - N_GROUPS - N_EXPERTS), f32)], axis=-1)
    wr_hi = w_router.astype(bf16)
    wr_lo = (w_router - wr_hi.astype(f32)).astype(bf16)
    b_router = jnp.concatenate(
        [moe_b_group, moe_b_expert, jnp.zeros((DEPTH, LANES - N_GROUPS - N_EXPERTS), f32)], axis=-1)
    q_gain = jnp.tile(q_norm_g, (1, 2))
    k_gain = jnp.tile(k_norm_g, (1, 2))
    lane = np.arange(LANES)
    gsum = jnp.asarray((lane[:, None] // HEAD_DIM) == (lane[None, :] // HEAD_DIM), bf16)
    rope_cos, rope_sin = _rope_tables()

    abar_re, abar_im, bbar_re, bbar_im = _s5_prep(ssm_a_re, ssm_a_im, ssm_log_dt, ssm_b_re, ssm_b_im)
    abar = jnp.concatenate([abar_re, abar_im], axis=-1).reshape(DEPTH, 2, 1, S5_CPLX)
    bblk = jnp.concatenate([_block_diag_b(bbar_re), _block_diag_b(bbar_im)], axis=-1).astype(bf16)
    cblk = jnp.concatenate([_block_diag_c(ssm_c_re), -_block_diag_c(ssm_c_im)], axis=2).astype(bf16)
    h0 = state_ssm.transpose(1, 2, 0, 5, 3, 4).reshape(DEPTH, 2, DEC_BATCH, S5_CPLX)

    ck_na, cv_na = _pair_layout(cache_na_k), _pair_layout(cache_na_v)
    ck_g, cv_g = _pair_layout(cache_gqa_k)[:, :, 0], _pair_layout(cache_gqa_v)[:, :, 0]
    na_bias = jax.vmap(_na_bias_table)(na_rpb)

    kv32_l, fin_l = [], []
    moe_residual = None
    for l in range(DEPTH):
        mods3 = mods_all[l, :N_MODSETS].reshape(N_MODSETS, 1, 6 * D_MODEL)
        u, qkv, kv32, x = _in_proj(x, moe_residual, mods3, norm_mix_g[l][None], w_in_bf[l], q_gain[l][None],
                                   k_gain[l][None], gsum, rope_cos, rope_sin)
        kv32_l.append(kv32)

        u_tm = jnp.pad(u.reshape(S5_SEQS, S5_CHUNK, SSM_WIDTH).transpose(1, 0, 2),
                       ((0, 0), (0, S5_ROWS - S5_SEQS), (0, 0)))
        y_dir, powers, finals = _s5_scan(u_tm.reshape(S5_CHUNK * S5_ROWS, SSM_WIDTH), bblk[l], abar[l], cblk[l])
        fin_l.append(finals[:, :BATCH])
        y_tm = _s5_finish(u_tm, y_dir.reshape(2, S5_CHUNK, S5_ROWS, SSM_WIDTH), powers, finals, h0[l], cblk[l],
                          ssm_d[l][None], w_glu_bf[l], b_glu[l][None])
        y_ssm = y_tm[:, :S5_SEQS].transpose(1, 0, 2).reshape(T_ALL, SSM_WIDTH)

        y_ctx = _attn_ctx(qkv)
        y_g_lat = _gqa_lat(qkv, ck_g[l], cv_g[l])
        y_na_lat = _na_lat(qkv, ck_na[l], cv_na[l], na_bias[l])

        x1, hg = _out_proj(x, y_ssm, y_ctx, y_na_lat, y_g_lat, mods3, norm_ffn_g[l][None], w_out_bf[l],
                           wr_hi[l], wr_lo[l], b_router[l][None])
        tile_group, row_of_slot = _moe_dispatch(hg[:, D_MODEL + GROUP_LANE])
        y_moe = _moe(hg, tile_group, row_of_slot, wgu_bf[l], wd_bf[l])
        x, moe_residual = x1, (y_moe, mods3)

    y = _final_norm(x, moe_residual[0], moe_residual[1], final_norm_g[None])
    y_prompt = y[:T_CTX].reshape(BATCH, SEQ, D_MODEL)
    y_sample = y[T_CTX:].reshape(DEC_BATCH, DEC_SEQ, D_MODEL)

    kv32 = jnp.stack(kv32_l, axis=0).reshape(DEPTH, BATCH, SEQ, KV32_WIDTH)

    def heads(t, n_heads):
        return t.reshape(DEPTH, BATCH, SEQ, n_heads, HEAD_DIM).transpose(1, 0, 3, 2, 4)

    new_na_k = heads(kv32[..., 0:NA_WIDTH], NA_HEADS)
    new_na_v = heads(kv32[..., NA_WIDTH:2 * NA_WIDTH], NA_HEADS)
    new_gqa_k = heads(kv32[..., 2 * NA_WIDTH:2 * NA_WIDTH + KV_WIDTH], GQA_KV_HEADS)
    new_gqa_v = heads(kv32[..., 2 * NA_WIDTH + KV_WIDTH:], GQA_KV_HEADS)
    fin = jnp.stack(fin_l, axis=0).reshape(DEPTH, 2, BATCH, 2, SSM_GROUPS, SSM_STATE)
    new_state = fin.transpose(2, 0, 1, 4, 5, 3)
    return (y_prompt, y_sample, new_na_k, new_na_v, new_gqa_k, new_gqa_v, new_state)
```

```python
import functools
import math

import jax
import jax.numpy as jnp
import numpy as np
from jax import lax
from jax.experimental import pallas as pl
from jax.experimental.pallas import tpu as pltpu

D_MODEL = 1024
BATCH = 16
SEQ = 256
DEPTH = 4
DEC_BATCH = 2
DEC_SEQ = 4096
PAST_LEN = 512
GRID_W = 64
GRID_H = DEC_SEQ // GRID_W
HEAD_DIM = 64
SSM_WIDTH = 256
SSM_GROUP_CH = 16
SSM_GROUPS = 16
SSM_STATE = 64
NA_HEADS = 6
NA_WIDTH = 384
NA_KH = 8
NA_KW = 16
GQA_HEADS = 6
GQA_KV_HEADS = 2
GQA_REP = 3
GQA_WIDTH = 384
KV_WIDTH = 128
IN_WIDTH = 2048
ROPE_THETA = 10000.0
ROPE_F = 16
N_GROUPS = 4
EXPERTS_PER_GROUP = 4
N_EXPERTS = 16
EXPERT_HIDDEN = 256
EPS = 1e-6
NEG_INF = -1e30

T_CTX = BATCH * SEQ
T_LAT = DEC_BATCH * DEC_SEQ
T_ALL = T_CTX + T_LAT
N_MODSETS = 1 + DEC_BATCH
LANES = 128
SUBLANES = 8
PAIR = 2 * HEAD_DIM
ATTN_SCALE = HEAD_DIM ** -0.5

S5_CHUNK = 256
S5_SEQS = T_ALL // S5_CHUNK
S5_ROWS = 56
S5_POW_ROW = 48
S5_LAT0 = BATCH
S5_NLAT = S5_SEQS - BATCH
S5_TB = 16
S5_NBLK = S5_CHUNK // S5_TB
S5_CPLX = 2 * SSM_GROUPS * SSM_STATE
S5_HALF = SSM_GROUPS * SSM_STATE

TM = 512
VMEM_LIMIT = 56 * 1024 * 1024


def _cparams(*sem):
    return pltpu.CompilerParams(dimension_semantics=sem, vmem_limit_bytes=VMEM_LIMIT)


def _modset(i):
    return jnp.where(i < T_CTX // TM, 0, 1 + (i - T_CTX // TM) // (DEC_SEQ // TM))


def _bdot(a, b):
    return jnp.dot(a, b, preferred_element_type=jnp.float32)


def _bdot_t(a, b):
    return lax.dot_general(a, b, (((1,), (1,)), ((), ())), preferred_element_type=jnp.float32)


def _ada_kernel(c_ref, w_ref, b_ref, o_ref):
    c = c_ref[...]
    s = c / (1.0 + jnp.exp(-c))
    o_ref[0] = _bdot(s.astype(jnp.bfloat16), w_ref[0].astype(jnp.bfloat16)) + b_ref[0]


def _adaln_all(cvec8, w_ada, b_ada):
    tn = 1024
    return pl.pallas_call(
        _ada_kernel,
        grid=(DEPTH, 6 * D_MODEL // tn),
        in_specs=[pl.BlockSpec((SUBLANES, D_MODEL), lambda l, j: (0, 0)),
                  pl.BlockSpec((1, D_MODEL, tn), lambda l, j: (l, 0, j)),
                  pl.BlockSpec((1, 1, tn), lambda l, j: (l, 0, j))],
        out_specs=pl.BlockSpec((1, SUBLANES, tn), lambda l, j: (l, 0, j)),
        out_shape=jax.ShapeDtypeStruct((DEPTH, SUBLANES, 6 * D_MODEL), jnp.float32),
        compiler_params=_cparams("arbitrary", "arbitrary"),
        name="adaln",
    )(cvec8, w_ada, b_ada.reshape(DEPTH, 1, 6 * D_MODEL))


def _s5prep_kernel(ar_ref, ai_ref, ldt_ref, br_ref, bi_ref, abar_re, abar_im, bbar_re, bbar_im):
    ar, ai = ar_ref[0], ai_ref[0]
    dt = jnp.exp(ldt_ref[0])
    mag = jnp.exp(ar * dt)
    are = mag * jnp.cos(ai * dt)
    aim = mag * jnp.sin(ai * dt)
    zr, zi = are - 1.0, aim
    den = ar * ar + ai * ai
    cre = (zr * ar + zi * ai) / den
    cim = (zi * ar - zr * ai) / den
    abar_re[0] = are
    abar_im[0] = aim
    br, bi = br_ref[0], bi_ref[0]
    bbar_re[0] = cre * br - cim * bi
    bbar_im[0] = cre * bi + cim * br


def _s5_prep(a_re, a_im, log_dt, b_re, b_im):
    n = DEPTH * 2
    ar = a_re.reshape(n, 1, S5_HALF)
    ai = a_im.reshape(n, 1, S5_HALF)
    ldt = jnp.broadcast_to(log_dt[..., None], (DEPTH, 2, SSM_GROUPS, SSM_STATE)).reshape(n, 1, S5_HALF)
    br = b_re.transpose(0, 1, 4, 2, 3).reshape(n, SSM_GROUP_CH, S5_HALF)
    bi = b_im.transpose(0, 1, 4, 2, 3).reshape(n, SSM_GROUP_CH, S5_HALF)
    vec = jax.ShapeDtypeStruct((n, 1, S5_HALF), jnp.float32)
    mat = jax.ShapeDtypeStruct((n, SSM_GROUP_CH, S5_HALF), jnp.float32)
    vspec = pl.BlockSpec((1, 1, S5_HALF), lambda j: (j, 0, 0))
    mspec = pl.BlockSpec((1, SSM_GROUP_CH, S5_HALF), lambda j: (j, 0, 0))
    return pl.pallas_call(
        _s5prep_kernel,
        grid=(n,),
        in_specs=[vspec, vspec, vspec, mspec, mspec],
        out_specs=[vspec, vspec, mspec, mspec],
        out_shape=(vec, vec, mat, mat),
        compiler_params=_cparams("arbitrary"),
        name="s5prep",
    )(ar, ai, ldt, br, bi)


def _head_rms(blk, gsum, gain):
    sq = blk * blk
    hi = sq.astype(jnp.bfloat16)
    lo = (sq - hi.astype(jnp.float32)).astype(jnp.bfloat16)
    ms = (_bdot(hi, gsum) + _bdot(lo, gsum)) * (1.0 / HEAD_DIM)
    return blk * lax.rsqrt(ms + EPS) * gain


def _rope(blk, cos, sin_signed, first_half):
    partner = jnp.where(first_half, pltpu.roll(blk, LANES - ROPE_F, 1), pltpu.roll(blk, ROPE_F, 1))
    return blk * cos + partner * sin_signed


def _inproj_kernel(has_moe_residual, *refs):
    if has_moe_residual:
        (x_ref, y_ref, pmod_ref, mod_ref, g_ref, w_ref, qg_ref, kg_ref, gsum_ref, cos_ref, sin_ref,
         u_ref, qkv_ref, kv32_ref, xo_ref) = refs
        x = x_ref[...] + pmod_ref[0, :, 5 * D_MODEL:6 * D_MODEL] * y_ref[...]
        xo_ref[...] = x
    else:
        (x_ref, mod_ref, g_ref, w_ref, qg_ref, kg_ref, gsum_ref, cos_ref, sin_ref,
         u_ref, qkv_ref, kv32_ref) = refs
        x = x_ref[...]
    i = pl.program_id(0)
    shift = mod_ref[0, :, 0:D_MODEL]
    scale = mod_ref[0, :, D_MODEL:2 * D_MODEL]
    ms = jnp.mean(x * x, axis=-1, keepdims=True)
    h = (x * lax.rsqrt(ms + EPS) * g_ref[...]) * (1.0 + scale) + shift
    proj = _bdot(h.astype(jnp.bfloat16), w_ref[...])
    u_ref[...] = proj[:, 0:SSM_WIDTH]

    c0 = SSM_WIDTH
    qn = proj[:, c0:c0 + NA_WIDTH] * ATTN_SCALE
    kn = proj[:, c0 + NA_WIDTH:c0 + 2 * NA_WIDTH]
    vn = proj[:, c0 + 2 * NA_WIDTH:c0 + 3 * NA_WIDTH]
    c1 = c0 + 3 * NA_WIDTH
    vg = proj[:, c1 + GQA_WIDTH + KV_WIDTH:c1 + GQA_WIDTH + 2 * KV_WIDTH]

    gsum = gsum_ref[...]
    cos, sin = cos_ref[...], sin_ref[...]
    lane = lax.broadcasted_iota(jnp.int32, (1, LANES), 1)
    first_half = (lane & (2 * ROPE_F - 1)) < ROPE_F
    qg_blocks = []
    for p in range(GQA_WIDTH // PAIR):
        blk = proj[:, c1 + p * PAIR:c1 + (p + 1) * PAIR]
        blk = _rope(_head_rms(blk, gsum, qg_ref[...]), cos, sin, first_half)
        qg_blocks.append(blk * ATTN_SCALE)
    kg = proj[:, c1 + GQA_WIDTH:c1 + GQA_WIDTH + KV_WIDTH]
    kg = _rope(_head_rms(kg, gsum, kg_ref[...]), cos, sin, first_half)

    o = 0
    for piece in [qn, kn, vn] + qg_blocks + [kg, vg]:
        w = piece.shape[1]
        qkv_ref[:, o:o + w] = piece.astype(jnp.bfloat16)
        o += w

    @pl.when(i < T_CTX // TM)
    def _():
        kv32_ref[:, 0:NA_WIDTH] = kn
        kv32_ref[:, NA_WIDTH:2 * NA_WIDTH] = vn
        kv32_ref[:, 2 * NA_WIDTH:2 * NA_WIDTH + KV_WIDTH] = kg
        kv32_ref[:, 2 * NA_WIDTH + KV_WIDTH:2 * NA_WIDTH + 2 * KV_WIDTH] = vg


QKV_WIDTH = 3 * NA_WIDTH + GQA_WIDTH + 2 * KV_WIDTH
KV32_WIDTH = 2 * NA_WIDTH + 2 * KV_WIDTH
QN_BLK, KN_BLK, VN_BLK, QG_BLK, KG_BLK, VG_BLK = 0, 3, 6, 9, 12, 13


def _rope_block(i):
    n_ctx = T_CTX // TM
    return jnp.where(i < n_ctx, 0, 1 + (i - n_ctx) % (DEC_SEQ // TM))


def _in_proj(x, moe_residual, mods3, norm_g, w_in_bf, q_gain, k_gain, gsum, rope_cos, rope_sin):
    n_ctx = T_CTX // TM
    row_spec = pl.BlockSpec((TM, D_MODEL), lambda i: (i, 0))
    mod_spec = pl.BlockSpec((1, 1, 6 * D_MODEL), lambda i: (_modset(i), 0, 0))
    args, in_specs = [x], [row_spec]
    out_specs = [pl.BlockSpec((TM, SSM_WIDTH), lambda i: (i, 0)),
                 pl.BlockSpec((TM, QKV_WIDTH), lambda i: (i, 0)),
                 pl.BlockSpec((TM, KV32_WIDTH), lambda i: (jnp.minimum(i, n_ctx - 1), 0))]
    out_shape = [jax.ShapeDtypeStruct((T_ALL, SSM_WIDTH), jnp.float32),
                 jax.ShapeDtypeStruct((T_ALL, QKV_WIDTH), jnp.bfloat16),
                 jax.ShapeDtypeStruct((T_CTX, KV32_WIDTH), jnp.float32)]
    if moe_residual is not None:
        args += list(moe_residual)
        in_specs += [row_spec, mod_spec]
        out_specs.append(row_spec)
        out_shape.append(jax.ShapeDtypeStruct((T_ALL, D_MODEL), jnp.float32))
    args += [mods3, norm_g, w_in_bf, q_gain, k_gain, gsum, rope_cos, rope_sin]
    in_specs += [mod_spec,
                 pl.BlockSpec((1, D_MODEL), lambda i: (0, 0)),
                 pl.BlockSpec((D_MODEL, IN_WIDTH), lambda i: (0, 0)),
                 pl.BlockSpec((1, LANES), lambda i: (0, 0)),
                 pl.BlockSpec((1, LANES), lambda i: (0, 0)),
                 pl.BlockSpec((LANES, LANES), lambda i: (0, 0)),
                 pl.BlockSpec((TM, LANES), lambda i: (_rope_block(i), 0)),
                 pl.BlockSpec((TM, LANES), lambda i: (_rope_block(i), 0))]
    outs = pl.pallas_call(
        functools.partial(_inproj_kernel, moe_residual is not None),
        grid=(T_ALL // TM,),
        in_specs=in_specs,
        out_specs=out_specs,
        out_shape=out_shape,
        compiler_params=_cparams("arbitrary"),
        name="in_proj",
    )(*args)
    return outs if moe_residual is not None else (*outs, x)


def _s5scan_kernel(u_ref, b_ref, a_ref, c_ref, y_ref, p_ref, fin_ref, hbuf):
    d = pl.program_id(0)
    k = pl.program_id(1)
    rows = S5_TB * S5_ROWS
    a_re = jnp.broadcast_to(a_ref[0, :, 0:S5_HALF], (SUBLANES, S5_HALF))
    a_im = jnp.broadcast_to(a_ref[0, :, S5_HALF:S5_CPLX], (SUBLANES, S5_HALF))
    ub = u_ref[...].astype(jnp.bfloat16)

    def run(forward):
        data0 = S5_ROWS if forward else 0
        prev0 = 0 if forward else rows
        last0 = rows if forward else 0

        @pl.when(k == 0)
        def _():
            r = lax.broadcasted_iota(jnp.int32, (S5_ROWS, S5_CPLX), 0)
            l = lax.broadcasted_iota(jnp.int32, (S5_ROWS, S5_CPLX), 1)
            hbuf[prev0:prev0 + S5_ROWS, :] = jnp.where((r == S5_POW_ROW) & (l < S5_HALF), 1.0, 0.0)

        @pl.when(k > 0)
        def _():
            hbuf[prev0:prev0 + S5_ROWS, :] = hbuf[last0:last0 + S5_ROWS, :]

        hbuf[data0:data0 + rows, :] = _bdot(ub, b_ref[0])

        def step(i, carry):
            t = i if forward else S5_TB - 1 - i
            cur = pl.multiple_of(data0 + t * S5_ROWS, SUBLANES)
            prv = pl.multiple_of(cur - S5_ROWS if forward else cur + S5_ROWS, SUBLANES)
            for st in range(S5_ROWS // SUBLANES):
                rc = pl.ds(cur + st * SUBLANES, SUBLANES)
                rp = pl.ds(prv + st * SUBLANES, SUBLANES)
                hr = hbuf[rp, 0:S5_HALF]
                hi = hbuf[rp, S5_HALF:S5_CPLX]
                nr = a_re * hr - a_im * hi + hbuf[rc, 0:S5_HALF]
                ni = a_re * hi + a_im * hr + hbuf[rc, S5_HALF:S5_CPLX]
                hbuf[rc, 0:S5_HALF] = nr
                hbuf[rc, S5_HALF:S5_CPLX] = ni
            return carry

        lax.fori_loop(0, S5_TB, step, 0, unroll=True)
        y_ref[0] = _bdot(hbuf[data0:data0 + rows, :].astype(jnp.bfloat16), c_ref[0])
        for t in range(S5_TB):
            r0 = data0 + t * S5_ROWS + S5_POW_ROW
            p_ref[0, t:t + 1, :] = hbuf[r0:r0 + 1, :]

        @pl.when(k == S5_NBLK - 1)
        def _():
            fin_ref[0] = hbuf[last0:last0 + S5_ROWS, :]

    @pl.when(d == 0)
    def _():
        run(True)

    @pl.when(d == 1)
    def _():
        run(False)


def _s5_scan(u_tm, bblk, abar, cblk):
    rows = S5_TB * S5_ROWS

    def tblk(d, k):
        return jnp.where(d == 0, k, S5_NBLK - 1 - k)

    return pl.pallas_call(
        _s5scan_kernel,
        grid=(2, S5_NBLK),
        in_specs=[pl.BlockSpec((rows, SSM_WIDTH), lambda d, k: (tblk(d, k), 0)),
                  pl.BlockSpec((1, SSM_WIDTH, S5_CPLX), lambda d, k: (d, 0, 0)),
                  pl.BlockSpec((1, 1, S5_CPLX), lambda d, k: (d, 0, 0)),
                  pl.BlockSpec((1, S5_CPLX, SSM_WIDTH), lambda d, k: (d, 0, 0))],
        out_specs=[pl.BlockSpec((1, rows, SSM_WIDTH), lambda d, k: (d, tblk(d, k), 0)),
                   pl.BlockSpec((1, S5_TB, S5_CPLX), lambda d, k: (d, tblk(d, k), 0)),
                   pl.BlockSpec((1, S5_ROWS, S5_CPLX), lambda d, k: (d, 0, 0))],
        out_shape=(jax.ShapeDtypeStruct((2, S5_CHUNK * S5_ROWS, SSM_WIDTH), jnp.float32),
                   jax.ShapeDtypeStruct((2, S5_CHUNK, S5_CPLX), jnp.float32),
                   jax.ShapeDtypeStruct((2, S5_ROWS, S5_CPLX), jnp.float32)),
        scratch_shapes=[pltpu.VMEM((rows + S5_ROWS, S5_CPLX), jnp.float32)],
        compiler_params=_cparams("arbitrary", "arbitrary"),
        name="s5_scan",
    )(u_tm, bblk, abar, cblk)


def _cmul(pr, pi, cr, ci):
    return pr * cr - pi * ci, pr * ci + pi * cr


def _s5fin_kernel(u_ref, yf_ref, yb_ref, p_ref, fin_ref, h0_ref, c_ref, d_ref, wg_ref, bg_ref,
                  o_ref, carry, hc, ybuf):
    k = pl.program_id(0)
    nchunk = DEC_SEQ // S5_CHUNK

    @pl.when(k == 0)
    def _():
        for d in range(2):
            t_full = S5_CHUNK - 1 if d == 0 else 0
            pr = p_ref[d, t_full:t_full + 1, 0:S5_HALF]
            pi = p_ref[d, t_full:t_full + 1, S5_HALF:S5_CPLX]
            for b in range(DEC_BATCH):
                cr = h0_ref[d, b:b + 1, 0:S5_HALF]
                ci = h0_ref[d, b:b + 1, S5_HALF:S5_CPLX]
                order = range(nchunk) if d == 0 else range(nchunk - 1, -1, -1)
                for c in order:
                    row = b * nchunk + c
                    carry[d, row:row + 1, 0:S5_HALF] = cr
                    carry[d, row:row + 1, S5_HALF:S5_CPLX] = ci
                    fr = fin_ref[d, S5_LAT0 + row:S5_LAT0 + row + 1, 0:S5_HALF]
                    fi = fin_ref[d, S5_LAT0 + row:S5_LAT0 + row + 1, S5_HALF:S5_CPLX]
                    mr, mi = _cmul(pr, pi, cr, ci)
                    cr, ci = fr + mr, fi + mi

    ybuf[...] = yf_ref[0] + yb_ref[0] + u_ref[...] * d_ref[...]
    for d in range(2):
        cr = carry[d, :, 0:S5_HALF]
        ci = carry[d, :, S5_HALF:S5_CPLX]
        for t in range(S5_TB):
            pr = p_ref[d, pl.ds(k * S5_TB + t, 1), 0:S5_HALF]
            pi = p_ref[d, pl.ds(k * S5_TB + t, 1), S5_HALF:S5_CPLX]
            mr, mi = _cmul(pr, pi, cr, ci)
            hc[t * S5_NLAT:(t + 1) * S5_NLAT, 0:S5_HALF] = mr.astype(jnp.bfloat16)
            hc[t * S5_NLAT:(t + 1) * S5_NLAT, S5_HALF:S5_CPLX] = mi.astype(jnp.bfloat16)
        corr = _bdot(hc[...], c_ref[d])
        ybuf[:, S5_LAT0:S5_LAT0 + S5_NLAT, :] += corr.reshape(S5_TB, S5_NLAT, SSM_WIDTH)

    y = ybuf[...].reshape(S5_TB * S5_ROWS, SSM_WIDTH)
    yg = 0.5 * y * (1.0 + jnp.tanh(math.sqrt(2.0 / math.pi) * (y + 0.044715 * (y * y * y))))
    z = _bdot(yg.astype(jnp.bfloat16), wg_ref[...]) + bg_ref[...]
    out = yg / (1.0 + jnp.exp(-z))
    o_ref[...] = out.reshape(S5_TB, S5_ROWS, SSM_WIDTH).astype(jnp.bfloat16)


def _s5_finish(u_tm3, y_dir4, powers, finals, h0, cblk, ssm_d, w_glu_bf, b_glu):
    blk3 = (S5_TB, S5_ROWS, SSM_WIDTH)
    return pl.pallas_call(
        _s5fin_kernel,
        grid=(S5_NBLK,),
        in_specs=[pl.BlockSpec(blk3, lambda k: (k, 0, 0)),
                  pl.BlockSpec((1,) + blk3, lambda k: (0, k, 0, 0)),
                  pl.BlockSpec((1,) + blk3, lambda k: (1, k, 0, 0)),
                  pl.BlockSpec((2, S5_CHUNK, S5_CPLX), lambda k: (0, 0, 0)),
                  pl.BlockSpec((2, S5_ROWS, S5_CPLX), lambda k: (0, 0, 0)),
                  pl.BlockSpec((2, DEC_BATCH, S5_CPLX), lambda k: (0, 0, 0)),
                  pl.BlockSpec((2, S5_CPLX, SSM_WIDTH), lambda k: (0, 0, 0)),
                  pl.BlockSpec((1, SSM_WIDTH), lambda k: (0, 0)),
                  pl.BlockSpec((SSM_WIDTH, SSM_WIDTH), lambda k: (0, 0)),
                  pl.BlockSpec((1, SSM_WIDTH), lambda k: (0, 0))],
        out_specs=pl.BlockSpec(blk3, lambda k: (k, 0, 0)),
        out_shape=jax.ShapeDtypeStruct((S5_CHUNK, S5_ROWS, SSM_WIDTH), jnp.bfloat16),
        scratch_shapes=[pltpu.VMEM((2, S5_NLAT, S5_CPLX), jnp.float32),
                        pltpu.VMEM((S5_TB * S5_NLAT, S5_CPLX), jnp.bfloat16),
                        pltpu.VMEM(blk3, jnp.float32)],
        compiler_params=_cparams("arbitrary"),
        name="s5_finish",
    )(u_tm3, y_dir4, y_dir4, powers, finals, h0, cblk, ssm_d, w_glu_bf, b_glu)


def _pair_attention(q, kvs, biases):
    lane = lax.broadcasted_iota(jnp.int32, (1, PAIR), 1)
    outs = []
    for a in range(2):
        sel = (lane < HEAD_DIM) if a == 0 else (lane >= HEAD_DIM)
        qa = jnp.where(sel, q, jnp.zeros_like(q))
        ss = []
        for (kk, _), bias in zip(kvs, biases):
            s = _bdot_t(qa, kk)
            if bias is not None:
                s = s + bias[a]
            ss.append(s)
        m = ss[0].max(axis=-1, keepdims=True)
        for s in ss[1:]:
            m = jnp.maximum(m, s.max(axis=-1, keepdims=True))
        den = None
        acc = None
        for s, (_, vv) in zip(ss, kvs):
            p = jnp.exp(s - m)
            l = p.sum(axis=-1, keepdims=True)
            o = _bdot(p.astype(jnp.bfloat16), vv)
            den = l if den is None else den + l
            acc = o if acc is None else acc + o
        outs.append(acc / den)
    return jnp.where(lane < HEAD_DIM, outs[0], outs[1])


ATTN_WIDTH = NA_WIDTH + GQA_WIDTH
NPAIR = NA_WIDTH // PAIR


def _attn_ctx_kernel(qkv_ref, o_ref):
    def blk(j):
        return qkv_ref[:, j * PAIR:(j + 1) * PAIR]

    for p in range(NPAIR):
        o = _pair_attention(blk(QN_BLK + p), [(blk(KN_BLK + p), blk(VN_BLK + p))], [None])
        o_ref[:, p * PAIR:(p + 1) * PAIR] = o.astype(jnp.bfloat16)
    for p in range(NPAIR):
        o = _pair_attention(blk(QG_BLK + p), [(blk(KG_BLK), blk(VG_BLK))], [None])
        o_ref[:, (NPAIR + p) * PAIR:(NPAIR + p + 1) * PAIR] = o.astype(jnp.bfloat16)


def _attn_ctx(qkv):
    return pl.pallas_call(
        _attn_ctx_kernel,
        grid=(BATCH,),
        in_specs=[pl.BlockSpec((SEQ, QKV_WIDTH), lambda b: (b, 0))],
        out_specs=pl.BlockSpec((SEQ, ATTN_WIDTH), lambda b: (b, 0)),
        out_shape=jax.ShapeDtypeStruct((T_CTX, ATTN_WIDTH), jnp.bfloat16),
        compiler_params=_cparams("arbitrary"),
        name="attn_ctx",
    )(qkv)


TQ = 256


def _gqa_lat_kernel(q_ref, k_ref, v_ref, ck_ref, cv_ref, o_ref):
    kvs = [(ck_ref[0], cv_ref[0]), (k_ref[...], v_ref[...])]
    o_ref[...] = _pair_attention(q_ref[...], kvs, [None, None]).astype(jnp.bfloat16)


def _gqa_lat(qkv, ck, cv):
    nq = DEC_SEQ // TQ
    ctx_blocks = T_CTX // DEC_SEQ
    return pl.pallas_call(
        _gqa_lat_kernel,
        grid=(DEC_BATCH, NPAIR, nq),
        in_specs=[pl.BlockSpec((TQ, PAIR), lambda b, p, i: (T_CTX // TQ + b * nq + i, QG_BLK + p)),
                  pl.BlockSpec((DEC_SEQ, PAIR), lambda b, p, i: (ctx_blocks + b, KG_BLK)),
                  pl.BlockSpec((DEC_SEQ, PAIR), lambda b, p, i: (ctx_blocks + b, VG_BLK)),
                  pl.BlockSpec((1, PAST_LEN, PAIR), lambda b, p, i: (b, 0, 0)),
                  pl.BlockSpec((1, PAST_LEN, PAIR), lambda b, p, i: (b, 0, 0))],
        out_specs=pl.BlockSpec((TQ, PAIR), lambda b, p, i: (b * nq + i, p)),
        out_shape=jax.ShapeDtypeStruct((T_LAT, GQA_WIDTH), jnp.bfloat16),
        compiler_params=_cparams("arbitrary", "arbitrary", "arbitrary"),
        name="gqa_lat",
    )(qkv, qkv, qkv, ck, cv)


NA_BLK = 4
NA_NBLK = GRID_H // NA_BLK
NA_SLAB_ROWS = NA_KH + NA_BLK - 1
NA_SLAB = NA_SLAB_ROWS * GRID_W
NA_QROWS = NA_BLK * GRID_W
NA_BLOCK_TYPES = (0, 2, NA_NBLK - 1)


def _na_slab_start(j):
    return jnp.clip(j * NA_BLK - NA_KH // 2, 0, GRID_H - NA_SLAB_ROWS)


def _na_lat_kernel(q_ref, k_ref, v_ref, ck_ref, cv_ref, bias_ref, o_ref):
    ck, cv = ck_ref[0, 0], cv_ref[0, 0]
    lane = lax.broadcasted_iota(jnp.int32, (1, PAIR), 1)

    def block(j, carry):
        q0 = pl.multiple_of(j * NA_QROWS, NA_QROWS)
        k0 = pl.multiple_of(_na_slab_start(j) * GRID_W, GRID_W)
        btype = jnp.where(j == 0, 0, jnp.where(j == NA_NBLK - 1, 2, 1))
        q = q_ref[pl.ds(q0, NA_QROWS), :]
        lhs = jnp.concatenate([jnp.where(lane < HEAD_DIM, q, jnp.zeros_like(q)),
                               jnp.where(lane >= HEAD_DIM, q, jnp.zeros_like(q))], axis=0)
        ks = k_ref[pl.ds(k0, NA_SLAB), :]
        vs = v_ref[pl.ds(k0, NA_SLAB), :]
        s_lat = _bdot_t(lhs, ks) + bias_ref[0, btype]
        s_ctx = _bdot_t(lhs, ck)
        m = jnp.maximum(s_lat.max(axis=-1, keepdims=True), s_ctx.max(axis=-1, keepdims=True))
        p_lat = jnp.exp(s_lat - m)
        p_ctx = jnp.exp(s_ctx - m)
        den = p_lat.sum(axis=-1, keepdims=True) + p_ctx.sum(axis=-1, keepdims=True)
        o = (_bdot(p_lat.astype(jnp.bfloat16), vs) + _bdot(p_ctx.astype(jnp.bfloat16), cv)) / den
        out = jnp.where(lane < HEAD_DIM, o[0:NA_QROWS], o[NA_QROWS:2 * NA_QROWS])
        o_ref[pl.ds(q0, NA_QROWS), :] = out.astype(jnp.bfloat16)
        return carry

    lax.fori_loop(0, NA_NBLK, block, 0)


def _na_lat(qkv, ck, cv, bias):
    ctx_blocks = T_CTX // DEC_SEQ
    return pl.pallas_call(
        _na_lat_kernel,
        grid=(DEC_BATCH, NPAIR),
        in_specs=[pl.BlockSpec((DEC_SEQ, PAIR), lambda b, p: (ctx_blocks + b, QN_BLK + p)),
                  pl.BlockSpec((DEC_SEQ, PAIR), lambda b, p: (ctx_blocks + b, KN_BLK + p)),
                  pl.BlockSpec((DEC_SEQ, PAIR), lambda b, p: (ctx_blocks + b, VN_BLK + p)),
                  pl.BlockSpec((1, 1, PAST_LEN, PAIR), lambda b, p: (b, p, 0, 0)),
                  pl.BlockSpec((1, 1, PAST_LEN, PAIR), lambda b, p: (b, p, 0, 0)),
                  pl.BlockSpec((1, len(NA_BLOCK_TYPES), 2 * NA_QROWS, NA_SLAB), lambda b, p: (p, 0, 0, 0))],
        out_specs=pl.BlockSpec((DEC_SEQ, PAIR), lambda b, p: (b, p)),
        out_shape=jax.ShapeDtypeStruct((T_LAT, NA_WIDTH), jnp.bfloat16),
        compiler_params=_cparams("arbitrary", "arbitrary"),
        name="na_lat",
    )(qkv, qkv, qkv, ck, cv, bias)


def _na_bias_table(rpb):
    nt = len(NA_BLOCK_TYPES)
    j = np.array(NA_BLOCK_TYPES)[:, None]
    r = j * NA_BLK + np.arange(NA_BLK)[None, :]
    slab0 = np.clip(j * NA_BLK - NA_KH // 2, 0, GRID_H - NA_SLAB_ROWS)
    win0 = np.clip(r - NA_KH // 2, 0, GRID_H - NA_KH)
    key_row = slab0[:, :, None] + np.arange(NA_SLAB_ROWS)[None, None, :]
    row_in = (key_row >= win0[:, :, None]) & (key_row < win0[:, :, None] + NA_KH)
    dr = np.clip(key_row - r[:, :, None] + (NA_KH - 1), 0, 2 * NA_KH - 2)
    cq = np.arange(GRID_W)
    ck = np.arange(GRID_W)
    cs = np.clip(cq - NA_KW // 2, 0, GRID_W - NA_KW)
    col_in = (ck[None, :] >= cs[:, None]) & (ck[None, :] < cs[:, None] + NA_KW)
    dc = np.clip(ck[None, :] - cq[:, None] + NA_KW - 1, 0, 2 * NA_KW - 2)
    sel_r = jnp.asarray(np.eye(2 * NA_KH - 1, dtype=np.float32)[dr])
    sel_c = jnp.asarray(np.eye(2 * NA_KW - 1, dtype=np.float32)[dc])
    b = jnp.einsum("hrc,tnir,qkc->htnqik", rpb.astype(jnp.float32), sel_r, sel_c,
                   precision=lax.Precision.HIGHEST)
    keep = row_in[:, :, None, :, None] & col_in[None, None, :, None, :]
    b = jnp.where(jnp.asarray(keep)[None], b, NEG_INF)
    b = b.reshape(NA_HEADS // 2, 2, nt, NA_QROWS, NA_SLAB).transpose(0, 2, 1, 3, 4)
    return b.reshape(NA_HEADS // 2, nt, 2 * NA_QROWS, NA_SLAB)


GATE_LANE0 = N_GROUPS
GROUP_LANE = 0
_EPG_SHIFT = EXPERTS_PER_GROUP.bit_length() - 1
assert 1 << _EPG_SHIFT == EXPERTS_PER_GROUP


def _route(logits):
    lane_i = lax.broadcasted_iota(jnp.int32, logits.shape, 1)
    lane = lane_i.astype(jnp.float32)
    big = float(LANES)
    is_g = lane_i < N_GROUPS
    lg = jnp.where(is_g, logits, NEG_INF)
    mg = lg.max(axis=-1, keepdims=True)
    pg_top = 1.0 / jnp.where(is_g, jnp.exp(lg - mg), 0.0).sum(axis=-1, keepdims=True)
    g_idx = jnp.where(is_g & (lg == mg), lane, big).min(axis=-1, keepdims=True)
    in_e = (lane_i >= GATE_LANE0) & (lane_i < GATE_LANE0 + N_EXPERTS)
    lane_group = jnp.where(in_e, ((lane_i - GATE_LANE0) >> _EPG_SHIFT).astype(jnp.float32), -1.0)
    sel = lane_group == g_idx
    le = jnp.where(sel, logits, NEG_INF)
    me = le.max(axis=-1, keepdims=True)
    ex = jnp.where(sel, jnp.exp(le - me), 0.0)
    pe = ex / ex.sum(axis=-1, keepdims=True)
    p1 = pe.max(axis=-1, keepdims=True)
    i1 = jnp.where(sel & (pe == p1), lane, big).min(axis=-1, keepdims=True)
    rest = sel & (lane != i1)
    pr = jnp.where(rest, pe, -1.0)
    p2 = pr.max(axis=-1, keepdims=True)
    i2 = jnp.where(rest & (pr == p2), lane, big).min(axis=-1, keepdims=True)
    tot = p1 + p2
    within = jnp.where(lane == i1, p1 / tot, 0.0) + jnp.where(lane == i2, p2 / tot, 0.0)
    return pg_top * within + jnp.where(lane_i == GROUP_LANE, g_idx, 0.0)


HG_WIDTH = D_MODEL + LANES


def _outproj_kernel(x_ref, ys_ref, yc_ref, yn_ref, yg_ref, mod_ref, g_ref, w_ref, wr_hi_ref, wr_lo_ref, br_ref,
                    x1_ref, hg_ref):
    is_ctx = pl.program_id(0) < T_CTX // TM
    y_na = jnp.where(is_ctx, yc_ref[:, 0:NA_WIDTH], yn_ref[...])
    y_g = jnp.where(is_ctx, yc_ref[:, NA_WIDTH:ATTN_WIDTH], yg_ref[...])
    mix = (_bdot(ys_ref[...], w_ref[0:SSM_WIDTH, :])
           + _bdot(y_na, w_ref[SSM_WIDTH:SSM_WIDTH + NA_WIDTH, :])
           + _bdot(y_g, w_ref[SSM_WIDTH + NA_WIDTH:, :]))
    gate1 = mod_ref[0, :, 2 * D_MODEL:3 * D_MODEL]
    shift2 = mod_ref[0, :, 3 * D_MODEL:4 * D_MODEL]
    scale2 = mod_ref[0, :, 4 * D_MODEL:5 * D_MODEL]
    x1 = x_ref[...] + gate1 * mix
    x1_ref[...] = x1
    ms = jnp.mean(x1 * x1, axis=-1, keepdims=True)
    h2 = (x1 * lax.rsqrt(ms + EPS) * g_ref[...]) * (1.0 + scale2) + shift2
    hg_ref[:, 0:D_MODEL] = h2
    h_hi = h2.astype(jnp.bfloat16)
    h_lo = (h2 - h_hi.astype(jnp.float32)).astype(jnp.bfloat16)
    logits = (_bdot(h_hi, wr_hi_ref[...]) + _bdot(h_lo, wr_hi_ref[...]) + _bdot(h_hi, wr_lo_ref[...])
              + br_ref[...])
    hg_ref[:, D_MODEL:HG_WIDTH] = _route(logits)


def _out_proj(x, y_ssm, y_ctx, y_na_lat, y_g_lat, mods3, norm_g, w_out_bf, wr_hi, wr_lo, b_router):
    n_ctx = T_CTX // TM
    return pl.pallas_call(
        _outproj_kernel,
        grid=(T_ALL // TM,),
        in_specs=[pl.BlockSpec((TM, D_MODEL), lambda i: (i, 0)),
                  pl.BlockSpec((TM, SSM_WIDTH), lambda i: (i, 0)),
                  pl.BlockSpec((TM, ATTN_WIDTH), lambda i: (jnp.minimum(i, n_ctx - 1), 0)),
                  pl.BlockSpec((TM, NA_WIDTH), lambda i: (jnp.maximum(i - n_ctx, 0), 0)),
                  pl.BlockSpec((TM, GQA_WIDTH), lambda i: (jnp.maximum(i - n_ctx, 0), 0)),
                  pl.BlockSpec((1, 1, 6 * D_MODEL), lambda i: (_modset(i), 0, 0)),
                  pl.BlockSpec((1, D_MODEL), lambda i: (0, 0)),
                  pl.BlockSpec((D_MODEL, D_MODEL), lambda i: (0, 0)),
                  pl.BlockSpec((D_MODEL, LANES), lambda i: (0, 0)),
                  pl.BlockSpec((D_MODEL, LANES), lambda i: (0, 0)),
                  pl.BlockSpec((1, LANES), lambda i: (0, 0))],
        out_specs=[pl.BlockSpec((TM, D_MODEL), lambda i: (i, 0)),
                   pl.BlockSpec((TM, HG_WIDTH), lambda i: (i, 0))],
        out_shape=(jax.ShapeDtypeStruct((T_ALL, D_MODEL), jnp.float32),
                   jax.ShapeDtypeStruct((T_ALL, HG_WIDTH), jnp.float32)),
        compiler_params=_cparams("arbitrary"),
        name="out_proj",
    )(x, y_ssm, y_ctx, y_na_lat, y_g_lat, mods3, norm_g, w_out_bf, wr_hi, wr_lo, b_router)


TMOE = 256
NT_MOE = T_ALL // TMOE + N_GROUPS
MOE_SLOTS = NT_MOE * TMOE


def _moe_dispatch(group_f):
    i32 = jnp.int32
    g = group_f.astype(i32)
    onehot = (g[:, None] == jnp.arange(N_GROUPS, dtype=i32)[None, :]).astype(i32)
    csum = jnp.cumsum(onehot, axis=0)
    counts = csum[-1]
    rank = jnp.sum(onehot * csum, axis=1) - 1
    tiles = (counts + TMOE - 1) // TMOE
    tile_end = jnp.cumsum(tiles)
    tile_start = tile_end - tiles
    slot = jnp.sum(onehot * tile_start[None, :], axis=1) * TMOE + rank
    tok = jnp.full((MOE_SLOTS,), -1, i32).at[slot].set(jnp.arange(T_ALL, dtype=i32))
    pad = tok < 0
    pad_rank = jnp.cumsum(pad.astype(i32)) - 1
    row_of_slot = jnp.where(pad, T_ALL + pad_rank, tok)
    tile_group = jnp.minimum(
        jnp.sum(jnp.arange(NT_MOE, dtype=i32)[:, None] >= tile_end[None, :], axis=1), N_GROUPS - 1).astype(i32)
    return tile_group, row_of_slot


def _moe_kernel(tg_ref, row_ref, hg_hbm, wg_ref, wu_ref, wd_ref, y_hbm,
                gbuf, ybuf, wg_bf, wu_bf, wd_bf, gsem, ssem):
    i = pl.program_id(0)
    slot = lax.rem(i, 2)
    other = 1 - slot
    last = NT_MOE - 1
    nxt = jnp.minimum(i + 1, last)
    prv = jnp.maximum(i - 1, 0)

    def gather_row(row, buf_slot, r):
        return pltpu.make_async_copy(hg_hbm.at[pl.ds(row, 1)], gbuf.at[buf_slot, pl.ds(r, 1)], gsem.at[buf_slot])

    def scatter_row(row, buf_slot, r):
        return pltpu.make_async_copy(ybuf.at[buf_slot, pl.ds(r, 1)], y_hbm.at[pl.ds(row, 1)], ssem.at[buf_slot])

    def wait_gather(buf_slot):
        pltpu.make_async_copy(hg_hbm.at[pl.ds(0, TMOE)], gbuf.at[buf_slot], gsem.at[buf_slot]).wait()

    def wait_scatter(buf_slot):
        pltpu.make_async_copy(ybuf.at[buf_slot], y_hbm.at[pl.ds(0, TMOE)], ssem.at[buf_slot]).wait()

    def start_gather_row(tile, buf_slot, r):
        row = row_ref[tile * TMOE + r]
        gather_row(jnp.where(row < T_ALL, row, 0), buf_slot, r).start()

    @pl.when(i == 0)
    def _():
        ybuf[1] = jnp.zeros((TMOE, D_MODEL), jnp.float32)
        for r in range(TMOE):
            start_gather_row(0, 0, r)

    first_of_group = (i == 0) | (tg_ref[i] != tg_ref[prv])

    @pl.when(first_of_group)
    def _():
        wg_bf[...] = wg_ref[...].astype(jnp.bfloat16)
        wu_bf[...] = wu_ref[...].astype(jnp.bfloat16)
        wd_bf[...] = wd_ref[...].astype(jnp.bfloat16)

    wait_gather(slot)

    h = gbuf[slot, :, 0:D_MODEL].astype(jnp.bfloat16)
    routing = gbuf[slot, :, D_MODEL:HG_WIDTH]
    lane = lax.broadcasted_iota(jnp.int32, (1, LANES), 1)
    lane0 = GATE_LANE0 + tg_ref[i] * EXPERTS_PER_GROUP
    rows_per_chunk = TMOE // EXPERTS_PER_GROUP
    y = None
    for j in range(EXPERTS_PER_GROUP):
        for r in range(j * rows_per_chunk, (j + 1) * rows_per_chunk):
            start_gather_row(nxt, other, r)
            dst = jnp.where(i > 0, row_ref[prv * TMOE + r], MOE_SLOTS + r)
            scatter_row(dst, other, r).start()
        gate = jnp.where(lane == lane0 + j, routing, 0.0).sum(axis=-1, keepdims=True)
        aj = _bdot(h, wg_bf[j])
        bj = _bdot(h, wu_bf[j])
        act = ((aj / (1.0 + jnp.exp(-aj))) * bj * gate).astype(jnp.bfloat16)
        yj = _bdot(act, wd_bf[j])
        y = yj if y is None else y + yj

    @pl.when(i > 0)
    def _():
        wait_scatter(slot)

    ybuf[slot] = y

    @pl.when(i == last)
    def _():
        for r in range(TMOE):
            scatter_row(row_ref[last * TMOE + r], slot, r).start()
        wait_scatter(other)
        wait_scatter(slot)
        wait_gather(other)


def _moe(hg, tile_group, row_of_slot, w_gate, w_up, w_down):
    def wspec(shape):
        return pl.BlockSpec((EXPERTS_PER_GROUP,) + shape, lambda i, tg, rows: (tg[i], 0, 0))

    grid_spec = pltpu.PrefetchScalarGridSpec(
        num_scalar_prefetch=2,
        grid=(NT_MOE,),
        in_specs=[pl.BlockSpec(memory_space=pl.ANY),
                  wspec((D_MODEL, EXPERT_HIDDEN)), wspec((D_MODEL, EXPERT_HIDDEN)),
                  wspec((EXPERT_HIDDEN, D_MODEL))],
        out_specs=pl.BlockSpec(memory_space=pl.ANY),
        scratch_shapes=[pltpu.VMEM((2, TMOE, HG_WIDTH), jnp.float32),
                        pltpu.VMEM((2, TMOE, D_MODEL), jnp.float32),
                        pltpu.VMEM((EXPERTS_PER_GROUP, D_MODEL, EXPERT_HIDDEN), jnp.bfloat16),
                        pltpu.VMEM((EXPERTS_PER_GROUP, D_MODEL, EXPERT_HIDDEN), jnp.bfloat16),
                        pltpu.VMEM((EXPERTS_PER_GROUP, EXPERT_HIDDEN, D_MODEL), jnp.bfloat16),
                        pltpu.SemaphoreType.DMA((2,)),
                        pltpu.SemaphoreType.DMA((2,))])
    return pl.pallas_call(
        _moe_kernel,
        grid_spec=grid_spec,
        out_shape=jax.ShapeDtypeStruct((MOE_SLOTS + TMOE, D_MODEL), jnp.float32),
        compiler_params=_cparams("arbitrary"),
        name="moe",
    )(tile_group, row_of_slot, hg, w_gate, w_up, w_down)


def _final_kernel(x1_ref, y_ref, mod_ref, g_ref, o_ref):
    x = x1_ref[...] + mod_ref[0, :, 5 * D_MODEL:6 * D_MODEL] * y_ref[...]
    ms = jnp.mean(x * x, axis=-1, keepdims=True)
    o_ref[...] = x * lax.rsqrt(ms + EPS) * g_ref[...]


def _final_norm(x1, y, mods3, g):
    return pl.pallas_call(
        _final_kernel,
        grid=(T_ALL // TM,),
        in_specs=[pl.BlockSpec((TM, D_MODEL), lambda i: (i, 0)),
                  pl.BlockSpec((TM, D_MODEL), lambda i: (i, 0)),
                  pl.BlockSpec((1, 1, 6 * D_MODEL), lambda i: (_modset(i), 0, 0)),
                  pl.BlockSpec((1, D_MODEL), lambda i: (0, 0))],
        out_specs=pl.BlockSpec((TM, D_MODEL), lambda i: (i, 0)),
        out_shape=jax.ShapeDtypeStruct((T_ALL, D_MODEL), jnp.float32),
        compiler_params=_cparams("arbitrary"),
        name="final_norm",
    )(x1, y, mods3, g)


_QG_ORDER = np.array([0, 3, 1, 4, 2, 5])


def _permute_qg_columns(w_in):
    c1 = SSM_WIDTH + 3 * NA_WIDTH
    qg = w_in[..., c1:c1 + GQA_WIDTH].reshape(*w_in.shape[:-1], GQA_HEADS, HEAD_DIM)
    qg = qg[..., _QG_ORDER, :].reshape(*w_in.shape[:-1], GQA_WIDTH)
    return jnp.concatenate([w_in[..., :c1], qg, w_in[..., c1 + GQA_WIDTH:]], axis=-1)


def _permute_yg_rows(w_out):
    r0 = SSM_WIDTH + NA_WIDTH
    yg = w_out[:, r0:, :].reshape(DEPTH, GQA_HEADS, HEAD_DIM, D_MODEL)[:, _QG_ORDER]
    return jnp.concatenate([w_out[:, :r0, :], yg.reshape(DEPTH, GQA_WIDTH, D_MODEL)], axis=1)


def _rope_tables():
    t = jnp.arange(DEC_SEQ)
    row = (t // GRID_W).astype(jnp.float32)
    col = (t % GRID_W).astype(jnp.float32)
    freqs = ROPE_THETA ** (-jnp.arange(ROPE_F, dtype=jnp.float32) / ROPE_F)
    ang_r = row[:, None] * freqs
    ang_c = col[:, None] * freqs
    ang = jnp.concatenate([ang_r, ang_r, ang_c, ang_c], axis=1)
    cos = jnp.cos(ang)
    sin = jnp.sin(ang)
    sign = jnp.tile(jnp.concatenate([-jnp.ones(ROPE_F), jnp.ones(ROPE_F)]), 2).astype(jnp.float32)
    sin = sin * sign
    cos = jnp.tile(cos, (1, 2))
    sin = jnp.tile(sin, (1, 2))
    cos = jnp.concatenate([jnp.ones((TM, LANES), jnp.float32), cos], axis=0)
    sin = jnp.concatenate([jnp.zeros((TM, LANES), jnp.float32), sin], axis=0)
    return cos, sin


def _pair_layout(cache):
    b, l, h, n, dh = cache.shape
    c = cache.reshape(b, l, h // 2, 2, n, dh).transpose(1, 0, 2, 4, 3, 5)
    return c.reshape(l, b, h // 2, n, 2 * dh).astype(jnp.bfloat16)


def _block_diag_b(bbar):
    bb = bbar.reshape(DEPTH, 2, SSM_GROUP_CH, SSM_GROUPS, SSM_STATE)
    eye = jnp.eye(SSM_GROUPS, dtype=bbar.dtype)
    out = eye[None, None, :, None, :, None] * bb.transpose(0, 1, 3, 2, 4)[:, :, :, :, None, :]
    return out.reshape(DEPTH, 2, SSM_WIDTH, S5_HALF)


def _block_diag_c(c):
    eye = jnp.eye(SSM_GROUPS, dtype=c.dtype)
    out = eye[None, None, :, None, :, None] * c.transpose(0, 1, 2, 4, 3)[:, :, :, :, None, :]
    return out.reshape(DEPTH, 2, S5_HALF, SSM_WIDTH)


def kernel(x_prompt, x_sample, cache_na_k, cache_na_v, cache_gqa_k, cache_gqa_v, state_ssm, c, c_ctx, norm_mix_g, norm_ffn_g, w_ada, b_ada, w_in, ssm_a_re, ssm_a_im, ssm_log_dt, ssm_b_re, ssm_b_im, ssm_c_re, ssm_c_im, ssm_d, w_glu, b_glu, na_rpb, q_norm_g, k_norm_g, w_out, moe_w_group, moe_b_group, moe_w_expert, moe_b_expert, moe_w_gate, moe_w_up, moe_w_down, final_norm_g):
    f32, bf16 = jnp.float32, jnp.bfloat16
    x = jnp.concatenate([x_prompt.reshape(T_CTX, D_MODEL), x_sample.reshape(T_LAT, D_MODEL)], axis=0)

    cvec8 = jnp.concatenate([c_ctx[None, :], c, jnp.zeros((SUBLANES - N_MODSETS, D_MODEL), f32)], axis=0)
    mods_all = _adaln_all(cvec8, w_ada, b_ada)

    w_in_bf = _permute_qg_columns(w_in).astype(bf16)
    w_out_bf = _permute_yg_rows(w_out).astype(bf16)
    w_glu_bf = w_glu.astype(bf16)
    w_router = jnp.concatenate(
        [moe_w_group, moe_w_expert, jnp.zeros((DEPTH, D_MODEL, LANES - N_GROUPS - N_EXPERTS), f32)], axis=-1)
    wr_hi = w_router.astype(bf16)
    wr_lo = (w_router - wr_hi.astype(f32)).astype(bf16)
    b_router = jnp.concatenate(
        [moe_b_group, moe_b_expert, jnp.zeros((DEPTH, LANES - N_GROUPS - N_EXPERTS), f32)], axis=-1)
    q_gain = jnp.tile(q_norm_g, (1, 2))
    k_gain = jnp.tile(k_norm_g, (1, 2))
    lane = np.arange(LANES)
    gsum = jnp.asarray((lane[:, None] // HEAD_DIM) == (lane[None, :] // HEAD_DIM), bf16)
    rope_cos, rope_sin = _rope_tables()

    abar_re, abar_im, bbar_re, bbar_im = _s5_prep(ssm_a_re, ssm_a_im, ssm_log_dt, ssm_b_re, ssm_b_im)
    abar = jnp.concatenate([abar_re, abar_im], axis=-1).reshape(DEPTH, 2, 1, S5_CPLX)
    bblk = jnp.concatenate([_block_diag_b(bbar_re), _block_diag_b(bbar_im)], axis=-1).astype(bf16)
    cblk = jnp.concatenate([_block_diag_c(ssm_c_re), -_block_diag_c(ssm_c_im)], axis=2).astype(bf16)
    h0 = state_ssm.transpose(1, 2, 0, 5, 3, 4).reshape(DEPTH, 2, DEC_BATCH, S5_CPLX)

    ck_na, cv_na = _pair_layout(cache_na_k), _pair_layout(cache_na_v)
    ck_g, cv_g = _pair_layout(cache_gqa_k)[:, :, 0], _pair_layout(cache_gqa_v)[:, :, 0]
    na_bias = jax.vmap(_na_bias_table)(na_rpb)

    kv32_l, fin_l = [], []
    moe_residual = None
    for l in range(DEPTH):
        mods3 = mods_all[l, :N_MODSETS].reshape(N_MODSETS, 1, 6 * D_MODEL)
        u, qkv, kv32, x = _in_proj(x, moe_residual, mods3, norm_mix_g[l][None], w_in_bf[l], q_gain[l][None],
                                   k_gain[l][None], gsum, rope_cos, rope_sin)
        kv32_l.append(kv32)

        u_tm = jnp.pad(u.reshape(S5_SEQS, S5_CHUNK, SSM_WIDTH).transpose(1, 0, 2),
                       ((0, 0), (0, S5_ROWS - S5_SEQS), (0, 0)))
        y_dir, powers, finals = _s5_scan(u_tm.reshape(S5_CHUNK * S5_ROWS, SSM_WIDTH), bblk[l], abar[l], cblk[l])
        fin_l.append(finals[:, :BATCH])
        y_tm = _s5_finish(u_tm, y_dir.reshape(2, S5_CHUNK, S5_ROWS, SSM_WIDTH), powers, finals, h0[l], cblk[l],
                          ssm_d[l][None], w_glu_bf[l], b_glu[l][None])
        y_ssm = y_tm[:, :S5_SEQS].transpose(1, 0, 2).reshape(T_ALL, SSM_WIDTH)

        y_ctx = _attn_ctx(qkv)
        y_g_lat = _gqa_lat(qkv, ck_g[l], cv_g[l])
        y_na_lat = _na_lat(qkv, ck_na[l], cv_na[l], na_bias[l])

        x1, hg = _out_proj(x, y_ssm, y_ctx, y_na_lat, y_g_lat, mods3, norm_ffn_g[l][None], w_out_bf[l],
                           wr_hi[l], wr_lo[l], b_router[l][None])
        tile_group, row_of_slot = _moe_dispatch(hg[:, D_MODEL + GROUP_LANE])
        y_moe = _moe(hg, tile_group, row_of_slot, moe_w_gate[l], moe_w_up[l], moe_w_down[l])
        x, moe_residual = x1, (y_moe, mods3)

    y = _final_norm(x, moe_residual[0], moe_residual[1], final_norm_g[None])
    y_prompt = y[:T_CTX].reshape(BATCH, SEQ, D_MODEL)
    y_sample = y[T_CTX:].reshape(DEC_BATCH, DEC_SEQ, D_MODEL)

    kv32 = jnp.stack(kv32_l, axis=0).reshape(DEPTH, BATCH, SEQ, KV32_WIDTH)

    def heads(t, n_heads):
        return t.reshape(DEPTH, BATCH, SEQ, n_heads, HEAD_DIM).transpose(1, 0, 3, 2, 4)

    new_na_k = heads(kv32[..., 0:NA_WIDTH], NA_HEADS)
    new_na_v = heads(kv32[..., NA_WIDTH:2 * NA_WIDTH], NA_HEADS)
    new_gqa_k = heads(kv32[..., 2 * NA_WIDTH:2 * NA_WIDTH + KV_WIDTH], GQA_KV_HEADS)
    new_gqa_v = heads(kv32[..., 2 * NA_WIDTH + KV_WIDTH:], GQA_KV_HEADS)
    fin = jnp.stack(fin_l, axis=0).reshape(DEPTH, 2, BATCH, 2, SSM_GROUPS, SSM_STATE)
    new_state = fin.transpose(2, 0, 1, 4, 5, 3)
    return (y_prompt, y_sample, new_na_k, new_na_v, new_gqa_k, new_gqa_v, new_state)
```

```python
import functools
import math

import jax
import jax.numpy as jnp
import numpy as np
from jax import lax
from jax.experimental import pallas as pl
from jax.experimental.pallas import tpu as pltpu

D_MODEL = 1024
BATCH = 16
SEQ = 256
DEPTH = 4
DEC_BATCH = 2
DEC_SEQ = 4096
PAST_LEN = 512
GRID_W = 64
GRID_H = DEC_SEQ // GRID_W
HEAD_DIM = 64
SSM_WIDTH = 256
SSM_GROUP_CH = 16
SSM_GROUPS = 16
SSM_STATE = 64
NA_HEADS = 6
NA_WIDTH = 384
NA_KH = 8
NA_KW = 16
GQA_HEADS = 6
GQA_KV_HEADS = 2
GQA_REP = 3
GQA_WIDTH = 384
KV_WIDTH = 128
IN_WIDTH = 2048
ROPE_THETA = 10000.0
ROPE_F = 16
N_GROUPS = 4
EXPERTS_PER_GROUP = 4
N_EXPERTS = 16
EXPERT_HIDDEN = 256
EPS = 1e-6
NEG_INF = -1e30

T_CTX = BATCH * SEQ
T_LAT = DEC_BATCH * DEC_SEQ
T_ALL = T_CTX + T_LAT
N_MODSETS = 1 + DEC_BATCH
LANES = 128
SUBLANES = 8
PAIR = 2 * HEAD_DIM
ATTN_SCALE = HEAD_DIM ** -0.5

S5_CHUNK = 256
S5_SEQS = T_ALL // S5_CHUNK
S5_ROWS = 56
S5_POW_ROW = 48
S5_LAT0 = BATCH
S5_NLAT = S5_SEQS - BATCH
S5_TB = 16
S5_NBLK = S5_CHUNK // S5_TB
S5_CPLX = 2 * SSM_GROUPS * SSM_STATE
S5_HALF = SSM_GROUPS * SSM_STATE

TM = 512
VMEM_LIMIT = 56 * 1024 * 1024


def _cparams(*sem):
    return pltpu.CompilerParams(dimension_semantics=sem, vmem_limit_bytes=VMEM_LIMIT)


def _modset(i):
    return jnp.where(i < T_CTX // TM, 0, 1 + (i - T_CTX // TM) // (DEC_SEQ // TM))


def _bdot(a, b):
    return jnp.dot(a, b, preferred_element_type=jnp.float32)


def _bdot_t(a, b):
    return lax.dot_general(a, b, (((1,), (1,)), ((), ())), preferred_element_type=jnp.float32)


def _ada_kernel(c_ref, w_ref, b_ref, o_ref):
    c = c_ref[...]
    s = c / (1.0 + jnp.exp(-c))
    o_ref[0] = _bdot(s.astype(jnp.bfloat16), w_ref[0].astype(jnp.bfloat16)) + b_ref[0]


def _adaln_all(cvec8, w_ada, b_ada):
    tn = 1024
    return pl.pallas_call(
        _ada_kernel,
        grid=(DEPTH, 6 * D_MODEL // tn),
        in_specs=[pl.BlockSpec((SUBLANES, D_MODEL), lambda l, j: (0, 0)),
                  pl.BlockSpec((1, D_MODEL, tn), lambda l, j: (l, 0, j)),
                  pl.BlockSpec((1, 1, tn), lambda l, j: (l, 0, j))],
        out_specs=pl.BlockSpec((1, SUBLANES, tn), lambda l, j: (l, 0, j)),
        out_shape=jax.ShapeDtypeStruct((DEPTH, SUBLANES, 6 * D_MODEL), jnp.float32),
        compiler_params=_cparams("arbitrary", "arbitrary"),
        name="adaln",
    )(cvec8, w_ada, b_ada.reshape(DEPTH, 1, 6 * D_MODEL))


def _s5prep_kernel(ar_ref, ai_ref, ldt_ref, br_ref, bi_ref, abar_re, abar_im, bbar_re, bbar_im):
    ar, ai = ar_ref[0], ai_ref[0]
    dt = jnp.exp(ldt_ref[0])
    mag = jnp.exp(ar * dt)
    are = mag * jnp.cos(ai * dt)
    aim = mag * jnp.sin(ai * dt)
    zr, zi = are - 1.0, aim
    den = ar * ar + ai * ai
    cre = (zr * ar + zi * ai) / den
    cim = (zi * ar - zr * ai) / den
    abar_re[0] = are
    abar_im[0] = aim
    br, bi = br_ref[0], bi_ref[0]
    bbar_re[0] = cre * br - cim * bi
    bbar_im[0] = cre * bi + cim * br


def _s5_prep(a_re, a_im, log_dt, b_re, b_im):
    n = DEPTH * 2
    ar = a_re.reshape(n, 1, S5_HALF)
    ai = a_im.reshape(n, 1, S5_HALF)
    ldt = jnp.broadcast_to(log_dt[..., None], (DEPTH, 2, SSM_GROUPS, SSM_STATE)).reshape(n, 1, S5_HALF)
    br = b_re.transpose(0, 1, 4, 2, 3).reshape(n, SSM_GROUP_CH, S5_HALF)
    bi = b_im.transpose(0, 1, 4, 2, 3).reshape(n, SSM_GROUP_CH, S5_HALF)
    vec = jax.ShapeDtypeStruct((n, 1, S5_HALF), jnp.float32)
    mat = jax.ShapeDtypeStruct((n, SSM_GROUP_CH, S5_HALF), jnp.float32)
    vspec = pl.BlockSpec((1, 1, S5_HALF), lambda j: (j, 0, 0))
    mspec = pl.BlockSpec((1, SSM_GROUP_CH, S5_HALF), lambda j: (j, 0, 0))
    return pl.pallas_call(
        _s5prep_kernel,
        grid=(n,),
        in_specs=[vspec, vspec, vspec, mspec, mspec],
        out_specs=[vspec, vspec, mspec, mspec],
        out_shape=(vec, vec, mat, mat),
        compiler_params=_cparams("arbitrary"),
        name="s5prep",
    )(ar, ai, ldt, br, bi)


def _head_rms(blk, gsum, gain):
    sq = blk * blk
    hi = sq.astype(jnp.bfloat16)
    lo = (sq - hi.astype(jnp.float32)).astype(jnp.bfloat16)
    ms = (_bdot(hi, gsum) + _bdot(lo, gsum)) * (1.0 / HEAD_DIM)
    return blk * lax.rsqrt(ms + EPS) * gain


def _rope(blk, cos, sin_signed, first_half):
    partner = jnp.where(first_half, pltpu.roll(blk, LANES - ROPE_F, 1), pltpu.roll(blk, ROPE_F, 1))
    return blk * cos + partner * sin_signed


def _inproj_kernel(has_moe_residual, *refs):
    if has_moe_residual:
        (x_ref, y_ref, pmod_ref, mod_ref, g_ref, w_ref, qg_ref, kg_ref, gsum_ref, cos_ref, sin_ref,
         u_ref, qkv_ref, kv32_ref, xo_ref) = refs
        x = x_ref[...] + pmod_ref[0, :, 5 * D_MODEL:6 * D_MODEL] * y_ref[...]
        xo_ref[...] = x
    else:
        (x_ref, mod_ref, g_ref, w_ref, qg_ref, kg_ref, gsum_ref, cos_ref, sin_ref,
         u_ref, qkv_ref, kv32_ref) = refs
        x = x_ref[...]
    i = pl.program_id(0)
    shift = mod_ref[0, :, 0:D_MODEL]
    scale = mod_ref[0, :, D_MODEL:2 * D_MODEL]
    ms = jnp.mean(x * x, axis=-1, keepdims=True)
    h = (x * lax.rsqrt(ms + EPS) * g_ref[...]) * (1.0 + scale) + shift
    proj = _bdot(h.astype(jnp.bfloat16), w_ref[...])
    u_ref[...] = proj[:, 0:SSM_WIDTH]

    c0 = SSM_WIDTH
    qn = proj[:, c0:c0 + NA_WIDTH] * ATTN_SCALE
    kn = proj[:, c0 + NA_WIDTH:c0 + 2 * NA_WIDTH]
    vn = proj[:, c0 + 2 * NA_WIDTH:c0 + 3 * NA_WIDTH]
    c1 = c0 + 3 * NA_WIDTH
    vg = proj[:, c1 + GQA_WIDTH + KV_WIDTH:c1 + GQA_WIDTH + 2 * KV_WIDTH]

    gsum = gsum_ref[...]
    cos, sin = cos_ref[...], sin_ref[...]
    lane = lax.broadcasted_iota(jnp.int32, (1, LANES), 1)
    first_half = (lane & (2 * ROPE_F - 1)) < ROPE_F
    qg_blocks = []
    for p in range(GQA_WIDTH // PAIR):
        blk = proj[:, c1 + p * PAIR:c1 + (p + 1) * PAIR]
        blk = _rope(_head_rms(blk, gsum, qg_ref[...]), cos, sin, first_half)
        qg_blocks.append(blk * ATTN_SCALE)
    kg = proj[:, c1 + GQA_WIDTH:c1 + GQA_WIDTH + KV_WIDTH]
    kg = _rope(_head_rms(kg, gsum, kg_ref[...]), cos, sin, first_half)

    o = 0
    for piece in [qn, kn, vn] + qg_blocks + [kg, vg]:
        w = piece.shape[1]
        qkv_ref[:, o:o + w] = piece.astype(jnp.bfloat16)
        o += w

    @pl.when(i < T_CTX // TM)
    def _():
        kv32_ref[:, 0:NA_WIDTH] = kn
        kv32_ref[:, NA_WIDTH:2 * NA_WIDTH] = vn
        kv32_ref[:, 2 * NA_WIDTH:2 * NA_WIDTH + KV_WIDTH] = kg
        kv32_ref[:, 2 * NA_WIDTH + KV_WIDTH:2 * NA_WIDTH + 2 * KV_WIDTH] = vg


QKV_WIDTH = 3 * NA_WIDTH + GQA_WIDTH + 2 * KV_WIDTH
KV32_WIDTH = 2 * NA_WIDTH + 2 * KV_WIDTH
QN_BLK, KN_BLK, VN_BLK, QG_BLK, KG_BLK, VG_BLK = 0, 3, 6, 9, 12, 13


def _rope_block(i):
    n_ctx = T_CTX // TM
    return jnp.where(i < n_ctx, 0, 1 + (i - n_ctx) % (DEC_SEQ // TM))


def _in_proj(x, moe_residual, mods3, norm_g, w_in_bf, q_gain, k_gain, gsum, rope_cos, rope_sin):
    n_ctx = T_CTX // TM
    row_spec = pl.BlockSpec((TM, D_MODEL), lambda i: (i, 0))
    mod_spec = pl.BlockSpec((1, 1, 6 * D_MODEL), lambda i: (_modset(i), 0, 0))
    args, in_specs = [x], [row_spec]
    out_specs = [pl.BlockSpec((TM, SSM_WIDTH), lambda i: (i, 0)),
                 pl.BlockSpec((TM, QKV_WIDTH), lambda i: (i, 0)),
                 pl.BlockSpec((TM, KV32_WIDTH), lambda i: (jnp.minimum(i, n_ctx - 1), 0))]
    out_shape = [jax.ShapeDtypeStruct((T_ALL, SSM_WIDTH), jnp.float32),
                 jax.ShapeDtypeStruct((T_ALL, QKV_WIDTH), jnp.bfloat16),
                 jax.ShapeDtypeStruct((T_CTX, KV32_WIDTH), jnp.float32)]
    if moe_residual is not None:
        args += list(moe_residual)
        in_specs += [row_spec, mod_spec]
        out_specs.append(row_spec)
        out_shape.append(jax.ShapeDtypeStruct((T_ALL, D_MODEL), jnp.float32))
    args += [mods3, norm_g, w_in_bf, q_gain, k_gain, gsum, rope_cos, rope_sin]
    in_specs += [mod_spec,
                 pl.BlockSpec((1, D_MODEL), lambda i: (0, 0)),
                 pl.BlockSpec((D_MODEL, IN_WIDTH), lambda i: (0, 0)),
                 pl.BlockSpec((1, LANES), lambda i: (0, 0)),
                 pl.BlockSpec((1, LANES), lambda i: (0, 0)),
                 pl.BlockSpec((LANES, LANES), lambda i: (0, 0)),
                 pl.BlockSpec((TM, LANES), lambda i: (_rope_block(i), 0)),
                 pl.BlockSpec((TM, LANES), lambda i: (_rope_block(i), 0))]
    outs = pl.pallas_call(
        functools.partial(_inproj_kernel, moe_residual is not None),
        grid=(T_ALL // TM,),
        in_specs=in_specs,
        out_specs=out_specs,
        out_shape=out_shape,
        compiler_params=_cparams("arbitrary"),
        name="in_proj",
    )(*args)
    return outs if moe_residual is not None else (*outs, x)


def _s5scan_kernel(u_ref, b_ref, a_ref, c_ref, y_ref, p_ref, fin_ref, hbuf):
    d = pl.program_id(0)
    k = pl.program_id(1)
    rows = S5_TB * S5_ROWS
    a_re = jnp.broadcast_to(a_ref[0, :, 0:S5_HALF], (SUBLANES, S5_HALF))
    a_im = jnp.broadcast_to(a_ref[0, :, S5_HALF:S5_CPLX], (SUBLANES, S5_HALF))
    ub = u_ref[...].astype(jnp.bfloat16)

    def run(forward):
        data0 = S5_ROWS if forward else 0
        prev0 = 0 if forward else rows
        last0 = rows if forward else 0

        @pl.when(k == 0)
        def _():
            r = lax.broadcasted_iota(jnp.int32, (S5_ROWS, S5_CPLX), 0)
            l = lax.broadcasted_iota(jnp.int32, (S5_ROWS, S5_CPLX), 1)
            hbuf[prev0:prev0 + S5_ROWS, :] = jnp.where((r == S5_POW_ROW) & (l < S5_HALF), 1.0, 0.0)

        @pl.when(k > 0)
        def _():
            hbuf[prev0:prev0 + S5_ROWS, :] = hbuf[last0:last0 + S5_ROWS, :]

        hbuf[data0:data0 + rows, :] = _bdot(ub, b_ref[0])

        def step(i, carry):
            t = i if forward else S5_TB - 1 - i
            cur = pl.multiple_of(data0 + t * S5_ROWS, SUBLANES)
            prv = pl.multiple_of(cur - S5_ROWS if forward else cur + S5_ROWS, SUBLANES)
            for st in range(S5_ROWS // SUBLANES):
                rc = pl.ds(cur + st * SUBLANES, SUBLANES)
                rp = pl.ds(prv + st * SUBLANES, SUBLANES)
                hr = hbuf[rp, 0:S5_HALF]
                hi = hbuf[rp, S5_HALF:S5_CPLX]
                nr = a_re * hr - a_im * hi + hbuf[rc, 0:S5_HALF]
                ni = a_re * hi + a_im * hr + hbuf[rc, S5_HALF:S5_CPLX]
                hbuf[rc, 0:S5_HALF] = nr
                hbuf[rc, S5_HALF:S5_CPLX] = ni
            return carry

        lax.fori_loop(0, S5_TB, step, 0, unroll=True)
        y_ref[0] = _bdot(hbuf[data0:data0 + rows, :].astype(jnp.bfloat16), c_ref[0])
        for t in range(S5_TB):
            r0 = data0 + t * S5_ROWS + S5_POW_ROW
            p_ref[0, t:t + 1, :] = hbuf[r0:r0 + 1, :]

        @pl.when(k == S5_NBLK - 1)
        def _():
            fin_ref[0] = hbuf[last0:last0 + S5_ROWS, :]

    @pl.when(d == 0)
    def _():
        run(True)

    @pl.when(d == 1)
    def _():
        run(False)


def _s5_scan(u_tm, bblk, abar, cblk):
    rows = S5_TB * S5_ROWS

    def tblk(d, k):
        return jnp.where(d == 0, k, S5_NBLK - 1 - k)

    return pl.pallas_call(
        _s5scan_kernel,
        grid=(2, S5_NBLK),
        in_specs=[pl.BlockSpec((rows, SSM_WIDTH), lambda d, k: (tblk(d, k), 0)),
                  pl.BlockSpec((1, SSM_WIDTH, S5_CPLX), lambda d, k: (d, 0, 0)),
                  pl.BlockSpec((1, 1, S5_CPLX), lambda d, k: (d, 0, 0)),
                  pl.BlockSpec((1, S5_CPLX, SSM_WIDTH), lambda d, k: (d, 0, 0))],
        out_specs=[pl.BlockSpec((1, rows, SSM_WIDTH), lambda d, k: (d, tblk(d, k), 0)),
                   pl.BlockSpec((1, S5_TB, S5_CPLX), lambda d, k: (d, tblk(d, k), 0)),
                   pl.BlockSpec((1, S5_ROWS, S5_CPLX), lambda d, k: (d, 0, 0))],
        out_shape=(jax.ShapeDtypeStruct((2, S5_CHUNK * S5_ROWS, SSM_WIDTH), jnp.float32),
                   jax.ShapeDtypeStruct((2, S5_CHUNK, S5_CPLX), jnp.float32),
                   jax.ShapeDtypeStruct((2, S5_ROWS, S5_CPLX), jnp.float32)),
        scratch_shapes=[pltpu.VMEM((rows + S5_ROWS, S5_CPLX), jnp.float32)],
        compiler_params=_cparams("arbitrary", "arbitrary"),
        name="s5_scan",
    )(u_tm, bblk, abar, cblk)


def _cmul(pr, pi, cr, ci):
    return pr * cr - pi * ci, pr * ci + pi * cr


def _s5fin_kernel(u_ref, yf_ref, yb_ref, p_ref, fin_ref, h0_ref, c_ref, d_ref, wg_ref, bg_ref,
                  o_ref, carry, hc, ybuf):
    k = pl.program_id(0)
    nchunk = DEC_SEQ // S5_CHUNK

    @pl.when(k == 0)
    def _():
        for d in range(2):
            t_full = S5_CHUNK - 1 if d == 0 else 0
            pr = p_ref[d, t_full:t_full + 1, 0:S5_HALF]
            pi = p_ref[d, t_full:t_full + 1, S5_HALF:S5_CPLX]
            for b in range(DEC_BATCH):
                cr = h0_ref[d, b:b + 1, 0:S5_HALF]
                ci = h0_ref[d, b:b + 1, S5_HALF:S5_CPLX]
                order = range(nchunk) if d == 0 else range(nchunk - 1, -1, -1)
                for c in order:
                    row = b * nchunk + c
                    carry[d, row:row + 1, 0:S5_HALF] = cr
                    carry[d, row:row + 1, S5_HALF:S5_CPLX] = ci
                    fr = fin_ref[d, S5_LAT0 + row:S5_LAT0 + row + 1, 0:S5_HALF]
                    fi = fin_ref[d, S5_LAT0 + row:S5_LAT0 + row + 1, S5_HALF:S5_CPLX]
                    mr, mi = _cmul(pr, pi, cr, ci)
                    cr, ci = fr + mr, fi + mi

    ybuf[...] = yf_ref[0] + yb_ref[0] + u_ref[...] * d_ref[...]
    for d in range(2):
        cr = carry[d, :, 0:S5_HALF]
        ci = carry[d, :, S5_HALF:S5_CPLX]
        for t in range(S5_TB):
            pr = p_ref[d, pl.ds(k * S5_TB + t, 1), 0:S5_HALF]
            pi = p_ref[d, pl.ds(k * S5_TB + t, 1), S5_HALF:S5_CPLX]
            mr, mi = _cmul(pr, pi, cr, ci)
            hc[t * S5_NLAT:(t + 1) * S5_NLAT, 0:S5_HALF] = mr.astype(jnp.bfloat16)
            hc[t * S5_NLAT:(t + 1) * S5_NLAT, S5_HALF:S5_CPLX] = mi.astype(jnp.bfloat16)
        corr = _bdot(hc[...], c_ref[d])
        ybuf[:, S5_LAT0:S5_LAT0 + S5_NLAT, :] += corr.reshape(S5_TB, S5_NLAT, SSM_WIDTH)

    y = ybuf[...].reshape(S5_TB * S5_ROWS, SSM_WIDTH)
    yg = 0.5 * y * (1.0 + jnp.tanh(math.sqrt(2.0 / math.pi) * (y + 0.044715 * (y * y * y))))
    z = _bdot(yg.astype(jnp.bfloat16), wg_ref[...]) + bg_ref[...]
    out = yg / (1.0 + jnp.exp(-z))
    o_ref[...] = out.reshape(S5_TB, S5_ROWS, SSM_WIDTH).astype(jnp.bfloat16)


def _s5_finish(u_tm3, y_dir4, powers, finals, h0, cblk, ssm_d, w_glu_bf, b_glu):
    blk3 = (S5_TB, S5_ROWS, SSM_WIDTH)
    return pl.pallas_call(
        _s5fin_kernel,
        grid=(S5_NBLK,),
        in_specs=[pl.BlockSpec(blk3, lambda k: (k, 0, 0)),
                  pl.BlockSpec((1,) + blk3, lambda k: (0, k, 0, 0)),
                  pl.BlockSpec((1,) + blk3, lambda k: (1, k, 0, 0)),
                  pl.BlockSpec((2, S5_CHUNK, S5_CPLX), lambda k: (0, 0, 0)),
                  pl.BlockSpec((2, S5_ROWS, S5_CPLX), lambda k: (0, 0, 0)),
                  pl.BlockSpec((2, DEC_BATCH, S5_CPLX), lambda k: (0, 0, 0)),
                  pl.BlockSpec((2, S5_CPLX, SSM_WIDTH), lambda k: (0, 0, 0)),
                  pl.BlockSpec((1, SSM_WIDTH), lambda k: (0, 0)),
                  pl.BlockSpec((SSM_WIDTH, SSM_WIDTH), lambda k: (0, 0)),
                  pl.BlockSpec((1, SSM_WIDTH), lambda k: (0, 0))],
        out_specs=pl.BlockSpec(blk3, lambda k: (k, 0, 0)),
        out_shape=jax.ShapeDtypeStruct((S5_CHUNK, S5_ROWS, SSM_WIDTH), jnp.bfloat16),
        scratch_shapes=[pltpu.VMEM((2, S5_NLAT, S5_CPLX), jnp.float32),
                        pltpu.VMEM((S5_TB * S5_NLAT, S5_CPLX), jnp.bfloat16),
                        pltpu.VMEM(blk3, jnp.float32)],
        compiler_params=_cparams("arbitrary"),
        name="s5_finish",
    )(u_tm3, y_dir4, y_dir4, powers, finals, h0, cblk, ssm_d, w_glu_bf, b_glu)


def _pair_attention(q, kvs, biases):
    lane = lax.broadcasted_iota(jnp.int32, (1, PAIR), 1)
    outs = []
    for a in range(2):
        sel = (lane < HEAD_DIM) if a == 0 else (lane >= HEAD_DIM)
        qa = jnp.where(sel, q, jnp.zeros_like(q))
        ss = []
        for (kk, _), bias in zip(kvs, biases):
            s = _bdot_t(qa, kk)
            if bias is not None:
                s = s + bias[a]
            ss.append(s)
        m = ss[0].max(axis=-1, keepdims=True)
        for s in ss[1:]:
            m = jnp.maximum(m, s.max(axis=-1, keepdims=True))
        acc = None
        for s, (_, vv) in zip(ss, kvs):
            p = jnp.exp(s - m).astype(jnp.bfloat16)
            o = _bdot(p, jnp.where(sel, vv, jnp.ones_like(vv)))
            acc = o if acc is None else acc + o
        den = pltpu.roll(acc, HEAD_DIM, 1)
        outs.append(acc / den)
    return jnp.where(lane < HEAD_DIM, outs[0], outs[1])


ATTN_WIDTH = NA_WIDTH + GQA_WIDTH
NPAIR = NA_WIDTH // PAIR


def _attn_ctx_kernel(qkv_ref, o_ref):
    def blk(j):
        return qkv_ref[:, j * PAIR:(j + 1) * PAIR]

    for p in range(NPAIR):
        o = _pair_attention(blk(QN_BLK + p), [(blk(KN_BLK + p), blk(VN_BLK + p))], [None])
        o_ref[:, p * PAIR:(p + 1) * PAIR] = o.astype(jnp.bfloat16)
    for p in range(NPAIR):
        o = _pair_attention(blk(QG_BLK + p), [(blk(KG_BLK), blk(VG_BLK))], [None])
        o_ref[:, (NPAIR + p) * PAIR:(NPAIR + p + 1) * PAIR] = o.astype(jnp.bfloat16)


def _attn_ctx(qkv):
    return pl.pallas_call(
        _attn_ctx_kernel,
        grid=(BATCH,),
        in_specs=[pl.BlockSpec((SEQ, QKV_WIDTH), lambda b: (b, 0))],
        out_specs=pl.BlockSpec((SEQ, ATTN_WIDTH), lambda b: (b, 0)),
        out_shape=jax.ShapeDtypeStruct((T_CTX, ATTN_WIDTH), jnp.bfloat16),
        compiler_params=_cparams("arbitrary"),
        name="attn_ctx",
    )(qkv)


TQ = 256


def _gqa_lat_kernel(q_ref, k_ref, v_ref, ck_ref, cv_ref, o_ref):
    kvs = [(ck_ref[0], cv_ref[0]), (k_ref[...], v_ref[...])]
    o_ref[...] = _pair_attention(q_ref[...], kvs, [None, None]).astype(jnp.bfloat16)


def _gqa_lat(qkv, ck, cv):
    nq = DEC_SEQ // TQ
    ctx_blocks = T_CTX // DEC_SEQ
    return pl.pallas_call(
        _gqa_lat_kernel,
        grid=(DEC_BATCH, NPAIR, nq),
        in_specs=[pl.BlockSpec((TQ, PAIR), lambda b, p, i: (T_CTX // TQ + b * nq + i, QG_BLK + p)),
                  pl.BlockSpec((DEC_SEQ, PAIR), lambda b, p, i: (ctx_blocks + b, KG_BLK)),
                  pl.BlockSpec((DEC_SEQ, PAIR), lambda b, p, i: (ctx_blocks + b, VG_BLK)),
                  pl.BlockSpec((1, PAST_LEN, PAIR), lambda b, p, i: (b, 0, 0)),
                  pl.BlockSpec((1, PAST_LEN, PAIR), lambda b, p, i: (b, 0, 0))],
        out_specs=pl.BlockSpec((TQ, PAIR), lambda b, p, i: (b * nq + i, p)),
        out_shape=jax.ShapeDtypeStruct((T_LAT, GQA_WIDTH), jnp.bfloat16),
        compiler_params=_cparams("arbitrary", "arbitrary", "arbitrary"),
        name="gqa_lat",
    )(qkv, qkv, qkv, ck, cv)


NA_BLK = 4
NA_NBLK = GRID_H // NA_BLK
NA_SLAB_ROWS = NA_KH + NA_BLK - 1
NA_SLAB = NA_SLAB_ROWS * GRID_W
NA_QROWS = NA_BLK * GRID_W
NA_BLOCK_TYPES = (0, 2, NA_NBLK - 1)


def _na_slab_start(j):
    return jnp.clip(j * NA_BLK - NA_KH // 2, 0, GRID_H - NA_SLAB_ROWS)


def _na_lat_kernel(q_ref, k_ref, v_ref, ck_ref, cv_ref, bias_ref, o_ref):
    ck, cv = ck_ref[0, 0], cv_ref[0, 0]
    lane = lax.broadcasted_iota(jnp.int32, (1, PAIR), 1)

    def block(j, carry):
        q0 = pl.multiple_of(j * NA_QROWS, NA_QROWS)
        k0 = pl.multiple_of(_na_slab_start(j) * GRID_W, GRID_W)
        btype = jnp.where(j == 0, 0, jnp.where(j == NA_NBLK - 1, 2, 1))
        q = q_ref[pl.ds(q0, NA_QROWS), :]
        lhs = jnp.concatenate([jnp.where(lane < HEAD_DIM, q, jnp.zeros_like(q)),
                               jnp.where(lane >= HEAD_DIM, q, jnp.zeros_like(q))], axis=0)
        ks = k_ref[pl.ds(k0, NA_SLAB), :]
        vs = v_ref[pl.ds(k0, NA_SLAB), :]
        s_lat = _bdot_t(lhs, ks) + bias_ref[0, 0, btype]
        s_ctx = _bdot_t(lhs, ck)
        m = jnp.maximum(s_lat.max(axis=-1, keepdims=True), s_ctx.max(axis=-1, keepdims=True))
        p_lat = jnp.exp(s_lat - m)
        p_ctx = jnp.exp(s_ctx - m)
        den = p_lat.sum(axis=-1, keepdims=True) + p_ctx.sum(axis=-1, keepdims=True)
        o = (_bdot(p_lat.astype(jnp.bfloat16), vs) + _bdot(p_ctx.astype(jnp.bfloat16), cv)) / den
        out = jnp.where(lane < HEAD_DIM, o[0:NA_QROWS], o[NA_QROWS:2 * NA_QROWS])
        o_ref[pl.ds(q0, NA_QROWS), :] = out.astype(jnp.bfloat16)
        return carry

    lax.fori_loop(0, NA_NBLK, block, 0)


def _na_lat(layer, qkv, ck, cv, bias):
    ctx_blocks = T_CTX // DEC_SEQ
    return pl.pallas_call(
        _na_lat_kernel,
        grid=(DEC_BATCH, NPAIR),
        in_specs=[pl.BlockSpec((DEC_SEQ, PAIR), lambda b, p: (ctx_blocks + b, QN_BLK + p)),
                  pl.BlockSpec((DEC_SEQ, PAIR), lambda b, p: (ctx_blocks + b, KN_BLK + p)),
                  pl.BlockSpec((DEC_SEQ, PAIR), lambda b, p: (ctx_blocks + b, VN_BLK + p)),
                  pl.BlockSpec((1, 1, PAST_LEN, PAIR), lambda b, p: (b, p, 0, 0)),
                  pl.BlockSpec((1, 1, PAST_LEN, PAIR), lambda b, p: (b, p, 0, 0)),
                  pl.BlockSpec((1, 1, len(NA_BLOCK_TYPES), 2 * NA_QROWS, NA_SLAB),
                               lambda b, p: (layer, p, 0, 0, 0))],
        out_specs=pl.BlockSpec((DEC_SEQ, PAIR), lambda b, p: (b, p)),
        out_shape=jax.ShapeDtypeStruct((T_LAT, NA_WIDTH), jnp.bfloat16),
        compiler_params=_cparams("arbitrary", "arbitrary"),
        name="na_lat",
    )(qkv, qkv, qkv, ck, cv, bias)


def _na_bias_table(rpb):
    nt = len(NA_BLOCK_TYPES)
    nl = rpb.shape[0]
    j = np.array(NA_BLOCK_TYPES)[:, None]
    r = j * NA_BLK + np.arange(NA_BLK)[None, :]
    slab0 = np.clip(j * NA_BLK - NA_KH // 2, 0, GRID_H - NA_SLAB_ROWS)
    win0 = np.clip(r - NA_KH // 2, 0, GRID_H - NA_KH)
    key_row = slab0[:, :, None] + np.arange(NA_SLAB_ROWS)[None, None, :]
    row_in = (key_row >= win0[:, :, None]) & (key_row < win0[:, :, None] + NA_KH)
    dr = np.clip(key_row - r[:, :, None] + (NA_KH - 1), 0, 2 * NA_KH - 2)
    cq = np.arange(GRID_W)
    ck = np.arange(GRID_W)
    cs = np.clip(cq - NA_KW // 2, 0, GRID_W - NA_KW)
    col_in = (ck[None, :] >= cs[:, None]) & (ck[None, :] < cs[:, None] + NA_KW)
    dc = np.clip(ck[None, :] - cq[:, None] + NA_KW - 1, 0, 2 * NA_KW - 2)
    sel_c = jnp.asarray(np.eye(2 * NA_KW - 1, dtype=np.float32)[dc])
    cols = jnp.einsum("lhdc,qkc->lhdqk", rpb.astype(jnp.float32), sel_c, precision=lax.Precision.HIGHEST)
    cols = jnp.where(jnp.asarray(col_in), cols, NEG_INF)
    cols = cols.reshape(nl, NA_HEADS // 2, 2, 2 * NA_KH - 1, GRID_W, GRID_W)
    masked = jnp.full((nl, NA_HEADS // 2, 2, GRID_W, GRID_W), NEG_INF, jnp.float32)
    per_type = []
    for t in range(nt):
        per_row = []
        for n in range(NA_BLK):
            pieces = [cols[:, :, :, int(dr[t, n, i])] if row_in[t, n, i] else masked
                      for i in range(NA_SLAB_ROWS)]
            per_row.append(jnp.stack(pieces, axis=4))
        per_type.append(jnp.stack(per_row, axis=3))
    b = jnp.stack(per_type, axis=2)
    return b.reshape(nl, NA_HEADS // 2, nt, 2 * NA_QROWS, NA_SLAB)


GATE_LANE0 = N_GROUPS
GROUP_LANE = 0
_EPG_SHIFT = EXPERTS_PER_GROUP.bit_length() - 1
assert 1 << _EPG_SHIFT == EXPERTS_PER_GROUP


def _route(logits):
    lane_i = lax.broadcasted_iota(jnp.int32, logits.shape, 1)
    lane = lane_i.astype(jnp.float32)
    big = float(LANES)
    is_g = lane_i < N_GROUPS
    lg = jnp.where(is_g, logits, NEG_INF)
    mg = lg.max(axis=-1, keepdims=True)
    pg_top = 1.0 / jnp.where(is_g, jnp.exp(lg - mg), 0.0).sum(axis=-1, keepdims=True)
    g_idx = jnp.where(is_g & (lg == mg), lane, big).min(axis=-1, keepdims=True)
    in_e = (lane_i >= GATE_LANE0) & (lane_i < GATE_LANE0 + N_EXPERTS)
    lane_group = jnp.where(in_e, ((lane_i - GATE_LANE0) >> _EPG_SHIFT).astype(jnp.float32), -1.0)
    sel = lane_group == g_idx
    le = jnp.where(sel, logits, NEG_INF)
    me = le.max(axis=-1, keepdims=True)
    ex = jnp.where(sel, jnp.exp(le - me), 0.0)
    pe = ex / ex.sum(axis=-1, keepdims=True)
    p1 = pe.max(axis=-1, keepdims=True)
    i1 = jnp.where(sel & (pe == p1), lane, big).min(axis=-1, keepdims=True)
    rest = sel & (lane != i1)
    pr = jnp.where(rest, pe, -1.0)
    p2 = pr.max(axis=-1, keepdims=True)
    i2 = jnp.where(rest & (pr == p2), lane, big).min(axis=-1, keepdims=True)
    tot = p1 + p2
    within = jnp.where(lane == i1, p1 / tot, 0.0) + jnp.where(lane == i2, p2 / tot, 0.0)
    return pg_top * within + jnp.where(lane_i == GROUP_LANE, g_idx, 0.0)


HG_WIDTH = D_MODEL + LANES


def _outproj_kernel(x_ref, ys_ref, yc_ref, yn_ref, yg_ref, mod_ref, g_ref, w_ref, wr_hi_ref, wr_lo_ref, br_ref,
                    x1_ref, hg_ref):
    is_ctx = pl.program_id(0) < T_CTX // TM
    y_na = jnp.where(is_ctx, yc_ref[:, 0:NA_WIDTH], yn_ref[...])
    y_g = jnp.where(is_ctx, yc_ref[:, NA_WIDTH:ATTN_WIDTH], yg_ref[...])
    mix = (_bdot(ys_ref[...], w_ref[0:SSM_WIDTH, :])
           + _bdot(y_na, w_ref[SSM_WIDTH:SSM_WIDTH + NA_WIDTH, :])
           + _bdot(y_g, w_ref[SSM_WIDTH + NA_WIDTH:, :]))
    gate1 = mod_ref[0, :, 2 * D_MODEL:3 * D_MODEL]
    shift2 = mod_ref[0, :, 3 * D_MODEL:4 * D_MODEL]
    scale2 = mod_ref[0, :, 4 * D_MODEL:5 * D_MODEL]
    x1 = x_ref[...] + gate1 * mix
    x1_ref[...] = x1
    ms = jnp.mean(x1 * x1, axis=-1, keepdims=True)
    h2 = (x1 * lax.rsqrt(ms + EPS) * g_ref[...]) * (1.0 + scale2) + shift2
    hg_ref[:, 0:D_MODEL] = h2
    h_hi = h2.astype(jnp.bfloat16)
    h_lo = (h2 - h_hi.astype(jnp.float32)).astype(jnp.bfloat16)
    logits = (_bdot(h_hi, wr_hi_ref[...]) + _bdot(h_lo, wr_hi_ref[...]) + _bdot(h_hi, wr_lo_ref[...])
              + br_ref[...])
    hg_ref[:, D_MODEL:HG_WIDTH] = _route(logits)


def _out_proj(x, y_ssm, y_ctx, y_na_lat, y_g_lat, mods3, norm_g, w_out_bf, wr_hi, wr_lo, b_router):
    n_ctx = T_CTX // TM
    return pl.pallas_call(
        _outproj_kernel,
        grid=(T_ALL // TM,),
        in_specs=[pl.BlockSpec((TM, D_MODEL), lambda i: (i, 0)),
                  pl.BlockSpec((TM, SSM_WIDTH), lambda i: (i, 0)),
                  pl.BlockSpec((TM, ATTN_WIDTH), lambda i: (jnp.minimum(i, n_ctx - 1), 0)),
                  pl.BlockSpec((TM, NA_WIDTH), lambda i: (jnp.maximum(i - n_ctx, 0), 0)),
                  pl.BlockSpec((TM, GQA_WIDTH), lambda i: (jnp.maximum(i - n_ctx, 0), 0)),
                  pl.BlockSpec((1, 1, 6 * D_MODEL), lambda i: (_modset(i), 0, 0)),
                  pl.BlockSpec((1, D_MODEL), lambda i: (0, 0)),
                  pl.BlockSpec((D_MODEL, D_MODEL), lambda i: (0, 0)),
                  pl.BlockSpec((D_MODEL, LANES), lambda i: (0, 0)),
                  pl.BlockSpec((D_MODEL, LANES), lambda i: (0, 0)),
                  pl.BlockSpec((1, LANES), lambda i: (0, 0))],
        out_specs=[pl.BlockSpec((TM, D_MODEL), lambda i: (i, 0)),
                   pl.BlockSpec((TM, HG_WIDTH), lambda i: (i, 0))],
        out_shape=(jax.ShapeDtypeStruct((T_ALL, D_MODEL), jnp.float32),
                   jax.ShapeDtypeStruct((T_ALL, HG_WIDTH), jnp.float32)),
        compiler_params=_cparams("arbitrary"),
        name="out_proj",
    )(x, y_ssm, y_ctx, y_na_lat, y_g_lat, mods3, norm_g, w_out_bf, wr_hi, wr_lo, b_router)


TMOE = 256
NT_MOE = T_ALL // TMOE + N_GROUPS
MOE_SLOTS = NT_MOE * TMOE


def _moe_dispatch(group_f):
    i32 = jnp.int32
    g = group_f.astype(i32)
    onehot = (g[:, None] == jnp.arange(N_GROUPS, dtype=i32)[None, :]).astype(i32)
    csum = jnp.cumsum(onehot, axis=0)
    counts = csum[-1]
    rank = jnp.sum(onehot * csum, axis=1) - 1
    tiles = (counts + TMOE - 1) // TMOE
    tile_end = jnp.cumsum(tiles)
    tile_start = tile_end - tiles
    slot = jnp.sum(onehot * tile_start[None, :], axis=1) * TMOE + rank
    tok = jnp.full((MOE_SLOTS,), -1, i32).at[slot].set(jnp.arange(T_ALL, dtype=i32))
    pad = tok < 0
    pad_rank = jnp.cumsum(pad.astype(i32)) - 1
    row_of_slot = jnp.where(pad, T_ALL + pad_rank, tok)
    tile_group = jnp.minimum(
        jnp.sum(jnp.arange(NT_MOE, dtype=i32)[:, None] >= tile_end[None, :], axis=1), N_GROUPS - 1).astype(i32)
    return tile_group, row_of_slot


def _moe_kernel(tg_ref, row_ref, hg_hbm, wg_ref, wu_ref, wd_ref, y_hbm,
                gbuf, ybuf, wg_bf, wu_bf, wd_bf, gsem, ssem):
    i = pl.program_id(0)
    slot = lax.rem(i, 2)
    other = 1 - slot
    last = NT_MOE - 1
    nxt = jnp.minimum(i + 1, last)
    prv = jnp.maximum(i - 1, 0)

    def gather_row(row, buf_slot, r):
        return pltpu.make_async_copy(hg_hbm.at[pl.ds(row, 1)], gbuf.at[buf_slot, pl.ds(r, 1)], gsem.at[buf_slot])

    def scatter_row(row, buf_slot, r):
        return pltpu.make_async_copy(ybuf.at[buf_slot, pl.ds(r, 1)], y_hbm.at[pl.ds(row, 1)], ssem.at[buf_slot])

    def wait_gather(buf_slot):
        pltpu.make_async_copy(hg_hbm.at[pl.ds(0, TMOE)], gbuf.at[buf_slot], gsem.at[buf_slot]).wait()

    def wait_scatter(buf_slot):
        pltpu.make_async_copy(ybuf.at[buf_slot], y_hbm.at[pl.ds(0, TMOE)], ssem.at[buf_slot]).wait()

    def start_gather_row(tile, buf_slot, r):
        row = row_ref[tile * TMOE + r]
        gather_row(jnp.where(row < T_ALL, row, 0), buf_slot, r).start(priority=r % 2)

    @pl.when(i == 0)
    def _():
        ybuf[1] = jnp.zeros((TMOE, D_MODEL), jnp.float32)
        for r in range(TMOE):
            start_gather_row(0, 0, r)

    first_of_group = (i == 0) | (tg_ref[i] != tg_ref[prv])

    @pl.when(first_of_group)
    def _():
        wg_bf[...] = wg_ref[0].astype(jnp.bfloat16)
        wu_bf[...] = wu_ref[0].astype(jnp.bfloat16)
        wd_bf[...] = wd_ref[0].astype(jnp.bfloat16)

    wait_gather(slot)

    h = gbuf[slot, :, 0:D_MODEL].astype(jnp.bfloat16)
    routing = gbuf[slot, :, D_MODEL:HG_WIDTH]
    lane = lax.broadcasted_iota(jnp.int32, (1, LANES), 1)
    lane0 = GATE_LANE0 + tg_ref[i] * EXPERTS_PER_GROUP
    rows_per_chunk = TMOE // EXPERTS_PER_GROUP
    y = None
    for j in range(EXPERTS_PER_GROUP):
        for r in range(j * rows_per_chunk, (j + 1) * rows_per_chunk):
            start_gather_row(nxt, other, r)
            dst = jnp.where(i > 0, row_ref[prv * TMOE + r], MOE_SLOTS + r)
            scatter_row(dst, other, r).start(priority=(r + 1) % 2)
        gate = jnp.where(lane == lane0 + j, routing, 0.0).sum(axis=-1, keepdims=True)
        aj = _bdot(h, wg_bf[j])
        bj = _bdot(h, wu_bf[j])
        act = ((aj / (1.0 + jnp.exp(-aj))) * bj * gate).astype(jnp.bfloat16)
        yj = _bdot(act, wd_bf[j])
        y = yj if y is None else y + yj

    @pl.when(i > 0)
    def _():
        wait_scatter(slot)

    ybuf[slot] = y

    @pl.when(i == last)
    def _():
        for r in range(TMOE):
            scatter_row(row_ref[last * TMOE + r], slot, r).start(priority=r % 2)
        wait_scatter(other)
        wait_scatter(slot)
        wait_gather(other)


def _moe(layer, hg, tile_group, row_of_slot, w_gate, w_up, w_down):
    def wspec(shape):
        return pl.BlockSpec((1, EXPERTS_PER_GROUP) + shape, lambda i, tg, rows: (layer, tg[i], 0, 0))

    grid_spec = pltpu.PrefetchScalarGridSpec(
        num_scalar_prefetch=2,
        grid=(NT_MOE,),
        in_specs=[pl.BlockSpec(memory_space=pl.ANY),
                  wspec((D_MODEL, EXPERT_HIDDEN)), wspec((D_MODEL, EXPERT_HIDDEN)),
                  wspec((EXPERT_HIDDEN, D_MODEL))],
        out_specs=pl.BlockSpec(memory_space=pl.ANY),
        scratch_shapes=[pltpu.VMEM((2, TMOE, HG_WIDTH), jnp.float32),
                        pltpu.VMEM((2, TMOE, D_MODEL), jnp.float32),
                        pltpu.VMEM((EXPERTS_PER_GROUP, D_MODEL, EXPERT_HIDDEN), jnp.bfloat16),
                        pltpu.VMEM((EXPERTS_PER_GROUP, D_MODEL, EXPERT_HIDDEN), jnp.bfloat16),
                        pltpu.VMEM((EXPERTS_PER_GROUP, EXPERT_HIDDEN, D_MODEL), jnp.bfloat16),
                        pltpu.SemaphoreType.DMA((2,)),
                        pltpu.SemaphoreType.DMA((2,))])
    return pl.pallas_call(
        _moe_kernel,
        grid_spec=grid_spec,
        out_shape=jax.ShapeDtypeStruct((MOE_SLOTS + TMOE, D_MODEL), jnp.float32),
        compiler_params=_cparams("arbitrary"),
        name="moe",
    )(tile_group, row_of_slot, hg, w_gate, w_up, w_down)


def _final_kernel(x1_ref, y_ref, mod_ref, g_ref, o_ref):
    x = x1_ref[...] + mod_ref[0, :, 5 * D_MODEL:6 * D_MODEL] * y_ref[...]
    ms = jnp.mean(x * x, axis=-1, keepdims=True)
    o_ref[...] = x * lax.rsqrt(ms + EPS) * g_ref[...]


def _final_norm(x1, y, mods3, g):
    return pl.pallas_call(
        _final_kernel,
        grid=(T_ALL // TM,),
        in_specs=[pl.BlockSpec((TM, D_MODEL), lambda i: (i, 0)),
                  pl.BlockSpec((TM, D_MODEL), lambda i: (i, 0)),
                  pl.BlockSpec((1, 1, 6 * D_MODEL), lambda i: (_modset(i), 0, 0)),
                  pl.BlockSpec((1, D_MODEL), lambda i: (0, 0))],
        out_specs=pl.BlockSpec((TM, D_MODEL), lambda i: (i, 0)),
        out_shape=jax.ShapeDtypeStruct((T_ALL, D_MODEL), jnp.float32),
        compiler_params=_cparams("arbitrary"),
        name="final_norm",
    )(x1, y, mods3, g)


_QG_ORDER = np.array([0, 3, 1, 4, 2, 5])


def _permute_qg_columns(w_in):
    c1 = SSM_WIDTH + 3 * NA_WIDTH
    qg = w_in[..., c1:c1 + GQA_WIDTH].reshape(*w_in.shape[:-1], GQA_HEADS, HEAD_DIM)
    qg = qg[..., _QG_ORDER, :].reshape(*w_in.shape[:-1], GQA_WIDTH)
    return jnp.concatenate([w_in[..., :c1], qg, w_in[..., c1 + GQA_WIDTH:]], axis=-1)


def _permute_yg_rows(w_out):
    r0 = SSM_WIDTH + NA_WIDTH
    yg = w_out[:, r0:, :].reshape(DEPTH, GQA_HEADS, HEAD_DIM, D_MODEL)[:, _QG_ORDER]
    return jnp.concatenate([w_out[:, :r0, :], yg.reshape(DEPTH, GQA_WIDTH, D_MODEL)], axis=1)


def _rope_tables():
    t = jnp.arange(DEC_SEQ)
    row = (t // GRID_W).astype(jnp.float32)
    col = (t % GRID_W).astype(jnp.float32)
    freqs = ROPE_THETA ** (-jnp.arange(ROPE_F, dtype=jnp.float32) / ROPE_F)
    ang_r = row[:, None] * freqs
    ang_c = col[:, None] * freqs
    ang = jnp.concatenate([ang_r, ang_r, ang_c, ang_c], axis=1)
    cos = jnp.cos(ang)
    sin = jnp.sin(ang)
    sign = jnp.tile(jnp.concatenate([-jnp.ones(ROPE_F), jnp.ones(ROPE_F)]), 2).astype(jnp.float32)
    sin = sin * sign
    cos = jnp.tile(cos, (1, 2))
    sin = jnp.tile(sin, (1, 2))
    cos = jnp.concatenate([jnp.ones((TM, LANES), jnp.float32), cos], axis=0)
    sin = jnp.concatenate([jnp.zeros((TM, LANES), jnp.float32), sin], axis=0)
    return cos, sin


def _pair_layout(cache):
    b, l, h, n, dh = cache.shape
    c = cache.reshape(b, l, h // 2, 2, n, dh).transpose(1, 0, 2, 4, 3, 5)
    return c.reshape(l, b, h // 2, n, 2 * dh).astype(jnp.bfloat16)


def _block_diag_b(bbar):
    bb = bbar.reshape(DEPTH, 2, SSM_GROUP_CH, SSM_GROUPS, SSM_STATE)
    eye = jnp.eye(SSM_GROUPS, dtype=bbar.dtype)
    out = eye[None, None, :, None, :, None] * bb.transpose(0, 1, 3, 2, 4)[:, :, :, :, None, :]
    return out.reshape(DEPTH, 2, SSM_WIDTH, S5_HALF)


def _block_diag_c(c):
    eye = jnp.eye(SSM_GROUPS, dtype=c.dtype)
    out = eye[None, None, :, None, :, None] * c.transpose(0, 1, 2, 4, 3)[:, :, :, :, None, :]
    return out.reshape(DEPTH, 2, S5_HALF, SSM_WIDTH)


def kernel(x_prompt, x_sample, cache_na_k, cache_na_v, cache_gqa_k, cache_gqa_v, state_ssm, c, c_ctx, norm_mix_g, norm_ffn_g, w_ada, b_ada, w_in, ssm_a_re, ssm_a_im, ssm_log_dt, ssm_b_re, ssm_b_im, ssm_c_re, ssm_c_im, ssm_d, w_glu, b_glu, na_rpb, q_norm_g, k_norm_g, w_out, moe_w_group, moe_b_group, moe_w_expert, moe_b_expert, moe_w_gate, moe_w_up, moe_w_down, final_norm_g):
    f32, bf16 = jnp.float32, jnp.bfloat16
    x = jnp.concatenate([x_prompt.reshape(T_CTX, D_MODEL), x_sample.reshape(T_LAT, D_MODEL)], axis=0)

    cvec8 = jnp.concatenate([c_ctx[None, :], c, jnp.zeros((SUBLANES - N_MODSETS, D_MODEL), f32)], axis=0)
    mods_all = _adaln_all(cvec8, w_ada, b_ada)

    w_in_bf = _permute_qg_columns(w_in).astype(bf16)
    w_out_bf = _permute_yg_rows(w_out).astype(bf16)
    w_glu_bf = w_glu.astype(bf16)
    w_router = jnp.concatenate(
        [moe_w_group, moe_w_expert, jnp.zeros((DEPTH, D_MODEL, LANES - N_GROUPS - N_EXPERTS), f32)], axis=-1)
    wr_hi = w_router.astype(bf16)
    wr_lo = (w_router - wr_hi.astype(f32)).astype(bf16)
    b_router = jnp.concatenate(
        [moe_b_group, moe_b_expert, jnp.zeros((DEPTH, LANES - N_GROUPS - N_EXPERTS), f32)], axis=-1)
    q_gain = jnp.tile(q_norm_g, (1, 2))
    k_gain = jnp.tile(k_norm_g, (1, 2))
    lane = np.arange(LANES)
    gsum = jnp.asarray((lane[:, None] // HEAD_DIM) == (lane[None, :] // HEAD_DIM), bf16)
    rope_cos, rope_sin = _rope_tables()

    abar_re, abar_im, bbar_re, bbar_im = _s5_prep(ssm_a_re, ssm_a_im, ssm_log_dt, ssm_b_re, ssm_b_im)
    abar = jnp.concatenate([abar_re, abar_im], axis=-1).reshape(DEPTH, 2, 1, S5_CPLX)
    bblk = jnp.concatenate([_block_diag_b(bbar_re), _block_diag_b(bbar_im)], axis=-1).astype(bf16)
    cblk = jnp.concatenate([_block_diag_c(ssm_c_re), -_block_diag_c(ssm_c_im)], axis=2).astype(bf16)
    h0 = state_ssm.transpose(1, 2, 0, 5, 3, 4).reshape(DEPTH, 2, DEC_BATCH, S5_CPLX)

    ck_na, cv_na = _pair_layout(cache_na_k), _pair_layout(cache_na_v)
    ck_g, cv_g = _pair_layout(cache_gqa_k)[:, :, 0], _pair_layout(cache_gqa_v)[:, :, 0]
    na_bias = _na_bias_table(na_rpb)

    kv32_l, fin_l = [], []
    moe_residual = None
    for l in range(DEPTH):
        mods3 = mods_all[l, :N_MODSETS].reshape(N_MODSETS, 1, 6 * D_MODEL)
        u, qkv, kv32, x = _in_proj(x, moe_residual, mods3, norm_mix_g[l][None], w_in_bf[l], q_gain[l][None],
                                   k_gain[l][None], gsum, rope_cos, rope_sin)
        kv32_l.append(kv32)

        u_tm = jnp.pad(u.reshape(S5_SEQS, S5_CHUNK, SSM_WIDTH).transpose(1, 0, 2),
                       ((0, 0), (0, S5_ROWS - S5_SEQS), (0, 0)))
        y_dir, powers, finals = _s5_scan(u_tm.reshape(S5_CHUNK * S5_ROWS, SSM_WIDTH), bblk[l], abar[l], cblk[l])
        fin_l.append(finals[:, :BATCH])
        y_tm = _s5_finish(u_tm, y_dir.reshape(2, S5_CHUNK, S5_ROWS, SSM_WIDTH), powers, finals, h0[l], cblk[l],
                          ssm_d[l][None], w_glu_bf[l], b_glu[l][None])
        y_ssm = y_tm[:, :S5_SEQS].transpose(1, 0, 2).reshape(T_ALL, SSM_WIDTH)

        y_ctx = _attn_ctx(qkv)
        y_g_lat = _gqa_lat(qkv, ck_g[l], cv_g[l])
        y_na_lat = _na_lat(l, qkv, ck_na[l], cv_na[l], na_bias)

        x1, hg = _out_proj(x, y_ssm, y_ctx, y_na_lat, y_g_lat, mods3, norm_ffn_g[l][None], w_out_bf[l],
                           wr_hi[l], wr_lo[l], b_router[l][None])
        tile_group, row_of_slot = _moe_dispatch(hg[:, D_MODEL + GROUP_LANE])
        y_moe = _moe(l, hg, tile_group, row_of_slot, moe_w_gate, moe_w_up, moe_w_down)
        x, moe_residual = x1, (y_moe, mods3)

    y = _final_norm(x, moe_residual[0], moe_residual[1], final_norm_g[None])
    y_prompt = y[:T_CTX].reshape(BATCH, SEQ, D_MODEL)
    y_sample = y[T_CTX:].reshape(DEC_BATCH, DEC_SEQ, D_MODEL)

    kv32 = jnp.stack(kv32_l, axis=0).reshape(DEPTH, BATCH, SEQ, KV32_WIDTH)

    def heads(t, n_heads):
        return t.reshape(DEPTH, BATCH, SEQ, n_heads, HEAD_DIM).transpose(1, 0, 3, 2, 4)

    new_na_k = heads(kv32[..., 0:NA_WIDTH], NA_HEADS)
    new_na_v = heads(kv32[..., NA_WIDTH:2 * NA_WIDTH], NA_HEADS)
    new_gqa_k = heads(kv32[..., 2 * NA_WIDTH:2 * NA_WIDTH + KV_WIDTH], GQA_KV_HEADS)
    new_gqa_v = heads(kv32[..., 2 * NA_WIDTH + KV_WIDTH:], GQA_KV_HEADS)
    fin = jnp.stack(fin_l, axis=0).reshape(DEPTH, 2, BATCH, 2, SSM_GROUPS, SSM_STATE)
    new_state = fin.transpose(2, 0, 1, 4, 5, 3)
    return (y_prompt, y_sample, new_na_k, new_na_v, new_gqa_k, new_gqa_v, new_state)
```

```python
import functools
import math

import jax
import jax.numpy as jnp
import numpy as np
from jax import lax
from jax.experimental import pallas as pl
from jax.experimental.pallas import tpu as pltpu

D_MODEL = 1024
BATCH = 16
SEQ = 256
DEPTH = 4
DEC_BATCH = 2
DEC_SEQ = 4096
PAST_LEN = 512
GRID_W = 64
GRID_H = DEC_SEQ // GRID_W
HEAD_DIM = 64
SSM_WIDTH = 256
SSM_GROUP_CH = 16
SSM_GROUPS = 16
SSM_STATE = 64
NA_HEADS = 6
NA_WIDTH = 384
NA_KH = 8
NA_KW = 16
GQA_HEADS = 6
GQA_KV_HEADS = 2
GQA_REP = 3
GQA_WIDTH = 384
KV_WIDTH = 128
IN_WIDTH = 2048
ROPE_THETA = 10000.0
ROPE_F = 16
N_GROUPS = 4
EXPERTS_PER_GROUP = 4
N_EXPERTS = 16
EXPERT_HIDDEN = 256
EPS = 1e-6
NEG_INF = -1e30

T_CTX = BATCH * SEQ
T_LAT = DEC_BATCH * DEC_SEQ
T_ALL = T_CTX + T_LAT
N_MODSETS = 1 + DEC_BATCH
LANES = 128
SUBLANES = 8
PAIR = 2 * HEAD_DIM
ATTN_SCALE = HEAD_DIM ** -0.5

S5_CHUNK = 256
S5_SEQS = T_ALL // S5_CHUNK
S5_ROWS = 56
S5_POW_ROW = 48
S5_LAT0 = BATCH
S5_NLAT = S5_SEQS - BATCH
S5_TB = 16
S5_NBLK = S5_CHUNK // S5_TB
S5_CPLX = 2 * SSM_GROUPS * SSM_STATE
S5_HALF = SSM_GROUPS * SSM_STATE

TM = 512
VMEM_LIMIT = 56 * 1024 * 1024


def _cparams(*sem):
    return pltpu.CompilerParams(dimension_semantics=sem, vmem_limit_bytes=VMEM_LIMIT)


def _modset(i):
    return jnp.where(i < T_CTX // TM, 0, 1 + (i - T_CTX // TM) // (DEC_SEQ // TM))


def _bdot(a, b):
    return jnp.dot(a, b, preferred_element_type=jnp.float32)


def _bdot_t(a, b):
    return lax.dot_general(a, b, (((1,), (1,)), ((), ())), preferred_element_type=jnp.float32)


def _ada_kernel(c_ref, w_ref, b_ref, o_ref):
    c = c_ref[...]
    s = c / (1.0 + jnp.exp(-c))
    o_ref[0] = _bdot(s.astype(jnp.bfloat16), w_ref[0].astype(jnp.bfloat16)) + b_ref[0]


def _adaln_all(cvec8, w_ada, b_ada):
    tn = 1024
    return pl.pallas_call(
        _ada_kernel,
        grid=(DEPTH, 6 * D_MODEL // tn),
        in_specs=[pl.BlockSpec((SUBLANES, D_MODEL), lambda l, j: (0, 0)),
                  pl.BlockSpec((1, D_MODEL, tn), lambda l, j: (l, 0, j)),
                  pl.BlockSpec((1, 1, tn), lambda l, j: (l, 0, j))],
        out_specs=pl.BlockSpec((1, SUBLANES, tn), lambda l, j: (l, 0, j)),
        out_shape=jax.ShapeDtypeStruct((DEPTH, SUBLANES, 6 * D_MODEL), jnp.float32),
        compiler_params=_cparams("arbitrary", "arbitrary"),
        name="adaln",
    )(cvec8, w_ada, b_ada.reshape(DEPTH, 1, 6 * D_MODEL))


def _s5prep_kernel(ar_ref, ai_ref, ldt_ref, br_ref, bi_ref, abar_re, abar_im, bbar_re, bbar_im):
    ar, ai = ar_ref[0], ai_ref[0]
    dt = jnp.exp(ldt_ref[0])
    mag = jnp.exp(ar * dt)
    are = mag * jnp.cos(ai * dt)
    aim = mag * jnp.sin(ai * dt)
    zr, zi = are - 1.0, aim
    den = ar * ar + ai * ai
    cre = (zr * ar + zi * ai) / den
    cim = (zi * ar - zr * ai) / den
    abar_re[0] = are
    abar_im[0] = aim
    br, bi = br_ref[0], bi_ref[0]
    bbar_re[0] = cre * br - cim * bi
    bbar_im[0] = cre * bi + cim * br


def _s5_prep(a_re, a_im, log_dt, b_re, b_im):
    n = DEPTH * 2
    ar = a_re.reshape(n, 1, S5_HALF)
    ai = a_im.reshape(n, 1, S5_HALF)
    ldt = jnp.broadcast_to(log_dt[..., None], (DEPTH, 2, SSM_GROUPS, SSM_STATE)).reshape(n, 1, S5_HALF)
    br = b_re.transpose(0, 1, 4, 2, 3).reshape(n, SSM_GROUP_CH, S5_HALF)
    bi = b_im.transpose(0, 1, 4, 2, 3).reshape(n, SSM_GROUP_CH, S5_HALF)
    vec = jax.ShapeDtypeStruct((n, 1, S5_HALF), jnp.float32)
    mat = jax.ShapeDtypeStruct((n, SSM_GROUP_CH, S5_HALF), jnp.float32)
    vspec = pl.BlockSpec((1, 1, S5_HALF), lambda j: (j, 0, 0))
    mspec = pl.BlockSpec((1, SSM_GROUP_CH, S5_HALF), lambda j: (j, 0, 0))
    return pl.pallas_call(
        _s5prep_kernel,
        grid=(n,),
        in_specs=[vspec, vspec, vspec, mspec, mspec],
        out_specs=[vspec, vspec, mspec, mspec],
        out_shape=(vec, vec, mat, mat),
        compiler_params=_cparams("arbitrary"),
        name="s5prep",
    )(ar, ai, ldt, br, bi)


def _head_rms(blk, gsum, gain):
    sq = blk * blk
    hi = sq.astype(jnp.bfloat16)
    lo = (sq - hi.astype(jnp.float32)).astype(jnp.bfloat16)
    ms = (_bdot(hi, gsum) + _bdot(lo, gsum)) * (1.0 / HEAD_DIM)
    return blk * lax.rsqrt(ms + EPS) * gain


def _rope(blk, cos, sin_signed, first_half):
    partner = jnp.where(first_half, pltpu.roll(blk, LANES - ROPE_F, 1), pltpu.roll(blk, ROPE_F, 1))
    return blk * cos + partner * sin_signed


def _inproj_kernel(has_moe_residual, *refs):
    if has_moe_residual:
        (x_ref, y_ref, pmod_ref, mod_ref, g_ref, w_ref, qg_ref, kg_ref, gsum_ref, cos_ref, sin_ref,
         u_ref, qkv_ref, kv32_ref, xo_ref) = refs
        x = x_ref[...] + pmod_ref[0, :, 5 * D_MODEL:6 * D_MODEL] * _load_token_tiles(y_ref, TM)
        xo_ref[...] = x
    else:
        (x_ref, mod_ref, g_ref, w_ref, qg_ref, kg_ref, gsum_ref, cos_ref, sin_ref,
         u_ref, qkv_ref, kv32_ref) = refs
        x = x_ref[...]
    i = pl.program_id(0)
    shift = mod_ref[0, :, 0:D_MODEL]
    scale = mod_ref[0, :, D_MODEL:2 * D_MODEL]
    ms = jnp.mean(x * x, axis=-1, keepdims=True)
    h = (x * lax.rsqrt(ms + EPS) * g_ref[...]) * (1.0 + scale) + shift
    proj = _bdot(h.astype(jnp.bfloat16), w_ref[...])
    u_ref[...] = proj[:, 0:SSM_WIDTH]

    c0 = SSM_WIDTH
    qn = proj[:, c0:c0 + NA_WIDTH] * ATTN_SCALE
    kn = proj[:, c0 + NA_WIDTH:c0 + 2 * NA_WIDTH]
    vn = proj[:, c0 + 2 * NA_WIDTH:c0 + 3 * NA_WIDTH]
    c1 = c0 + 3 * NA_WIDTH
    vg = proj[:, c1 + GQA_WIDTH + KV_WIDTH:c1 + GQA_WIDTH + 2 * KV_WIDTH]

    gsum = gsum_ref[...]
    cos, sin = cos_ref[...], sin_ref[...]
    lane = lax.broadcasted_iota(jnp.int32, (1, LANES), 1)
    first_half = (lane & (2 * ROPE_F - 1)) < ROPE_F
    qg_blocks = []
    for p in range(GQA_WIDTH // PAIR):
        blk = proj[:, c1 + p * PAIR:c1 + (p + 1) * PAIR]
        blk = _rope(_head_rms(blk, gsum, qg_ref[...]), cos, sin, first_half)
        qg_blocks.append(blk * ATTN_SCALE)
    kg = proj[:, c1 + GQA_WIDTH:c1 + GQA_WIDTH + KV_WIDTH]
    kg = _rope(_head_rms(kg, gsum, kg_ref[...]), cos, sin, first_half)

    o = 0
    for piece in [qn, kn, vn] + qg_blocks + [kg, vg]:
        w = piece.shape[1]
        qkv_ref[:, o:o + w] = piece.astype(jnp.bfloat16)
        o += w

    @pl.when(i < T_CTX // TM)
    def _():
        kv32_ref[:, 0:NA_WIDTH] = kn
        kv32_ref[:, NA_WIDTH:2 * NA_WIDTH] = vn
        kv32_ref[:, 2 * NA_WIDTH:2 * NA_WIDTH + KV_WIDTH] = kg
        kv32_ref[:, 2 * NA_WIDTH + KV_WIDTH:2 * NA_WIDTH + 2 * KV_WIDTH] = vg


QKV_WIDTH = 3 * NA_WIDTH + GQA_WIDTH + 2 * KV_WIDTH
KV32_WIDTH = 2 * NA_WIDTH + 2 * KV_WIDTH
QN_BLK, KN_BLK, VN_BLK, QG_BLK, KG_BLK, VG_BLK = 0, 3, 6, 9, 12, 13


def _rope_block(i):
    n_ctx = T_CTX // TM
    return jnp.where(i < n_ctx, 0, 1 + (i - n_ctx) % (DEC_SEQ // TM))


def _in_proj(x, moe_residual, mods3, norm_g, w_in_bf, q_gain, k_gain, gsum, rope_cos, rope_sin):
    n_ctx = T_CTX // TM
    row_spec = pl.BlockSpec((TM, D_MODEL), lambda i: (i, 0))
    mod_spec = pl.BlockSpec((1, 1, 6 * D_MODEL), lambda i: (_modset(i), 0, 0))
    args, in_specs = [x], [row_spec]
    out_specs = [pl.BlockSpec((TM, SSM_WIDTH), lambda i: (i, 0)),
                 pl.BlockSpec((TM, QKV_WIDTH), lambda i: (i, 0)),
                 pl.BlockSpec((TM, KV32_WIDTH), lambda i: (jnp.minimum(i, n_ctx - 1), 0))]
    out_shape = [jax.ShapeDtypeStruct((T_ALL, SSM_WIDTH), jnp.float32),
                 jax.ShapeDtypeStruct((T_ALL, QKV_WIDTH), jnp.bfloat16),
                 jax.ShapeDtypeStruct((T_CTX, KV32_WIDTH), jnp.float32)]
    if moe_residual is not None:
        args += list(moe_residual)
        in_specs += [pl.BlockSpec((TM * TILE_ROWS, LANES), lambda i: (i, 0)), mod_spec]
        out_specs.append(row_spec)
        out_shape.append(jax.ShapeDtypeStruct((T_ALL, D_MODEL), jnp.float32))
    args += [mods3, norm_g, w_in_bf, q_gain, k_gain, gsum, rope_cos, rope_sin]
    in_specs += [mod_spec,
                 pl.BlockSpec((1, D_MODEL), lambda i: (0, 0)),
                 pl.BlockSpec((D_MODEL, IN_WIDTH), lambda i: (0, 0)),
                 pl.BlockSpec((1, LANES), lambda i: (0, 0)),
                 pl.BlockSpec((1, LANES), lambda i: (0, 0)),
                 pl.BlockSpec((LANES, LANES), lambda i: (0, 0)),
                 pl.BlockSpec((TM, LANES), lambda i: (_rope_block(i), 0)),
                 pl.BlockSpec((TM, LANES), lambda i: (_rope_block(i), 0))]
    outs = pl.pallas_call(
        functools.partial(_inproj_kernel, moe_residual is not None),
        grid=(T_ALL // TM,),
        in_specs=in_specs,
        out_specs=out_specs,
        out_shape=out_shape,
        compiler_params=_cparams("arbitrary"),
        name="in_proj",
    )(*args)
    return outs if moe_residual is not None else (*outs, x)


def _s5scan_kernel(u_ref, b_ref, a_ref, c_ref, y_ref, p_ref, fin_ref, hbuf):
    d = pl.program_id(0)
    k = pl.program_id(1)
    rows = S5_TB * S5_ROWS
    a_re = jnp.broadcast_to(a_ref[0, :, 0:S5_HALF], (SUBLANES, S5_HALF))
    a_im = jnp.broadcast_to(a_ref[0, :, S5_HALF:S5_CPLX], (SUBLANES, S5_HALF))
    ub = u_ref[...].astype(jnp.bfloat16)

    def run(forward):
        data0 = S5_ROWS if forward else 0
        prev0 = 0 if forward else rows
        last0 = rows if forward else 0

        @pl.when(k == 0)
        def _():
            r = lax.broadcasted_iota(jnp.int32, (S5_ROWS, S5_CPLX), 0)
            l = lax.broadcasted_iota(jnp.int32, (S5_ROWS, S5_CPLX), 1)
            hbuf[prev0:prev0 + S5_ROWS, :] = jnp.where((r == S5_POW_ROW) & (l < S5_HALF), 1.0, 0.0)

        @pl.when(k > 0)
        def _():
            hbuf[prev0:prev0 + S5_ROWS, :] = hbuf[last0:last0 + S5_ROWS, :]

        hbuf[data0:data0 + rows, :] = _bdot(ub, b_ref[0])

        def step(i, carry):
            t = i if forward else S5_TB - 1 - i
            cur = pl.multiple_of(data0 + t * S5_ROWS, SUBLANES)
            prv = pl.multiple_of(cur - S5_ROWS if forward else cur + S5_ROWS, SUBLANES)
            for st in range(S5_ROWS // SUBLANES):
                rc = pl.ds(cur + st * SUBLANES, SUBLANES)
                rp = pl.ds(prv + st * SUBLANES, SUBLANES)
                hr = hbuf[rp, 0:S5_HALF]
                hi = hbuf[rp, S5_HALF:S5_CPLX]
                nr = a_re * hr - a_im * hi + hbuf[rc, 0:S5_HALF]
                ni = a_re * hi + a_im * hr + hbuf[rc, S5_HALF:S5_CPLX]
                hbuf[rc, 0:S5_HALF] = nr
                hbuf[rc, S5_HALF:S5_CPLX] = ni
            return carry

        lax.fori_loop(0, S5_TB, step, 0, unroll=True)
        y_ref[0] = _bdot(hbuf[data0:data0 + rows, :].astype(jnp.bfloat16), c_ref[0])
        for t in range(S5_TB):
            r0 = data0 + t * S5_ROWS + S5_POW_ROW
            p_ref[0, t:t + 1, :] = hbuf[r0:r0 + 1, :]

        @pl.when(k == S5_NBLK - 1)
        def _():
            fin_ref[0] = hbuf[last0:last0 + S5_ROWS, :]

    @pl.when(d == 0)
    def _():
        run(True)

    @pl.when(d == 1)
    def _():
        run(False)


def _s5_scan(u_tm, bblk, abar, cblk):
    rows = S5_TB * S5_ROWS

    def tblk(d, k):
        return jnp.where(d == 0, k, S5_NBLK - 1 - k)

    return pl.pallas_call(
        _s5scan_kernel,
        grid=(2, S5_NBLK),
        in_specs=[pl.BlockSpec((rows, SSM_WIDTH), lambda d, k: (tblk(d, k), 0)),
                  pl.BlockSpec((1, SSM_WIDTH, S5_CPLX), lambda d, k: (d, 0, 0)),
                  pl.BlockSpec((1, 1, S5_CPLX), lambda d, k: (d, 0, 0)),
                  pl.BlockSpec((1, S5_CPLX, SSM_WIDTH), lambda d, k: (d, 0, 0))],
        out_specs=[pl.BlockSpec((1, rows, SSM_WIDTH), lambda d, k: (d, tblk(d, k), 0)),
                   pl.BlockSpec((1, S5_TB, S5_CPLX), lambda d, k: (d, tblk(d, k), 0)),
                   pl.BlockSpec((1, S5_ROWS, S5_CPLX), lambda d, k: (d, 0, 0))],
        out_shape=(jax.ShapeDtypeStruct((2, S5_CHUNK * S5_ROWS, SSM_WIDTH), jnp.float32),
                   jax.ShapeDtypeStruct((2, S5_CHUNK, S5_CPLX), jnp.float32),
                   jax.ShapeDtypeStruct((2, S5_ROWS, S5_CPLX), jnp.float32)),
        scratch_shapes=[pltpu.VMEM((rows + S5_ROWS, S5_CPLX), jnp.float32)],
        compiler_params=_cparams("arbitrary", "arbitrary"),
        name="s5_scan",
    )(u_tm, bblk, abar, cblk)


def _cmul(pr, pi, cr, ci):
    return pr * cr - pi * ci, pr * ci + pi * cr


def _s5fin_kernel(u_ref, yf_ref, yb_ref, p_ref, fin_ref, h0_ref, c_ref, d_ref, wg_ref, bg_ref,
                  o_ref, carry, hc, ybuf):
    k = pl.program_id(0)
    nchunk = DEC_SEQ // S5_CHUNK

    @pl.when(k == 0)
    def _():
        for d in range(2):
            t_full = S5_CHUNK - 1 if d == 0 else 0
            pr = p_ref[d, t_full:t_full + 1, 0:S5_HALF]
            pi = p_ref[d, t_full:t_full + 1, S5_HALF:S5_CPLX]
            for b in range(DEC_BATCH):
                cr = h0_ref[d, b:b + 1, 0:S5_HALF]
                ci = h0_ref[d, b:b + 1, S5_HALF:S5_CPLX]
                order = range(nchunk) if d == 0 else range(nchunk - 1, -1, -1)
                for c in order:
                    row = b * nchunk + c
                    carry[d, row:row + 1, 0:S5_HALF] = cr
                    carry[d, row:row + 1, S5_HALF:S5_CPLX] = ci
                    fr = fin_ref[d, S5_LAT0 + row:S5_LAT0 + row + 1, 0:S5_HALF]
                    fi = fin_ref[d, S5_LAT0 + row:S5_LAT0 + row + 1, S5_HALF:S5_CPLX]
                    mr, mi = _cmul(pr, pi, cr, ci)
                    cr, ci = fr + mr, fi + mi

    ybuf[...] = yf_ref[0] + yb_ref[0] + u_ref[...] * d_ref[...]
    for d in range(2):
        cr = carry[d, :, 0:S5_HALF]
        ci = carry[d, :, S5_HALF:S5_CPLX]
        for t in range(S5_TB):
            pr = p_ref[d, pl.ds(k * S5_TB + t, 1), 0:S5_HALF]
            pi = p_ref[d, pl.ds(k * S5_TB + t, 1), S5_HALF:S5_CPLX]
            mr, mi = _cmul(pr, pi, cr, ci)
            hc[t * S5_NLAT:(t + 1) * S5_NLAT, 0:S5_HALF] = mr.astype(jnp.bfloat16)
            hc[t * S5_NLAT:(t + 1) * S5_NLAT, S5_HALF:S5_CPLX] = mi.astype(jnp.bfloat16)
        corr = _bdot(hc[...], c_ref[d])
        ybuf[:, S5_LAT0:S5_LAT0 + S5_NLAT, :] += corr.reshape(S5_TB, S5_NLAT, SSM_WIDTH)

    y = ybuf[...].reshape(S5_TB * S5_ROWS, SSM_WIDTH)
    yg = 0.5 * y * (1.0 + jnp.tanh(math.sqrt(2.0 / math.pi) * (y + 0.044715 * (y * y * y))))
    z = _bdot(yg.astype(jnp.bfloat16), wg_ref[...]) + bg_ref[...]
    out = yg / (1.0 + jnp.exp(-z))
    o_ref[...] = out.reshape(S5_TB, S5_ROWS, SSM_WIDTH).astype(jnp.bfloat16)


def _s5_finish(u_tm3, y_dir4, powers, finals, h0, cblk, ssm_d, w_glu_bf, b_glu):
    blk3 = (S5_TB, S5_ROWS, SSM_WIDTH)
    return pl.pallas_call(
        _s5fin_kernel,
        grid=(S5_NBLK,),
        in_specs=[pl.BlockSpec(blk3, lambda k: (k, 0, 0)),
                  pl.BlockSpec((1,) + blk3, lambda k: (0, k, 0, 0)),
                  pl.BlockSpec((1,) + blk3, lambda k: (1, k, 0, 0)),
                  pl.BlockSpec((2, S5_CHUNK, S5_CPLX), lambda k: (0, 0, 0)),
                  pl.BlockSpec((2, S5_ROWS, S5_CPLX), lambda k: (0, 0, 0)),
                  pl.BlockSpec((2, DEC_BATCH, S5_CPLX), lambda k: (0, 0, 0)),
                  pl.BlockSpec((2, S5_CPLX, SSM_WIDTH), lambda k: (0, 0, 0)),
                  pl.BlockSpec((1, SSM_WIDTH), lambda k: (0, 0)),
                  pl.BlockSpec((SSM_WIDTH, SSM_WIDTH), lambda k: (0, 0)),
                  pl.BlockSpec((1, SSM_WIDTH), lambda k: (0, 0))],
        out_specs=pl.BlockSpec(blk3, lambda k: (k, 0, 0)),
        out_shape=jax.ShapeDtypeStruct((S5_CHUNK, S5_ROWS, SSM_WIDTH), jnp.bfloat16),
        scratch_shapes=[pltpu.VMEM((2, S5_NLAT, S5_CPLX), jnp.float32),
                        pltpu.VMEM((S5_TB * S5_NLAT, S5_CPLX), jnp.bfloat16),
                        pltpu.VMEM(blk3, jnp.float32)],
        compiler_params=_cparams("arbitrary"),
        name="s5_finish",
    )(u_tm3, y_dir4, y_dir4, powers, finals, h0, cblk, ssm_d, w_glu_bf, b_glu)


def _pair_attention(q, kvs, biases):
    lane = lax.broadcasted_iota(jnp.int32, (1, PAIR), 1)
    outs = []
    for a in range(2):
        sel = (lane < HEAD_DIM) if a == 0 else (lane >= HEAD_DIM)
        qa = jnp.where(sel, q, jnp.zeros_like(q))
        ss = []
        for (kk, _), bias in zip(kvs, biases):
            s = _bdot_t(qa, kk)
            if bias is not None:
                s = s + bias[a]
            ss.append(s)
        m = ss[0].max(axis=-1, keepdims=True)
        for s in ss[1:]:
            m = jnp.maximum(m, s.max(axis=-1, keepdims=True))
        den = None
        acc = None
        for s, (_, vv) in zip(ss, kvs):
            p = jnp.exp(s - m)
            l = p.sum(axis=-1, keepdims=True)
            o = _bdot(p.astype(jnp.bfloat16), vv)
            den = l if den is None else den + l
            acc = o if acc is None else acc + o
        outs.append(acc / den)
    return jnp.where(lane < HEAD_DIM, outs[0], outs[1])


ATTN_WIDTH = NA_WIDTH + GQA_WIDTH
NPAIR = NA_WIDTH // PAIR


def _attn_ctx_kernel(qkv_ref, o_ref):
    def blk(j):
        return qkv_ref[:, j * PAIR:(j + 1) * PAIR]

    for p in range(NPAIR):
        o = _pair_attention(blk(QN_BLK + p), [(blk(KN_BLK + p), blk(VN_BLK + p))], [None])
        o_ref[:, p * PAIR:(p + 1) * PAIR] = o.astype(jnp.bfloat16)
    for p in range(NPAIR):
        o = _pair_attention(blk(QG_BLK + p), [(blk(KG_BLK), blk(VG_BLK))], [None])
        o_ref[:, (NPAIR + p) * PAIR:(NPAIR + p + 1) * PAIR] = o.astype(jnp.bfloat16)


def _attn_ctx(qkv):
    return pl.pallas_call(
        _attn_ctx_kernel,
        grid=(BATCH,),
        in_specs=[pl.BlockSpec((SEQ, QKV_WIDTH), lambda b: (b, 0))],
        out_specs=pl.BlockSpec((SEQ, ATTN_WIDTH), lambda b: (b, 0)),
        out_shape=jax.ShapeDtypeStruct((T_CTX, ATTN_WIDTH), jnp.bfloat16),
        compiler_params=_cparams("arbitrary"),
        name="attn_ctx",
    )(qkv)


TQ = 256


def _gqa_lat_kernel(q_ref, k_ref, v_ref, ck_ref, cv_ref, o_ref):
    kvs = [(ck_ref[0], cv_ref[0]), (k_ref[...], v_ref[...])]
    o_ref[...] = _pair_attention(q_ref[...], kvs, [None, None]).astype(jnp.bfloat16)


def _gqa_lat(qkv, ck, cv):
    nq = DEC_SEQ // TQ
    ctx_blocks = T_CTX // DEC_SEQ
    return pl.pallas_call(
        _gqa_lat_kernel,
        grid=(DEC_BATCH, NPAIR, nq),
        in_specs=[pl.BlockSpec((TQ, PAIR), lambda b, p, i: (T_CTX // TQ + b * nq + i, QG_BLK + p)),
                  pl.BlockSpec((DEC_SEQ, PAIR), lambda b, p, i: (ctx_blocks + b, KG_BLK)),
                  pl.BlockSpec((DEC_SEQ, PAIR), lambda b, p, i: (ctx_blocks + b, VG_BLK)),
                  pl.BlockSpec((1, PAST_LEN, PAIR), lambda b, p, i: (b, 0, 0)),
                  pl.BlockSpec((1, PAST_LEN, PAIR), lambda b, p, i: (b, 0, 0))],
        out_specs=pl.BlockSpec((TQ, PAIR), lambda b, p, i: (b * nq + i, p)),
        out_shape=jax.ShapeDtypeStruct((T_LAT, GQA_WIDTH), jnp.bfloat16),
        compiler_params=_cparams("arbitrary", "arbitrary", "arbitrary"),
        name="gqa_lat",
    )(qkv, qkv, qkv, ck, cv)


NA_BLK = 4
NA_NBLK = GRID_H // NA_BLK
NA_SLAB_ROWS = NA_KH + NA_BLK - 1
NA_SLAB = NA_SLAB_ROWS * GRID_W
NA_QROWS = NA_BLK * GRID_W
NA_BLOCK_TYPES = (0, 2, NA_NBLK - 1)


def _na_slab_start(j):
    return jnp.clip(j * NA_BLK - NA_KH // 2, 0, GRID_H - NA_SLAB_ROWS)


def _na_lat_kernel(q_ref, k_ref, v_ref, ck_ref, cv_ref, bias_ref, o_ref):
    ck, cv = ck_ref[0, 0], cv_ref[0, 0]
    lane = lax.broadcasted_iota(jnp.int32, (1, PAIR), 1)

    def block(j, carry):
        q0 = pl.multiple_of(j * NA_QROWS, NA_QROWS)
        k0 = pl.multiple_of(_na_slab_start(j) * GRID_W, GRID_W)
        btype = jnp.where(j == 0, 0, jnp.where(j == NA_NBLK - 1, 2, 1))
        q = q_ref[pl.ds(q0, NA_QROWS), :]
        lhs = jnp.concatenate([jnp.where(lane < HEAD_DIM, q, jnp.zeros_like(q)),
                               jnp.where(lane >= HEAD_DIM, q, jnp.zeros_like(q))], axis=0)
        ks = k_ref[pl.ds(k0, NA_SLAB), :]
        vs = v_ref[pl.ds(k0, NA_SLAB), :]
        s_lat = _bdot_t(lhs, ks) + bias_ref[0, 0, btype]
        s_ctx = _bdot_t(lhs, ck)
        m = jnp.maximum(s_lat.max(axis=-1, keepdims=True), s_ctx.max(axis=-1, keepdims=True))
        p_lat = jnp.exp(s_lat - m)
        p_ctx = jnp.exp(s_ctx - m)
        den = p_lat.sum(axis=-1, keepdims=True) + p_ctx.sum(axis=-1, keepdims=True)
        o = (_bdot(p_lat.astype(jnp.bfloat16), vs) + _bdot(p_ctx.astype(jnp.bfloat16), cv)) / den
        out = jnp.where(lane < HEAD_DIM, o[0:NA_QROWS], o[NA_QROWS:2 * NA_QROWS])
        o_ref[pl.ds(q0, NA_QROWS), :] = out.astype(jnp.bfloat16)
        return carry

    lax.fori_loop(0, NA_NBLK, block, 0)


def _na_lat(layer, qkv, ck, cv, bias):
    ctx_blocks = T_CTX // DEC_SEQ
    return pl.pallas_call(
        _na_lat_kernel,
        grid=(DEC_BATCH, NPAIR),
        in_specs=[pl.BlockSpec((DEC_SEQ, PAIR), lambda b, p: (ctx_blocks + b, QN_BLK + p)),
                  pl.BlockSpec((DEC_SEQ, PAIR), lambda b, p: (ctx_blocks + b, KN_BLK + p)),
                  pl.BlockSpec((DEC_SEQ, PAIR), lambda b, p: (ctx_blocks + b, VN_BLK + p)),
                  pl.BlockSpec((1, 1, PAST_LEN, PAIR), lambda b, p: (b, p, 0, 0)),
                  pl.BlockSpec((1, 1, PAST_LEN, PAIR), lambda b, p: (b, p, 0, 0)),
                  pl.BlockSpec((1, 1, len(NA_BLOCK_TYPES), 2 * NA_QROWS, NA_SLAB),
                               lambda b, p: (layer, p, 0, 0, 0))],
        out_specs=pl.BlockSpec((DEC_SEQ, PAIR), lambda b, p: (b, p)),
        out_shape=jax.ShapeDtypeStruct((T_LAT, NA_WIDTH), jnp.bfloat16),
        compiler_params=_cparams("arbitrary", "arbitrary"),
        name="na_lat",
    )(qkv, qkv, qkv, ck, cv, bias)


def _na_bias_table(rpb):
    nt = len(NA_BLOCK_TYPES)
    nl = rpb.shape[0]
    j = np.array(NA_BLOCK_TYPES)[:, None]
    r = j * NA_BLK + np.arange(NA_BLK)[None, :]
    slab0 = np.clip(j * NA_BLK - NA_KH // 2, 0, GRID_H - NA_SLAB_ROWS)
    win0 = np.clip(r - NA_KH // 2, 0, GRID_H - NA_KH)
    key_row = slab0[:, :, None] + np.arange(NA_SLAB_ROWS)[None, None, :]
    row_in = (key_row >= win0[:, :, None]) & (key_row < win0[:, :, None] + NA_KH)
    dr = np.clip(key_row - r[:, :, None] + (NA_KH - 1), 0, 2 * NA_KH - 2)
    cq = np.arange(GRID_W)
    ck = np.arange(GRID_W)
    cs = np.clip(cq - NA_KW // 2, 0, GRID_W - NA_KW)
    col_in = (ck[None, :] >= cs[:, None]) & (ck[None, :] < cs[:, None] + NA_KW)
    dc = np.clip(ck[None, :] - cq[:, None] + NA_KW - 1, 0, 2 * NA_KW - 2)
    sel_c = jnp.asarray(np.eye(2 * NA_KW - 1, dtype=np.float32)[dc])
    cols = jnp.einsum("lhdc,qkc->lhdqk", rpb.astype(jnp.float32), sel_c, precision=lax.Precision.HIGHEST)
    cols = jnp.where(jnp.asarray(col_in), cols, NEG_INF)
    n_off = 2 * NA_KH - 1
    cols = cols.reshape(nl, NA_HEADS // 2, 2, n_off, GRID_W, GRID_W)
    masked = jnp.full((nl, NA_HEADS // 2, 2, 1, GRID_W, GRID_W), NEG_INF, jnp.float32)
    cols = jnp.concatenate([cols, masked], axis=3)
    which = jnp.asarray(np.where(row_in, dr, n_off).reshape(-1), jnp.int32)

    def assemble(which_ref, cols_ref, o_ref):
        t = pl.program_id(2)
        for e in range(2):
            for n in range(NA_BLK):
                r0 = (e * NA_BLK + n) * GRID_W
                for i in range(NA_SLAB_ROWS):
                    d = which_ref[(t * NA_BLK + n) * NA_SLAB_ROWS + i]
                    o_ref[0, 0, 0, r0:r0 + GRID_W, i * GRID_W:(i + 1) * GRID_W] = cols_ref[0, 0, e, d]

    grid_spec = pltpu.PrefetchScalarGridSpec(
        num_scalar_prefetch=1,
        grid=(nl, NA_HEADS // 2, nt),
        in_specs=[pl.BlockSpec((1, 1, 2, n_off + 1, GRID_W, GRID_W), lambda l, p, t, w: (l, p, 0, 0, 0, 0))],
        out_specs=pl.BlockSpec((1, 1, 1, 2 * NA_QROWS, NA_SLAB), lambda l, p, t, w: (l, p, t, 0, 0)))
    return pl.pallas_call(
        assemble,
        grid_spec=grid_spec,
        out_shape=jax.ShapeDtypeStruct((nl, NA_HEADS // 2, nt, 2 * NA_QROWS, NA_SLAB), jnp.float32),
        compiler_params=_cparams("arbitrary", "arbitrary", "arbitrary"),
        name="na_bias",
    )(which, cols)


GATE_LANE0 = N_GROUPS


def _router_logits(h2, wr_hi, wr_lo, br):
    h_hi = h2.astype(jnp.bfloat16)
    h_lo = (h2 - h_hi.astype(jnp.float32)).astype(jnp.bfloat16)
    return _bdot(h_hi, wr_hi) + _bdot(h_lo, wr_hi) + _bdot(h_hi, wr_lo) + br


def _top_group(logits):
    lane_i = lax.broadcasted_iota(jnp.int32, logits.shape, 1)
    is_g = lane_i < N_GROUPS
    lg = jnp.where(is_g, logits, NEG_INF)
    mg = lg.max(axis=-1, keepdims=True)
    return jnp.where(is_g & (lg == mg), lane_i.astype(jnp.float32), float(LANES)).min(axis=-1, keepdims=True)


def _gates_in_group(logits, group):
    lane_i = lax.broadcasted_iota(jnp.int32, logits.shape, 1)
    lane = lane_i.astype(jnp.float32)
    big = float(LANES)
    is_g = lane_i < N_GROUPS
    lg = jnp.where(is_g, logits, NEG_INF)
    mg = lg.max(axis=-1, keepdims=True)
    own = jnp.where(lane_i == group, logits, 0.0).sum(axis=-1, keepdims=True)
    pg_top = jnp.exp(own - mg) / jnp.where(is_g, jnp.exp(lg - mg), 0.0).sum(axis=-1, keepdims=True)
    lane0 = GATE_LANE0 + group * EXPERTS_PER_GROUP
    sel = (lane_i >= lane0) & (lane_i < lane0 + EXPERTS_PER_GROUP)
    le = jnp.where(sel, logits, NEG_INF)
    me = le.max(axis=-1, keepdims=True)
    ex = jnp.where(sel, jnp.exp(le - me), 0.0)
    pe = ex / ex.sum(axis=-1, keepdims=True)
    p1 = pe.max(axis=-1, keepdims=True)
    i1 = jnp.where(sel & (pe == p1), lane, big).min(axis=-1, keepdims=True)
    rest = sel & (lane != i1)
    pr = jnp.where(rest, pe, -1.0)
    p2 = pr.max(axis=-1, keepdims=True)
    i2 = jnp.where(rest & (pr == p2), lane, big).min(axis=-1, keepdims=True)
    tot = p1 + p2
    within = jnp.where(lane == i1, p1 / tot, 0.0) + jnp.where(lane == i2, p2 / tot, 0.0)
    return pg_top * within


TILE_ROWS = D_MODEL // LANES


def _store_token_tiles(ref, x):
    n = x.shape[0]
    for c in range(TILE_ROWS):
        ref[pl.ds(c, n, stride=TILE_ROWS), :] = x[:, c * LANES:(c + 1) * LANES]


def _load_token_tiles(ref, n):
    return jnp.concatenate([ref[pl.ds(c, n, stride=TILE_ROWS), :] for c in range(TILE_ROWS)], axis=1)


def _outproj_kernel(x_ref, ys_ref, yc_ref, yn_ref, yg_ref, mod_ref, g_ref, w_ref, wr_hi_ref, wr_lo_ref, br_ref,
                    x1_ref, h2t_ref, grp_ref):
    is_ctx = pl.program_id(0) < T_CTX // TM
    y_na = jnp.where(is_ctx, yc_ref[:, 0:NA_WIDTH], yn_ref[...])
    y_g = jnp.where(is_ctx, yc_ref[:, NA_WIDTH:ATTN_WIDTH], yg_ref[...])
    mix = (_bdot(ys_ref[...], w_ref[0:SSM_WIDTH, :])
           + _bdot(y_na, w_ref[SSM_WIDTH:SSM_WIDTH + NA_WIDTH, :])
           + _bdot(y_g, w_ref[SSM_WIDTH + NA_WIDTH:, :]))
    gate1 = mod_ref[0, :, 2 * D_MODEL:3 * D_MODEL]
    shift2 = mod_ref[0, :, 3 * D_MODEL:4 * D_MODEL]
    scale2 = mod_ref[0, :, 4 * D_MODEL:5 * D_MODEL]
    x1 = x_ref[...] + gate1 * mix
    x1_ref[...] = x1
    ms = jnp.mean(x1 * x1, axis=-1, keepdims=True)
    h2 = (x1 * lax.rsqrt(ms + EPS) * g_ref[...]) * (1.0 + scale2) + shift2
    _store_token_tiles(h2t_ref, h2)
    logits = _router_logits(h2, wr_hi_ref[0], wr_lo_ref[0], br_ref[0])
    grp_ref[...] = jnp.broadcast_to(_top_group(logits), (TM, LANES))


def _out_proj(layer, x, y_ssm, y_ctx, y_na_lat, y_g_lat, mods3, norm_g, w_out_bf, wr_hi, wr_lo, b_router):
    n_ctx = T_CTX // TM
    return pl.pallas_call(
        _outproj_kernel,
        grid=(T_ALL // TM,),
        in_specs=[pl.BlockSpec((TM, D_MODEL), lambda i: (i, 0)),
                  pl.BlockSpec((TM, SSM_WIDTH), lambda i: (i, 0)),
                  pl.BlockSpec((TM, ATTN_WIDTH), lambda i: (jnp.minimum(i, n_ctx - 1), 0)),
                  pl.BlockSpec((TM, NA_WIDTH), lambda i: (jnp.maximum(i - n_ctx, 0), 0)),
                  pl.BlockSpec((TM, GQA_WIDTH), lambda i: (jnp.maximum(i - n_ctx, 0), 0)),
                  pl.BlockSpec((1, 1, 6 * D_MODEL), lambda i: (_modset(i), 0, 0)),
                  pl.BlockSpec((1, D_MODEL), lambda i: (0, 0)),
                  pl.BlockSpec((D_MODEL, D_MODEL), lambda i: (0, 0)),
                  pl.BlockSpec((1, D_MODEL, LANES), lambda i: (layer, 0, 0)),
                  pl.BlockSpec((1, D_MODEL, LANES), lambda i: (layer, 0, 0)),
                  pl.BlockSpec((1, 1, LANES), lambda i: (layer, 0, 0))],
        out_specs=[pl.BlockSpec((TM, D_MODEL), lambda i: (i, 0)),
                   pl.BlockSpec((TM * TILE_ROWS, LANES), lambda i: (i, 0)),
                   pl.BlockSpec((TM, LANES), lambda i: (i, 0))],
        out_shape=(jax.ShapeDtypeStruct((T_ALL, D_MODEL), jnp.float32),
                   jax.ShapeDtypeStruct((T_ALL * TILE_ROWS, LANES), jnp.float32),
                   jax.ShapeDtypeStruct((T_ALL, LANES), jnp.float32)),
        compiler_params=_cparams("arbitrary"),
        name="out_proj",
    )(x, y_ssm, y_ctx, y_na_lat, y_g_lat, mods3, norm_g, w_out_bf, wr_hi, wr_lo, b_router)


TMOE = 256
NT_MOE = T_ALL // TMOE + N_GROUPS
MOE_SLOTS = NT_MOE * TMOE


def _moe_dispatch(group_f):
    i32 = jnp.int32
    g = group_f.astype(i32)
    onehot = (g[:, None] == jnp.arange(N_GROUPS, dtype=i32)[None, :]).astype(i32)
    csum = jnp.cumsum(onehot, axis=0)
    counts = csum[-1]
    rank = jnp.sum(onehot * csum, axis=1) - 1
    tiles = (counts + TMOE - 1) // TMOE
    tile_end = jnp.cumsum(tiles)
    tile_start = tile_end - tiles
    slot = jnp.sum(onehot * tile_start[None, :], axis=1) * TMOE + rank
    tok = jnp.full((MOE_SLOTS,), -1, i32).at[slot].set(jnp.arange(T_ALL, dtype=i32))
    pad = tok < 0
    pad_rank = jnp.cumsum(pad.astype(i32)) - 1
    row_of_slot = jnp.where(pad, T_ALL + pad_rank, tok) * TILE_ROWS
    tile_group = jnp.minimum(
        jnp.sum(jnp.arange(NT_MOE, dtype=i32)[:, None] >= tile_end[None, :], axis=1), N_GROUPS - 1).astype(i32)
    return tile_group, row_of_slot


def _moe_kernel(tg_ref, row_ref, h_hbm, wr_hi_ref, wr_lo_ref, br_ref, wg_ref, wu_ref, wd_ref, y_hbm,
                gbuf, ybuf, wg_bf, wu_bf, wd_bf, gsem, ssem):
    i = pl.program_id(0)
    slot = lax.rem(i, 2)
    other = 1 - slot
    last = NT_MOE - 1
    nxt = jnp.minimum(i + 1, last)
    prv = jnp.maximum(i - 1, 0)
    buf_rows = TMOE * TILE_ROWS

    def gather_tok(row, buf_slot, r):
        return pltpu.make_async_copy(h_hbm.at[pl.ds(row, TILE_ROWS)],
                                     gbuf.at[buf_slot, pl.ds(r * TILE_ROWS, TILE_ROWS)], gsem.at[buf_slot])

    def scatter_tok(row, buf_slot, r):
        return pltpu.make_async_copy(ybuf.at[buf_slot, pl.ds(r * TILE_ROWS, TILE_ROWS)],
                                     y_hbm.at[pl.ds(row, TILE_ROWS)], ssem.at[buf_slot])

    def wait_gather(buf_slot):
        pltpu.make_async_copy(h_hbm.at[pl.ds(0, buf_rows)], gbuf.at[buf_slot], gsem.at[buf_slot]).wait()

    def wait_scatter(buf_slot):
        pltpu.make_async_copy(ybuf.at[buf_slot], y_hbm.at[pl.ds(0, buf_rows)], ssem.at[buf_slot]).wait()

    def start_gather_tok(tile, buf_slot, r):
        row = row_ref[tile * TMOE + r]
        gather_tok(jnp.where(row < T_ALL * TILE_ROWS, row, 0), buf_slot, r).start()

    @pl.when(i == 0)
    def _():
        ybuf[1] = jnp.zeros((buf_rows, LANES), jnp.float32)
        for r in range(TMOE):
            start_gather_tok(0, 0, r)

    first_of_group = (i == 0) | (tg_ref[i] != tg_ref[prv])

    @pl.when(first_of_group)
    def _():
        wg_bf[...] = wg_ref[0].astype(jnp.bfloat16)
        wu_bf[...] = wu_ref[0].astype(jnp.bfloat16)
        wd_bf[...] = wd_ref[0].astype(jnp.bfloat16)

    wait_gather(slot)

    h2 = _load_token_tiles(gbuf.at[slot], TMOE)
    h = h2.astype(jnp.bfloat16)
    routing = _gates_in_group(_router_logits(h2, wr_hi_ref[0], wr_lo_ref[0], br_ref[0]), tg_ref[i])
    lane = lax.broadcasted_iota(jnp.int32, (1, LANES), 1)
    lane0 = GATE_LANE0 + tg_ref[i] * EXPERTS_PER_GROUP
    rows_per_chunk = TMOE // EXPERTS_PER_GROUP
    y = None
    for j in range(EXPERTS_PER_GROUP):
        for r in range(j * rows_per_chunk, (j + 1) * rows_per_chunk):
            start_gather_tok(nxt, other, r)
            dst = jnp.where(i > 0, row_ref[prv * TMOE + r], (MOE_SLOTS + r) * TILE_ROWS)
            scatter_tok(dst, other, r).start()
        gate = jnp.where(lane == lane0 + j, routing, 0.0).sum(axis=-1, keepdims=True)
        aj = _bdot(h, wg_bf[j])
        bj = _bdot(h, wu_bf[j])
        act = ((aj / (1.0 + jnp.exp(-aj))) * bj * gate).astype(jnp.bfloat16)
        yj = _bdot(act, wd_bf[j])
        y = yj if y is None else y + yj

    @pl.when(i > 0)
    def _():
        wait_scatter(slot)

    _store_token_tiles(ybuf.at[slot], y)

    @pl.when(i == last)
    def _():
        for r in range(TMOE):
            scatter_tok(row_ref[last * TMOE + r], slot, r).start()
        wait_scatter(other)
        wait_scatter(slot)
        wait_gather(other)


def _moe(layer, h2_tiles, tile_group, row_of_slot, wr_hi, wr_lo, b_router, w_gate, w_up, w_down):
    def wspec(shape):
        return pl.BlockSpec((1, EXPERTS_PER_GROUP) + shape, lambda i, tg, rows: (layer, tg[i], 0, 0))

    def rspec(rows):
        return pl.BlockSpec((1, rows, LANES), lambda i, tg, rows_: (layer, 0, 0))

    grid_spec = pltpu.PrefetchScalarGridSpec(
        num_scalar_prefetch=2,
        grid=(NT_MOE,),
        in_specs=[pl.BlockSpec(memory_space=pl.ANY),
                  rspec(D_MODEL), rspec(D_MODEL), rspec(1),
                  wspec((D_MODEL, EXPERT_HIDDEN)), wspec((D_MODEL, EXPERT_HIDDEN)),
                  wspec((EXPERT_HIDDEN, D_MODEL))],
        out_specs=pl.BlockSpec(memory_space=pl.ANY),
        scratch_shapes=[pltpu.VMEM((2, TMOE * TILE_ROWS, LANES), jnp.float32),
                        pltpu.VMEM((2, TMOE * TILE_ROWS, LANES), jnp.float32),
                        pltpu.VMEM((EXPERTS_PER_GROUP, D_MODEL, EXPERT_HIDDEN), jnp.bfloat16),
                        pltpu.VMEM((EXPERTS_PER_GROUP, D_MODEL, EXPERT_HIDDEN), jnp.bfloat16),
                        pltpu.VMEM((EXPERTS_PER_GROUP, EXPERT_HIDDEN, D_MODEL), jnp.bfloat16),
                        pltpu.SemaphoreType.DMA((2,)),
                        pltpu.SemaphoreType.DMA((2,))])
    return pl.pallas_call(
        _moe_kernel,
        grid_spec=grid_spec,
        out_shape=jax.ShapeDtypeStruct(((MOE_SLOTS + TMOE) * TILE_ROWS, LANES), jnp.float32),
        compiler_params=_cparams("arbitrary"),
        name="moe",
    )(tile_group, row_of_slot, h2_tiles, wr_hi, wr_lo, b_router, w_gate, w_up, w_down)


def _final_kernel(x1_ref, y_ref, mod_ref, g_ref, o_ref):
    x = x1_ref[...] + mod_ref[0, :, 5 * D_MODEL:6 * D_MODEL] * _load_token_tiles(y_ref, TM)
    ms = jnp.mean(x * x, axis=-1, keepdims=True)
    o_ref[...] = x * lax.rsqrt(ms + EPS) * g_ref[...]


def _final_norm(x1, y, mods3, g):
    return pl.pallas_call(
        _final_kernel,
        grid=(T_ALL // TM,),
        in_specs=[pl.BlockSpec((TM, D_MODEL), lambda i: (i, 0)),
                  pl.BlockSpec((TM * TILE_ROWS, LANES), lambda i: (i, 0)),
                  pl.BlockSpec((1, 1, 6 * D_MODEL), lambda i: (_modset(i), 0, 0)),
                  pl.BlockSpec((1, D_MODEL), lambda i: (0, 0))],
        out_specs=pl.BlockSpec((TM, D_MODEL), lambda i: (i, 0)),
        out_shape=jax.ShapeDtypeStruct((T_ALL, D_MODEL), jnp.float32),
        compiler_params=_cparams("arbitrary"),
        name="final_norm",
    )(x1, y, mods3, g)


_QG_ORDER = np.array([0, 3, 1, 4, 2, 5])


def _permute_qg_columns(w_in):
    c1 = SSM_WIDTH + 3 * NA_WIDTH
    qg = w_in[..., c1:c1 + GQA_WIDTH].reshape(*w_in.shape[:-1], GQA_HEADS, HEAD_DIM)
    qg = qg[..., _QG_ORDER, :].reshape(*w_in.shape[:-1], GQA_WIDTH)
    return jnp.concatenate([w_in[..., :c1], qg, w_in[..., c1 + GQA_WIDTH:]], axis=-1)


def _permute_yg_rows(w_out):
    r0 = SSM_WIDTH + NA_WIDTH
    yg = w_out[:, r0:, :].reshape(DEPTH, GQA_HEADS, HEAD_DIM, D_MODEL)[:, _QG_ORDER]
    return jnp.concatenate([w_out[:, :r0, :], yg.reshape(DEPTH, GQA_WIDTH, D_MODEL)], axis=1)


def _rope_tables():
    t = jnp.arange(DEC_SEQ)
    row = (t // GRID_W).astype(jnp.float32)
    col = (t % GRID_W).astype(jnp.float32)
    freqs = ROPE_THETA ** (-jnp.arange(ROPE_F, dtype=jnp.float32) / ROPE_F)
    ang_r = row[:, None] * freqs
    ang_c = col[:, None] * freqs
    ang = jnp.concatenate([ang_r, ang_r, ang_c, ang_c], axis=1)
    cos = jnp.cos(ang)
    sin = jnp.sin(ang)
    sign = jnp.tile(jnp.concatenate([-jnp.ones(ROPE_F), jnp.ones(ROPE_F)]), 2).astype(jnp.float32)
    sin = sin * sign
    cos = jnp.tile(cos, (1, 2))
    sin = jnp.tile(sin, (1, 2))
    cos = jnp.concatenate([jnp.ones((TM, LANES), jnp.float32), cos], axis=0)
    sin = jnp.concatenate([jnp.zeros((TM, LANES), jnp.float32), sin], axis=0)
    return cos, sin


def _pair_layout(cache):
    b, l, h, n, dh = cache.shape
    c = cache.reshape(b, l, h // 2, 2, n, dh).transpose(1, 0, 2, 4, 3, 5)
    return c.reshape(l, b, h // 2, n, 2 * dh).astype(jnp.bfloat16)


def _block_diag_b(bbar):
    bb = bbar.reshape(DEPTH, 2, SSM_GROUP_CH, SSM_GROUPS, SSM_STATE)
    eye = jnp.eye(SSM_GROUPS, dtype=bbar.dtype)
    out = eye[None, None, :, None, :, None] * bb.transpose(0, 1, 3, 2, 4)[:, :, :, :, None, :]
    return out.reshape(DEPTH, 2, SSM_WIDTH, S5_HALF)


def _block_diag_c(c):
    eye = jnp.eye(SSM_GROUPS, dtype=c.dtype)
    out = eye[None, None, :, None, :, None] * c.transpose(0, 1, 2, 4, 3)[:, :, :, :, None, :]
    return out.reshape(DEPTH, 2, S5_HALF, SSM_WIDTH)


def kernel(x_prompt, x_sample, cache_na_k, cache_na_v, cache_gqa_k, cache_gqa_v, state_ssm, c, c_ctx, norm_mix_g, norm_ffn_g, w_ada, b_ada, w_in, ssm_a_re, ssm_a_im, ssm_log_dt, ssm_b_re, ssm_b_im, ssm_c_re, ssm_c_im, ssm_d, w_glu, b_glu, na_rpb, q_norm_g, k_norm_g, w_out, moe_w_group, moe_b_group, moe_w_expert, moe_b_expert, moe_w_gate, moe_w_up, moe_w_down, final_norm_g):
    f32, bf16 = jnp.float32, jnp.bfloat16
    x = jnp.concatenate([x_prompt.reshape(T_CTX, D_MODEL), x_sample.reshape(T_LAT, D_MODEL)], axis=0)

    cvec8 = jnp.concatenate([c_ctx[None, :], c, jnp.zeros((SUBLANES - N_MODSETS, D_MODEL), f32)], axis=0)
    mods_all = _adaln_all(cvec8, w_ada, b_ada)

    w_in_bf = _permute_qg_columns(w_in).astype(bf16)
    w_out_bf = _permute_yg_rows(w_out).astype(bf16)
    w_glu_bf = w_glu.astype(bf16)
    w_router = jnp.concatenate(
        [moe_w_group, moe_w_expert, jnp.zeros((DEPTH, D_MODEL, LANES - N_GROUPS - N_EXPERTS), f32)], axis=-1)
    wr_hi = w_router.astype(bf16)
    wr_lo = (w_router - wr_hi.astype(f32)).astype(bf16)
    b_router = jnp.concatenate(
        [moe_b_group, moe_b_expert, jnp.zeros((DEPTH, LANES - N_GROUPS - N_EXPERTS), f32)], axis=-1)[:, None, :]
    q_gain = jnp.tile(q_norm_g, (1, 2))
    k_gain = jnp.tile(k_norm_g, (1, 2))
    lane = np.arange(LANES)
    gsum = jnp.asarray((lane[:, None] // HEAD_DIM) == (lane[None, :] // HEAD_DIM), bf16)
    rope_cos, rope_sin = _rope_tables()

    abar_re, abar_im, bbar_re, bbar_im = _s5_prep(ssm_a_re, ssm_a_im, ssm_log_dt, ssm_b_re, ssm_b_im)
    abar = jnp.concatenate([abar_re, abar_im], axis=-1).reshape(DEPTH, 2, 1, S5_CPLX)
    bblk = jnp.concatenate([_block_diag_b(bbar_re), _block_diag_b(bbar_im)], axis=-1).astype(bf16)
    cblk = jnp.concatenate([_block_diag_c(ssm_c_re), -_block_diag_c(ssm_c_im)], axis=2).astype(bf16)
    h0 = state_ssm.transpose(1, 2, 0, 5, 3, 4).reshape(DEPTH, 2, DEC_BATCH, S5_CPLX)

    ck_na, cv_na = _pair_layout(cache_na_k), _pair_layout(cache_na_v)
    ck_g, cv_g = _pair_layout(cache_gqa_k)[:, :, 0], _pair_layout(cache_gqa_v)[:, :, 0]
    na_bias = _na_bias_table(na_rpb)

    kv32_l, fin_l = [], []
    moe_residual = None
    for l in range(DEPTH):
        mods3 = mods_all[l, :N_MODSETS].reshape(N_MODSETS, 1, 6 * D_MODEL)
        u, qkv, kv32, x = _in_proj(x, moe_residual, mods3, norm_mix_g[l][None], w_in_bf[l], q_gain[l][None],
                                   k_gain[l][None], gsum, rope_cos, rope_sin)
        kv32_l.append(kv32)

        u_tm = jnp.pad(u.reshape(S5_SEQS, S5_CHUNK, SSM_WIDTH).transpose(1, 0, 2),
                       ((0, 0), (0, S5_ROWS - S5_SEQS), (0, 0)))
        y_dir, powers, finals = _s5_scan(u_tm.reshape(S5_CHUNK * S5_ROWS, SSM_WIDTH), bblk[l], abar[l], cblk[l])
        fin_l.append(finals[:, :BATCH])
        y_tm = _s5_finish(u_tm, y_dir.reshape(2, S5_CHUNK, S5_ROWS, SSM_WIDTH), powers, finals, h0[l], cblk[l],
                          ssm_d[l][None], w_glu_bf[l], b_glu[l][None])
        y_ssm = y_tm[:, :S5_SEQS].transpose(1, 0, 2).reshape(T_ALL, SSM_WIDTH)

        y_ctx = _attn_ctx(qkv)
        y_g_lat = _gqa_lat(qkv, ck_g[l], cv_g[l])
        y_na_lat = _na_lat(l, qkv, ck_na[l], cv_na[l], na_bias)

        x1, h2_tiles, group = _out_proj(l, x, y_ssm, y_ctx, y_na_lat, y_g_lat, mods3, norm_ffn_g[l][None],
                                        w_out_bf[l], wr_hi, wr_lo, b_router)
        tile_group, row_of_slot = _moe_dispatch(group[:, 0])
        y_moe = _moe(l, h2_tiles, tile_group, row_of_slot, wr_hi, wr_lo, b_router,
                     moe_w_gate, moe_w_up, moe_w_down)
        x, moe_residual = x1, (y_moe, mods3)

    y = _final_norm(x, moe_residual[0], moe_residual[1], final_norm_g[None])
    y_prompt = y[:T_CTX].reshape(BATCH, SEQ, D_MODEL)
    y_sample = y[T_CTX:].reshape(DEC_BATCH, DEC_SEQ, D_MODEL)

    kv32 = jnp.stack(kv32_l, axis=0).reshape(DEPTH, BATCH, SEQ, KV32_WIDTH)

    def heads(t, n_heads):
        return t.reshape(DEPTH, BATCH, SEQ, n_heads, HEAD_DIM).transpose(1, 0, 3, 2, 4)

    new_na_k = heads(kv32[..., 0:NA_WIDTH], NA_HEADS)
    new_na_v = heads(kv32[..., NA_WIDTH:2 * NA_WIDTH], NA_HEADS)
    new_gqa_k = heads(kv32[..., 2 * NA_WIDTH:2 * NA_WIDTH + KV_WIDTH], GQA_KV_HEADS)
    new_gqa_v = heads(kv32[..., 2 * NA_WIDTH + KV_WIDTH:], GQA_KV_HEADS)
    fin = jnp.stack(fin_l, axis=0).reshape(DEPTH, 2, BATCH, 2, SSM_GROUPS, SSM_STATE)
    new_state = fin.transpose(2, 0, 1, 4, 5, 3)
    return (y_prompt, y_sample, new_na_k, new_na_v, new_gqa_k, new_gqa_v, new_state)
```

```python
import functools
import math

import jax
import jax.numpy as jnp
import numpy as np
from jax import lax
from jax.experimental import pallas as pl
from jax.experimental.pallas import tpu as pltpu

D_MODEL = 1024
BATCH = 16
SEQ = 256
DEPTH = 4
DEC_BATCH = 2
DEC_SEQ = 4096
PAST_LEN = 512
GRID_W = 64
GRID_H = DEC_SEQ // GRID_W
HEAD_DIM = 64
SSM_WIDTH = 256
SSM_GROUP_CH = 16
SSM_GROUPS = 16
SSM_STATE = 64
NA_HEADS = 6
NA_WIDTH = 384
NA_KH = 8
NA_KW = 16
GQA_HEADS = 6
GQA_KV_HEADS = 2
GQA_REP = 3
GQA_WIDTH = 384
KV_WIDTH = 128
IN_WIDTH = 2048
ROPE_THETA = 10000.0
ROPE_F = 16
N_GROUPS = 4
EXPERTS_PER_GROUP = 4
N_EXPERTS = 16
EXPERT_HIDDEN = 256
EPS = 1e-6
NEG_INF = -1e30

T_CTX = BATCH * SEQ
T_LAT = DEC_BATCH * DEC_SEQ
T_ALL = T_CTX + T_LAT
N_MODSETS = 1 + DEC_BATCH
LANES = 128
SUBLANES = 8
PAIR = 2 * HEAD_DIM
ATTN_SCALE = HEAD_DIM ** -0.5

S5_CHUNK = 256
S5_SEQS = T_ALL // S5_CHUNK
S5_ROWS = 56
S5_POW_ROW = 48
S5_LAT0 = BATCH
S5_NLAT = S5_SEQS - BATCH
S5_TB = 16
S5_NBLK = S5_CHUNK // S5_TB
S5_CPLX = 2 * SSM_GROUPS * SSM_STATE
S5_HALF = SSM_GROUPS * SSM_STATE

TM = 512
VMEM_LIMIT = 56 * 1024 * 1024


def _cparams(*sem):
    return pltpu.CompilerParams(dimension_semantics=sem, vmem_limit_bytes=VMEM_LIMIT)


def _modset(i):
    return jnp.where(i < T_CTX // TM, 0, 1 + (i - T_CTX // TM) // (DEC_SEQ // TM))


def _bdot(a, b):
    return jnp.dot(a, b, preferred_element_type=jnp.float32)


def _bdot_t(a, b):
    return lax.dot_general(a, b, (((1,), (1,)), ((), ())), preferred_element_type=jnp.float32)


def _ada_kernel(c_ref, w_ref, b_ref, o_ref):
    c = c_ref[...]
    s = c / (1.0 + jnp.exp(-c))
    o_ref[0] = _bdot(s.astype(jnp.bfloat16), w_ref[0].astype(jnp.bfloat16)) + b_ref[0]


def _adaln_all(cvec8, w_ada, b_ada):
    tn = 1024
    return pl.pallas_call(
        _ada_kernel,
        grid=(DEPTH, 6 * D_MODEL // tn),
        in_specs=[pl.BlockSpec((SUBLANES, D_MODEL), lambda l, j: (0, 0)),
                  pl.BlockSpec((1, D_MODEL, tn), lambda l, j: (l, 0, j)),
                  pl.BlockSpec((1, 1, tn), lambda l, j: (l, 0, j))],
        out_specs=pl.BlockSpec((1, SUBLANES, tn), lambda l, j: (l, 0, j)),
        out_shape=jax.ShapeDtypeStruct((DEPTH, SUBLANES, 6 * D_MODEL), jnp.float32),
        compiler_params=_cparams("arbitrary", "arbitrary"),
        name="adaln",
    )(cvec8, w_ada, b_ada.reshape(DEPTH, 1, 6 * D_MODEL))


def _s5prep_kernel(ar_ref, ai_ref, ldt_ref, br_ref, bi_ref, abar_re, abar_im, bbar_re, bbar_im):
    ar, ai = ar_ref[0], ai_ref[0]
    dt = jnp.exp(ldt_ref[0])
    mag = jnp.exp(ar * dt)
    are = mag * jnp.cos(ai * dt)
    aim = mag * jnp.sin(ai * dt)
    zr, zi = are - 1.0, aim
    den = ar * ar + ai * ai
    cre = (zr * ar + zi * ai) / den
    cim = (zi * ar - zr * ai) / den
    abar_re[0] = are
    abar_im[0] = aim
    br, bi = br_ref[0], bi_ref[0]
    bbar_re[0] = cre * br - cim * bi
    bbar_im[0] = cre * bi + cim * br


def _s5_prep(a_re, a_im, log_dt, b_re, b_im):
    n = DEPTH * 2
    ar = a_re.reshape(n, 1, S5_HALF)
    ai = a_im.reshape(n, 1, S5_HALF)
    ldt = jnp.broadcast_to(log_dt[..., None], (DEPTH, 2, SSM_GROUPS, SSM_STATE)).reshape(n, 1, S5_HALF)
    br = b_re.transpose(0, 1, 4, 2, 3).reshape(n, SSM_GROUP_CH, S5_HALF)
    bi = b_im.transpose(0, 1, 4, 2, 3).reshape(n, SSM_GROUP_CH, S5_HALF)
    vec = jax.ShapeDtypeStruct((n, 1, S5_HALF), jnp.float32)
    mat = jax.ShapeDtypeStruct((n, SSM_GROUP_CH, S5_HALF), jnp.float32)
    vspec = pl.BlockSpec((1, 1, S5_HALF), lambda j: (j, 0, 0))
    mspec = pl.BlockSpec((1, SSM_GROUP_CH, S5_HALF), lambda j: (j, 0, 0))
    return pl.pallas_call(
        _s5prep_kernel,
        grid=(n,),
        in_specs=[vspec, vspec, vspec, mspec, mspec],
        out_specs=[vspec, vspec, mspec, mspec],
        out_shape=(vec, vec, mat, mat),
        compiler_params=_cparams("arbitrary"),
        name="s5prep",
    )(ar, ai, ldt, br, bi)


def _head_rms(blk, gsum, gain):
    sq = blk * blk
    hi = sq.astype(jnp.bfloat16)
    lo = (sq - hi.astype(jnp.float32)).astype(jnp.bfloat16)
    ms = (_bdot(hi, gsum) + _bdot(lo, gsum)) * (1.0 / HEAD_DIM)
    return blk * lax.rsqrt(ms + EPS) * gain


def _rope(blk, cos, sin_signed, first_half):
    partner = jnp.where(first_half, pltpu.roll(blk, LANES - ROPE_F, 1), pltpu.roll(blk, ROPE_F, 1))
    return blk * cos + partner * sin_signed


def _inproj_kernel(has_moe_residual, *refs):
    if has_moe_residual:
        (x_ref, y_ref, pmod_ref, mod_ref, g_ref, w_ref, qg_ref, kg_ref, gsum_ref, cos_ref, sin_ref,
         u_ref, qkv_ref, kv32_ref, xo_ref) = refs
        x = x_ref[...] + pmod_ref[0, :, 5 * D_MODEL:6 * D_MODEL] * _load_token_tiles(y_ref, TM)
        xo_ref[...] = x
    else:
        (x_ref, mod_ref, g_ref, w_ref, qg_ref, kg_ref, gsum_ref, cos_ref, sin_ref,
         u_ref, qkv_ref, kv32_ref) = refs
        x = x_ref[...]
    i = pl.program_id(0)
    shift = mod_ref[0, :, 0:D_MODEL]
    scale = mod_ref[0, :, D_MODEL:2 * D_MODEL]
    ms = jnp.mean(x * x, axis=-1, keepdims=True)
    h = (x * lax.rsqrt(ms + EPS) * g_ref[...]) * (1.0 + scale) + shift
    proj = _bdot(h.astype(jnp.bfloat16), w_ref[...])
    u_ref[...] = proj[:, 0:SSM_WIDTH]

    c0 = SSM_WIDTH
    qn = proj[:, c0:c0 + NA_WIDTH] * ATTN_SCALE
    kn = proj[:, c0 + NA_WIDTH:c0 + 2 * NA_WIDTH]
    vn = proj[:, c0 + 2 * NA_WIDTH:c0 + 3 * NA_WIDTH]
    c1 = c0 + 3 * NA_WIDTH
    vg = proj[:, c1 + GQA_WIDTH + KV_WIDTH:c1 + GQA_WIDTH + 2 * KV_WIDTH]

    gsum = gsum_ref[...]
    cos, sin = cos_ref[...], sin_ref[...]
    lane = lax.broadcasted_iota(jnp.int32, (1, LANES), 1)
    first_half = (lane & (2 * ROPE_F - 1)) < ROPE_F
    qg_blocks = []
    for p in range(GQA_WIDTH // PAIR):
        blk = proj[:, c1 + p * PAIR:c1 + (p + 1) * PAIR]
        blk = _rope(_head_rms(blk, gsum, qg_ref[...]), cos, sin, first_half)
        qg_blocks.append(blk * ATTN_SCALE)
    kg = proj[:, c1 + GQA_WIDTH:c1 + GQA_WIDTH + KV_WIDTH]
    kg = _rope(_head_rms(kg, gsum, kg_ref[...]), cos, sin, first_half)

    o = 0
    vg_ones = [jnp.where(lane < HEAD_DIM, vg, 1.0), jnp.where(lane >= HEAD_DIM, vg, 1.0)]
    for piece in [qn, kn, vn] + qg_blocks + [kg] + vg_ones:
        w = piece.shape[1]
        qkv_ref[:, o:o + w] = piece.astype(jnp.bfloat16)
        o += w

    @pl.when(i < T_CTX // TM)
    def _():
        kv32_ref[:, 0:NA_WIDTH] = kn
        kv32_ref[:, NA_WIDTH:2 * NA_WIDTH] = vn
        kv32_ref[:, 2 * NA_WIDTH:2 * NA_WIDTH + KV_WIDTH] = kg
        kv32_ref[:, 2 * NA_WIDTH + KV_WIDTH:2 * NA_WIDTH + 2 * KV_WIDTH] = vg


QKV_WIDTH = 3 * NA_WIDTH + GQA_WIDTH + 3 * KV_WIDTH
KV32_WIDTH = 2 * NA_WIDTH + 2 * KV_WIDTH
QN_BLK, KN_BLK, VN_BLK, QG_BLK, KG_BLK, VG1_BLK = 0, 3, 6, 9, 12, 13


def _rope_block(i):
    n_ctx = T_CTX // TM
    return jnp.where(i < n_ctx, 0, 1 + (i - n_ctx) % (DEC_SEQ // TM))


def _in_proj(x, moe_residual, mods3, norm_g, w_in_bf, q_gain, k_gain, gsum, rope_cos, rope_sin):
    n_ctx = T_CTX // TM
    row_spec = pl.BlockSpec((TM, D_MODEL), lambda i: (i, 0))
    mod_spec = pl.BlockSpec((1, 1, 6 * D_MODEL), lambda i: (_modset(i), 0, 0))
    args, in_specs = [x], [row_spec]
    out_specs = [pl.BlockSpec((TM, SSM_WIDTH), lambda i: (i, 0)),
                 pl.BlockSpec((TM, QKV_WIDTH), lambda i: (i, 0)),
                 pl.BlockSpec((TM, KV32_WIDTH), lambda i: (jnp.minimum(i, n_ctx - 1), 0))]
    out_shape = [jax.ShapeDtypeStruct((T_ALL, SSM_WIDTH), jnp.float32),
                 jax.ShapeDtypeStruct((T_ALL, QKV_WIDTH), jnp.bfloat16),
                 jax.ShapeDtypeStruct((T_CTX, KV32_WIDTH), jnp.float32)]
    if moe_residual is not None:
        args += list(moe_residual)
        in_specs += [pl.BlockSpec((TM * TILE_ROWS, LANES), lambda i: (i, 0)), mod_spec]
        out_specs.append(row_spec)
        out_shape.append(jax.ShapeDtypeStruct((T_ALL, D_MODEL), jnp.float32))
    args += [mods3, norm_g, w_in_bf, q_gain, k_gain, gsum, rope_cos, rope_sin]
    in_specs += [mod_spec,
                 pl.BlockSpec((1, D_MODEL), lambda i: (0, 0)),
                 pl.BlockSpec((D_MODEL, IN_WIDTH), lambda i: (0, 0)),
                 pl.BlockSpec((1, LANES), lambda i: (0, 0)),
                 pl.BlockSpec((1, LANES), lambda i: (0, 0)),
                 pl.BlockSpec((LANES, LANES), lambda i: (0, 0)),
                 pl.BlockSpec((TM, LANES), lambda i: (_rope_block(i), 0)),
                 pl.BlockSpec((TM, LANES), lambda i: (_rope_block(i), 0))]
    outs = pl.pallas_call(
        functools.partial(_inproj_kernel, moe_residual is not None),
        grid=(T_ALL // TM,),
        in_specs=in_specs,
        out_specs=out_specs,
        out_shape=out_shape,
        compiler_params=_cparams("arbitrary"),
        name="in_proj",
    )(*args)
    return outs if moe_residual is not None else (*outs, x)


def _s5scan_kernel(u_ref, b_ref, a_ref, c_ref, y_ref, p_ref, fin_ref, hbuf):
    d = pl.program_id(0)
    k = pl.program_id(1)
    rows = S5_TB * S5_ROWS
    a_re = jnp.broadcast_to(a_ref[0, :, 0:S5_HALF], (SUBLANES, S5_HALF))
    a_im = jnp.broadcast_to(a_ref[0, :, S5_HALF:S5_CPLX], (SUBLANES, S5_HALF))
    ub = u_ref[...].astype(jnp.bfloat16)

    def run(forward):
        data0 = S5_ROWS if forward else 0
        prev0 = 0 if forward else rows
        last0 = rows if forward else 0

        @pl.when(k == 0)
        def _():
            r = lax.broadcasted_iota(jnp.int32, (S5_ROWS, S5_CPLX), 0)
            l = lax.broadcasted_iota(jnp.int32, (S5_ROWS, S5_CPLX), 1)
            hbuf[prev0:prev0 + S5_ROWS, :] = jnp.where((r == S5_POW_ROW) & (l < S5_HALF), 1.0, 0.0)

        @pl.when(k > 0)
        def _():
            hbuf[prev0:prev0 + S5_ROWS, :] = hbuf[last0:last0 + S5_ROWS, :]

        hbuf[data0:data0 + rows, :] = _bdot(ub, b_ref[0])

        def step(i, carry):
            t = i if forward else S5_TB - 1 - i
            cur = pl.multiple_of(data0 + t * S5_ROWS, SUBLANES)
            prv = pl.multiple_of(cur - S5_ROWS if forward else cur + S5_ROWS, SUBLANES)
            for st in range(S5_ROWS // SUBLANES):
                rc = pl.ds(cur + st * SUBLANES, SUBLANES)
                rp = pl.ds(prv + st * SUBLANES, SUBLANES)
                hr = hbuf[rp, 0:S5_HALF]
                hi = hbuf[rp, S5_HALF:S5_CPLX]
                nr = a_re * hr - a_im * hi + hbuf[rc, 0:S5_HALF]
                ni = a_re * hi + a_im * hr + hbuf[rc, S5_HALF:S5_CPLX]
                hbuf[rc, 0:S5_HALF] = nr
                hbuf[rc, S5_HALF:S5_CPLX] = ni
            return carry

        lax.fori_loop(0, S5_TB, step, 0, unroll=True)
        y_ref[0] = _bdot(hbuf[data0:data0 + rows, :].astype(jnp.bfloat16), c_ref[0])
        for t in range(S5_TB):
            r0 = data0 + t * S5_ROWS + S5_POW_ROW
            p_ref[0, t:t + 1, :] = hbuf[r0:r0 + 1, :]

        @pl.when(k == S5_NBLK - 1)
        def _():
            fin_ref[0] = hbuf[last0:last0 + S5_ROWS, :]

    @pl.when(d == 0)
    def _():
        run(True)

    @pl.when(d == 1)
    def _():
        run(False)


def _s5_scan(u_tm, bblk, abar, cblk):
    rows = S5_TB * S5_ROWS

    def tblk(d, k):
        return jnp.where(d == 0, k, S5_NBLK - 1 - k)

    return pl.pallas_call(
        _s5scan_kernel,
        grid=(2, S5_NBLK),
        in_specs=[pl.BlockSpec((rows, SSM_WIDTH), lambda d, k: (tblk(d, k), 0)),
                  pl.BlockSpec((1, SSM_WIDTH, S5_CPLX), lambda d, k: (d, 0, 0)),
                  pl.BlockSpec((1, 1, S5_CPLX), lambda d, k: (d, 0, 0)),
                  pl.BlockSpec((1, S5_CPLX, SSM_WIDTH), lambda d, k: (d, 0, 0))],
        out_specs=[pl.BlockSpec((1, rows, SSM_WIDTH), lambda d, k: (d, tblk(d, k), 0)),
                   pl.BlockSpec((1, S5_TB, S5_CPLX), lambda d, k: (d, tblk(d, k), 0)),
                   pl.BlockSpec((1, S5_ROWS, S5_CPLX), lambda d, k: (d, 0, 0))],
        out_shape=(jax.ShapeDtypeStruct((2, S5_CHUNK * S5_ROWS, SSM_WIDTH), jnp.float32),
                   jax.ShapeDtypeStruct((2, S5_CHUNK, S5_CPLX), jnp.float32),
                   jax.ShapeDtypeStruct((2, S5_ROWS, S5_CPLX), jnp.float32)),
        scratch_shapes=[pltpu.VMEM((rows + S5_ROWS, S5_CPLX), jnp.float32)],
        compiler_params=_cparams("arbitrary", "arbitrary"),
        name="s5_scan",
    )(u_tm, bblk, abar, cblk)


def _cmul(pr, pi, cr, ci):
    return pr * cr - pi * ci, pr * ci + pi * cr


def _s5fin_kernel(u_ref, yf_ref, yb_ref, p_ref, fin_ref, h0_ref, c_ref, d_ref, wg_ref, bg_ref,
                  o_ref, carry, hc, ybuf):
    k = pl.program_id(0)
    nchunk = DEC_SEQ // S5_CHUNK

    @pl.when(k == 0)
    def _():
        for d in range(2):
            t_full = S5_CHUNK - 1 if d == 0 else 0
            pr = p_ref[d, t_full:t_full + 1, 0:S5_HALF]
            pi = p_ref[d, t_full:t_full + 1, S5_HALF:S5_CPLX]
            for b in range(DEC_BATCH):
                cr = h0_ref[d, b:b + 1, 0:S5_HALF]
                ci = h0_ref[d, b:b + 1, S5_HALF:S5_CPLX]
                order = range(nchunk) if d == 0 else range(nchunk - 1, -1, -1)
                for c in order:
                    row = b * nchunk + c
                    carry[d, row:row + 1, 0:S5_HALF] = cr
                    carry[d, row:row + 1, S5_HALF:S5_CPLX] = ci
                    fr = fin_ref[d, S5_LAT0 + row:S5_LAT0 + row + 1, 0:S5_HALF]
                    fi = fin_ref[d, S5_LAT0 + row:S5_LAT0 + row + 1, S5_HALF:S5_CPLX]
                    mr, mi = _cmul(pr, pi, cr, ci)
                    cr, ci = fr + mr, fi + mi

    ybuf[...] = yf_ref[0] + yb_ref[0] + u_ref[...] * d_ref[...]
    for d in range(2):
        cr = carry[d, :, 0:S5_HALF]
        ci = carry[d, :, S5_HALF:S5_CPLX]
        for t in range(S5_TB):
            pr = p_ref[d, pl.ds(k * S5_TB + t, 1), 0:S5_HALF]
            pi = p_ref[d, pl.ds(k * S5_TB + t, 1), S5_HALF:S5_CPLX]
            mr, mi = _cmul(pr, pi, cr, ci)
            hc[t * S5_NLAT:(t + 1) * S5_NLAT, 0:S5_HALF] = mr.astype(jnp.bfloat16)
            hc[t * S5_NLAT:(t + 1) * S5_NLAT, S5_HALF:S5_CPLX] = mi.astype(jnp.bfloat16)
        corr = _bdot(hc[...], c_ref[d])
        ybuf[:, S5_LAT0:S5_LAT0 + S5_NLAT, :] += corr.reshape(S5_TB, S5_NLAT, SSM_WIDTH)

    y = ybuf[...].reshape(S5_TB * S5_ROWS, SSM_WIDTH)
    yg = 0.5 * y * (1.0 + jnp.tanh(math.sqrt(2.0 / math.pi) * (y + 0.044715 * (y * y * y))))
    z = _bdot(yg.astype(jnp.bfloat16), wg_ref[...]) + bg_ref[...]
    out = yg / (1.0 + jnp.exp(-z))
    o_ref[...] = out.reshape(S5_TB, S5_ROWS, SSM_WIDTH).astype(jnp.bfloat16)


def _s5_finish(u_tm3, y_dir4, powers, finals, h0, cblk, ssm_d, w_glu_bf, b_glu):
    blk3 = (S5_TB, S5_ROWS, SSM_WIDTH)
    return pl.pallas_call(
        _s5fin_kernel,
        grid=(S5_NBLK,),
        in_specs=[pl.BlockSpec(blk3, lambda k: (k, 0, 0)),
                  pl.BlockSpec((1,) + blk3, lambda k: (0, k, 0, 0)),
                  pl.BlockSpec((1,) + blk3, lambda k: (1, k, 0, 0)),
                  pl.BlockSpec((2, S5_CHUNK, S5_CPLX), lambda k: (0, 0, 0)),
                  pl.BlockSpec((2, S5_ROWS, S5_CPLX), lambda k: (0, 0, 0)),
                  pl.BlockSpec((2, DEC_BATCH, S5_CPLX), lambda k: (0, 0, 0)),
                  pl.BlockSpec((2, S5_CPLX, SSM_WIDTH), lambda k: (0, 0, 0)),
                  pl.BlockSpec((1, SSM_WIDTH), lambda k: (0, 0)),
                  pl.BlockSpec((SSM_WIDTH, SSM_WIDTH), lambda k: (0, 0)),
                  pl.BlockSpec((1, SSM_WIDTH), lambda k: (0, 0))],
        out_specs=pl.BlockSpec(blk3, lambda k: (k, 0, 0)),
        out_shape=jax.ShapeDtypeStruct((S5_CHUNK, S5_ROWS, SSM_WIDTH), jnp.bfloat16),
        scratch_shapes=[pltpu.VMEM((2, S5_NLAT, S5_CPLX), jnp.float32),
                        pltpu.VMEM((S5_TB * S5_NLAT, S5_CPLX), jnp.bfloat16),
                        pltpu.VMEM(blk3, jnp.float32)],
        compiler_params=_cparams("arbitrary"),
        name="s5_finish",
    )(u_tm3, y_dir4, y_dir4, powers, finals, h0, cblk, ssm_d, w_glu_bf, b_glu)


def _pair_attention(q, kvs, biases):
    lane = lax.broadcasted_iota(jnp.int32, (1, PAIR), 1)
    outs = []
    for a in range(2):
        sel = (lane < HEAD_DIM) if a == 0 else (lane >= HEAD_DIM)
        qa = jnp.where(sel, q, jnp.zeros_like(q))
        ss = []
        for (kk, _), bias in zip(kvs, biases):
            s = _bdot_t(qa, kk)
            if bias is not None:
                s = s + bias[a]
            ss.append(s)
        m = ss[0].max(axis=-1, keepdims=True)
        for s in ss[1:]:
            m = jnp.maximum(m, s.max(axis=-1, keepdims=True))
        den = None
        acc = None
        for s, (_, vv) in zip(ss, kvs):
            p = jnp.exp(s - m)
            l = p.sum(axis=-1, keepdims=True)
            o = _bdot(p.astype(jnp.bfloat16), vv)
            den = l if den is None else den + l
            acc = o if acc is None else acc + o
        outs.append(acc / den)
    return jnp.where(lane < HEAD_DIM, outs[0], outs[1])


def _pair_attention_vones(q, kvs):
    lane = lax.broadcasted_iota(jnp.int32, (1, PAIR), 1)
    outs = []
    for a in range(2):
        sel = (lane < HEAD_DIM) if a == 0 else (lane >= HEAD_DIM)
        qa = jnp.where(sel, q, jnp.zeros_like(q))
        ss = [_bdot_t(qa, kv[0]) for kv in kvs]
        m = ss[0].max(axis=-1, keepdims=True)
        for s in ss[1:]:
            m = jnp.maximum(m, s.max(axis=-1, keepdims=True))
        acc = None
        for s, kv in zip(ss, kvs):
            o = _bdot(jnp.exp((s - m).astype(jnp.bfloat16)), kv[1 + a])
            acc = o if acc is None else acc + o
        den = pltpu.roll(acc, HEAD_DIM, 1)
        outs.append(acc / den)
    return jnp.where(lane < HEAD_DIM, outs[0], outs[1])


ATTN_WIDTH = NA_WIDTH + GQA_WIDTH
NPAIR = NA_WIDTH // PAIR


def _attn_ctx_kernel(qkv_ref, o_ref):
    def blk(j):
        return qkv_ref[:, j * PAIR:(j + 1) * PAIR]

    for p in range(NPAIR):
        o = _pair_attention(blk(QN_BLK + p), [(blk(KN_BLK + p), blk(VN_BLK + p))], [None])
        o_ref[:, p * PAIR:(p + 1) * PAIR] = o.astype(jnp.bfloat16)
    for p in range(NPAIR):
        o = _pair_attention_vones(blk(QG_BLK + p), [(blk(KG_BLK), blk(VG1_BLK), blk(VG1_BLK + 1))])
        o_ref[:, (NPAIR + p) * PAIR:(NPAIR + p + 1) * PAIR] = o.astype(jnp.bfloat16)


def _attn_ctx(qkv):
    return pl.pallas_call(
        _attn_ctx_kernel,
        grid=(BATCH,),
        in_specs=[pl.BlockSpec((SEQ, QKV_WIDTH), lambda b: (b, 0))],
        out_specs=pl.BlockSpec((SEQ, ATTN_WIDTH), lambda b: (b, 0)),
        out_shape=jax.ShapeDtypeStruct((T_CTX, ATTN_WIDTH), jnp.bfloat16),
        compiler_params=_cparams("arbitrary"),
        name="attn_ctx",
    )(qkv)


TQ = 256


def _gqa_lat_kernel(q_ref, k_ref, v0_ref, v1_ref, ck_ref, cv_ref, o_ref):
    kvs = [(ck_ref[0], cv_ref[0, 0], cv_ref[0, 1]), (k_ref[...], v0_ref[...], v1_ref[...])]
    o_ref[...] = _pair_attention_vones(q_ref[...], kvs).astype(jnp.bfloat16)


def _gqa_lat(qkv, ck, cv_ones):
    nq = DEC_SEQ // TQ
    ctx_blocks = T_CTX // DEC_SEQ
    return pl.pallas_call(
        _gqa_lat_kernel,
        grid=(DEC_BATCH, NPAIR, nq),
        in_specs=[pl.BlockSpec((TQ, PAIR), lambda b, p, i: (T_CTX // TQ + b * nq + i, QG_BLK + p)),
                  pl.BlockSpec((DEC_SEQ, PAIR), lambda b, p, i: (ctx_blocks + b, KG_BLK)),
                  pl.BlockSpec((DEC_SEQ, PAIR), lambda b, p, i: (ctx_blocks + b, VG1_BLK)),
                  pl.BlockSpec((DEC_SEQ, PAIR), lambda b, p, i: (ctx_blocks + b, VG1_BLK + 1)),
                  pl.BlockSpec((1, PAST_LEN, PAIR), lambda b, p, i: (b, 0, 0)),
                  pl.BlockSpec((1, 2, PAST_LEN, PAIR), lambda b, p, i: (b, 0, 0, 0))],
        out_specs=pl.BlockSpec((TQ, PAIR), lambda b, p, i: (b * nq + i, p)),
        out_shape=jax.ShapeDtypeStruct((T_LAT, GQA_WIDTH), jnp.bfloat16),
        compiler_params=_cparams("arbitrary", "arbitrary", "arbitrary"),
        name="gqa_lat",
    )(qkv, qkv, qkv, qkv, ck, cv_ones)


NA_BLK = 4
NA_NBLK = GRID_H // NA_BLK
NA_SLAB_ROWS = NA_KH + NA_BLK - 1
NA_SLAB = NA_SLAB_ROWS * GRID_W
NA_QROWS = NA_BLK * GRID_W
NA_BLOCK_TYPES = (0, 2, NA_NBLK - 1)


def _na_slab_start(j):
    return jnp.clip(j * NA_BLK - NA_KH // 2, 0, GRID_H - NA_SLAB_ROWS)


def _na_lat_kernel(q_ref, k_ref, v_ref, ck_ref, cv_ref, bias_ref, o_ref):
    ck, cv = ck_ref[0, 0], cv_ref[0, 0]
    lane = lax.broadcasted_iota(jnp.int32, (1, PAIR), 1)

    def block(j, carry):
        q0 = pl.multiple_of(j * NA_QROWS, NA_QROWS)
        k0 = pl.multiple_of(_na_slab_start(j) * GRID_W, GRID_W)
        btype = jnp.where(j == 0, 0, jnp.where(j == NA_NBLK - 1, 2, 1))
        q = q_ref[pl.ds(q0, NA_QROWS), :]
        lhs = jnp.concatenate([jnp.where(lane < HEAD_DIM, q, jnp.zeros_like(q)),
                               jnp.where(lane >= HEAD_DIM, q, jnp.zeros_like(q))], axis=0)
        ks = k_ref[pl.ds(k0, NA_SLAB), :]
        vs = v_ref[pl.ds(k0, NA_SLAB), :]
        s_lat = _bdot_t(lhs, ks) + bias_ref[0, 0, btype]
        s_ctx = _bdot_t(lhs, ck)
        m = jnp.maximum(s_lat.max(axis=-1, keepdims=True), s_ctx.max(axis=-1, keepdims=True))
        p_lat = jnp.exp(s_lat - m)
        p_ctx = jnp.exp(s_ctx - m)
        den = p_lat.sum(axis=-1, keepdims=True) + p_ctx.sum(axis=-1, keepdims=True)
        o = (_bdot(p_lat.astype(jnp.bfloat16), vs) + _bdot(p_ctx.astype(jnp.bfloat16), cv)) / den
        out = jnp.where(lane < HEAD_DIM, o[0:NA_QROWS], o[NA_QROWS:2 * NA_QROWS])
        o_ref[pl.ds(q0, NA_QROWS), :] = out.astype(jnp.bfloat16)
        return carry

    lax.fori_loop(0, NA_NBLK, block, 0)


def _na_lat(layer, qkv, ck, cv, bias):
    ctx_blocks = T_CTX // DEC_SEQ
    return pl.pallas_call(
        _na_lat_kernel,
        grid=(DEC_BATCH, NPAIR),
        in_specs=[pl.BlockSpec((DEC_SEQ, PAIR), lambda b, p: (ctx_blocks + b, QN_BLK + p)),
                  pl.BlockSpec((DEC_SEQ, PAIR), lambda b, p: (ctx_blocks + b, KN_BLK + p)),
                  pl.BlockSpec((DEC_SEQ, PAIR), lambda b, p: (ctx_blocks + b, VN_BLK + p)),
                  pl.BlockSpec((1, 1, PAST_LEN, PAIR), lambda b, p: (b, p, 0, 0)),
                  pl.BlockSpec((1, 1, PAST_LEN, PAIR), lambda b, p: (b, p, 0, 0)),
                  pl.BlockSpec((1, 1, len(NA_BLOCK_TYPES), 2 * NA_QROWS, NA_SLAB),
                               lambda b, p: (layer, p, 0, 0, 0))],
        out_specs=pl.BlockSpec((DEC_SEQ, PAIR), lambda b, p: (b, p)),
        out_shape=jax.ShapeDtypeStruct((T_LAT, NA_WIDTH), jnp.bfloat16),
        compiler_params=_cparams("arbitrary", "arbitrary"),
        name="na_lat",
    )(qkv, qkv, qkv, ck, cv, bias)


def _na_bias_table(rpb):
    nt = len(NA_BLOCK_TYPES)
    nl = rpb.shape[0]
    j = np.array(NA_BLOCK_TYPES)[:, None]
    r = j * NA_BLK + np.arange(NA_BLK)[None, :]
    slab0 = np.clip(j * NA_BLK - NA_KH // 2, 0, GRID_H - NA_SLAB_ROWS)
    win0 = np.clip(r - NA_KH // 2, 0, GRID_H - NA_KH)
    key_row = slab0[:, :, None] + np.arange(NA_SLAB_ROWS)[None, None, :]
    row_in = (key_row >= win0[:, :, None]) & (key_row < win0[:, :, None] + NA_KH)
    dr = np.clip(key_row - r[:, :, None] + (NA_KH - 1), 0, 2 * NA_KH - 2)
    cq = np.arange(GRID_W)
    ck = np.arange(GRID_W)
    cs = np.clip(cq - NA_KW // 2, 0, GRID_W - NA_KW)
    col_in = (ck[None, :] >= cs[:, None]) & (ck[None, :] < cs[:, None] + NA_KW)
    dc = np.clip(ck[None, :] - cq[:, None] + NA_KW - 1, 0, 2 * NA_KW - 2)
    sel_c = jnp.asarray(np.eye(2 * NA_KW - 1, dtype=np.float32)[dc])
    cols = jnp.einsum("lhdc,qkc->lhdqk", rpb.astype(jnp.float32), sel_c, precision=lax.Precision.HIGHEST)
    cols = jnp.where(jnp.asarray(col_in), cols, NEG_INF)
    n_off = 2 * NA_KH - 1
    cols = cols.reshape(nl, NA_HEADS // 2, 2, n_off, GRID_W, GRID_W)
    masked = jnp.full((nl, NA_HEADS // 2, 2, 1, GRID_W, GRID_W), NEG_INF, jnp.float32)
    cols = jnp.concatenate([cols, masked], axis=3)
    which = jnp.asarray(np.where(row_in, dr, n_off).reshape(-1), jnp.int32)

    def assemble(which_ref, cols_ref, o_ref):
        t = pl.program_id(2)
        for e in range(2):
            for n in range(NA_BLK):
                r0 = (e * NA_BLK + n) * GRID_W
                for i in range(NA_SLAB_ROWS):
                    d = which_ref[(t * NA_BLK + n) * NA_SLAB_ROWS + i]
                    o_ref[0, 0, 0, r0:r0 + GRID_W, i * GRID_W:(i + 1) * GRID_W] = cols_ref[0, 0, e, d]

    grid_spec = pltpu.PrefetchScalarGridSpec(
        num_scalar_prefetch=1,
        grid=(nl, NA_HEADS // 2, nt),
        in_specs=[pl.BlockSpec((1, 1, 2, n_off + 1, GRID_W, GRID_W), lambda l, p, t, w: (l, p, 0, 0, 0, 0))],
        out_specs=pl.BlockSpec((1, 1, 1, 2 * NA_QROWS, NA_SLAB), lambda l, p, t, w: (l, p, t, 0, 0)))
    return pl.pallas_call(
        assemble,
        grid_spec=grid_spec,
        out_shape=jax.ShapeDtypeStruct((nl, NA_HEADS // 2, nt, 2 * NA_QROWS, NA_SLAB), jnp.float32),
        compiler_params=_cparams("arbitrary", "arbitrary", "arbitrary"),
        name="na_bias",
    )(which, cols)


GATE_LANE0 = N_GROUPS


def _router_logits(h2, wr_hi, wr_lo, br):
    h_hi = h2.astype(jnp.bfloat16)
    h_lo = (h2 - h_hi.astype(jnp.float32)).astype(jnp.bfloat16)
    return _bdot(h_hi, wr_hi) + _bdot(h_lo, wr_hi) + _bdot(h_hi, wr_lo) + br


def _top_group(logits):
    lane_i = lax.broadcasted_iota(jnp.int32, logits.shape, 1)
    is_g = lane_i < N_GROUPS
    lg = jnp.where(is_g, logits, NEG_INF)
    mg = lg.max(axis=-1, keepdims=True)
    return jnp.where(is_g & (lg == mg), lane_i.astype(jnp.float32), float(LANES)).min(axis=-1, keepdims=True)


def _gates_in_group(logits, group):
    lane_i = lax.broadcasted_iota(jnp.int32, logits.shape, 1)
    lane = lane_i.astype(jnp.float32)
    big = float(LANES)
    is_g = lane_i < N_GROUPS
    lg = jnp.where(is_g, logits, NEG_INF)
    mg = lg.max(axis=-1, keepdims=True)
    own = jnp.where(lane_i == group, logits, 0.0).sum(axis=-1, keepdims=True)
    pg_top = jnp.exp(own - mg) / jnp.where(is_g, jnp.exp(lg - mg), 0.0).sum(axis=-1, keepdims=True)
    lane0 = GATE_LANE0 + group * EXPERTS_PER_GROUP
    sel = (lane_i >= lane0) & (lane_i < lane0 + EXPERTS_PER_GROUP)
    le = jnp.where(sel, logits, NEG_INF)
    me = le.max(axis=-1, keepdims=True)
    ex = jnp.where(sel, jnp.exp(le - me), 0.0)
    pe = ex / ex.sum(axis=-1, keepdims=True)
    p1 = pe.max(axis=-1, keepdims=True)
    i1 = jnp.where(sel & (pe == p1), lane, big).min(axis=-1, keepdims=True)
    rest = sel & (lane != i1)
    pr = jnp.where(rest, pe, -1.0)
    p2 = pr.max(axis=-1, keepdims=True)
    i2 = jnp.where(rest & (pr == p2), lane, big).min(axis=-1, keepdims=True)
    tot = p1 + p2
    within = jnp.where(lane == i1, p1 / tot, 0.0) + jnp.where(lane == i2, p2 / tot, 0.0)
    return pg_top * within


TILE_ROWS = D_MODEL // LANES


def _store_token_tiles(ref, x):
    n = x.shape[0]
    for c in range(TILE_ROWS):
        ref[pl.ds(c, n, stride=TILE_ROWS), :] = x[:, c * LANES:(c + 1) * LANES]


def _load_token_tiles(ref, n):
    return jnp.concatenate([ref[pl.ds(c, n, stride=TILE_ROWS), :] for c in range(TILE_ROWS)], axis=1)


def _outproj_kernel(x_ref, ys_ref, yc_ref, yn_ref, yg_ref, mod_ref, g_ref, w_ref, wr_hi_ref, wr_lo_ref, br_ref,
                    x1_ref, h2t_ref, grp_ref):
    is_ctx = pl.program_id(0) < T_CTX // TM
    y_na = jnp.where(is_ctx, yc_ref[:, 0:NA_WIDTH], yn_ref[...])
    y_g = jnp.where(is_ctx, yc_ref[:, NA_WIDTH:ATTN_WIDTH], yg_ref[...])
    mix = (_bdot(ys_ref[...], w_ref[0:SSM_WIDTH, :])
           + _bdot(y_na, w_ref[SSM_WIDTH:SSM_WIDTH + NA_WIDTH, :])
           + _bdot(y_g, w_ref[SSM_WIDTH + NA_WIDTH:, :]))
    gate1 = mod_ref[0, :, 2 * D_MODEL:3 * D_MODEL]
    shift2 = mod_ref[0, :, 3 * D_MODEL:4 * D_MODEL]
    scale2 = mod_ref[0, :, 4 * D_MODEL:5 * D_MODEL]
    x1 = x_ref[...] + gate1 * mix
    x1_ref[...] = x1
    ms = jnp.mean(x1 * x1, axis=-1, keepdims=True)
    h2 = (x1 * lax.rsqrt(ms + EPS) * g_ref[...]) * (1.0 + scale2) + shift2
    _store_token_tiles(h2t_ref, h2)
    logits = _router_logits(h2, wr_hi_ref[0], wr_lo_ref[0], br_ref[0])
    grp_ref[...] = jnp.broadcast_to(_top_group(logits), (TM, LANES))


def _out_proj(layer, x, y_ssm, y_ctx, y_na_lat, y_g_lat, mods3, norm_g, w_out_bf, wr_hi, wr_lo, b_router):
    n_ctx = T_CTX // TM
    return pl.pallas_call(
        _outproj_kernel,
        grid=(T_ALL // TM,),
        in_specs=[pl.BlockSpec((TM, D_MODEL), lambda i: (i, 0)),
                  pl.BlockSpec((TM, SSM_WIDTH), lambda i: (i, 0)),
                  pl.BlockSpec((TM, ATTN_WIDTH), lambda i: (jnp.minimum(i, n_ctx - 1), 0)),
                  pl.BlockSpec((TM, NA_WIDTH), lambda i: (jnp.maximum(i - n_ctx, 0), 0)),
                  pl.BlockSpec((TM, GQA_WIDTH), lambda i: (jnp.maximum(i - n_ctx, 0), 0)),
                  pl.BlockSpec((1, 1, 6 * D_MODEL), lambda i: (_modset(i), 0, 0)),
                  pl.BlockSpec((1, D_MODEL), lambda i: (0, 0)),
                  pl.BlockSpec((D_MODEL, D_MODEL), lambda i: (0, 0)),
                  pl.BlockSpec((1, D_MODEL, LANES), lambda i: (layer, 0, 0)),
                  pl.BlockSpec((1, D_MODEL, LANES), lambda i: (layer, 0, 0)),
                  pl.BlockSpec((1, 1, LANES), lambda i: (layer, 0, 0))],
        out_specs=[pl.BlockSpec((TM, D_MODEL), lambda i: (i, 0)),
                   pl.BlockSpec((TM * TILE_ROWS, LANES), lambda i: (i, 0)),
                   pl.BlockSpec((TM, LANES), lambda i: (i, 0))],
        out_shape=(jax.ShapeDtypeStruct((T_ALL, D_MODEL), jnp.float32),
                   jax.ShapeDtypeStruct((T_ALL * TILE_ROWS, LANES), jnp.float32),
                   jax.ShapeDtypeStruct((T_ALL, LANES), jnp.float32)),
        compiler_params=_cparams("arbitrary"),
        name="out_proj",
    )(x, y_ssm, y_ctx, y_na_lat, y_g_lat, mods3, norm_g, w_out_bf, wr_hi, wr_lo, b_router)


TMOE = 256
NT_MOE = T_ALL // TMOE + N_GROUPS
MOE_SLOTS = NT_MOE * TMOE


def _moe_dispatch(group_f):
    i32 = jnp.int32
    g = group_f.astype(i32)
    onehot = (g[:, None] == jnp.arange(N_GROUPS, dtype=i32)[None, :]).astype(i32)
    csum = jnp.cumsum(onehot, axis=0)
    counts = csum[-1]
    rank = jnp.sum(onehot * csum, axis=1) - 1
    tiles = (counts + TMOE - 1) // TMOE
    tile_end = jnp.cumsum(tiles)
    tile_start = tile_end - tiles
    slot = jnp.sum(onehot * tile_start[None, :], axis=1) * TMOE + rank
    tok = jnp.full((MOE_SLOTS,), -1, i32).at[slot].set(jnp.arange(T_ALL, dtype=i32))
    pad = tok < 0
    pad_rank = jnp.cumsum(pad.astype(i32)) - 1
    row_of_slot = jnp.where(pad, T_ALL + pad_rank, tok) * TILE_ROWS
    tile_group = jnp.minimum(
        jnp.sum(jnp.arange(NT_MOE, dtype=i32)[:, None] >= tile_end[None, :], axis=1), N_GROUPS - 1).astype(i32)
    return tile_group, row_of_slot


def _moe_kernel(tg_ref, row_ref, h_hbm, wr_hi_ref, wr_lo_ref, br_ref, wg_ref, wu_ref, wd_ref, y_hbm,
                gbuf, ybuf, wg_bf, wu_bf, wd_bf, gsem, ssem):
    i = pl.program_id(0)
    slot = lax.rem(i, 2)
    other = 1 - slot
    last = NT_MOE - 1
    nxt = jnp.minimum(i + 1, last)
    prv = jnp.maximum(i - 1, 0)
    buf_rows = TMOE * TILE_ROWS

    def gather_tok(row, buf_slot, r):
        return pltpu.make_async_copy(h_hbm.at[pl.ds(row, TILE_ROWS)],
                                     gbuf.at[buf_slot, pl.ds(r * TILE_ROWS, TILE_ROWS)], gsem.at[buf_slot])

    def scatter_tok(row, buf_slot, r):
        return pltpu.make_async_copy(ybuf.at[buf_slot, pl.ds(r * TILE_ROWS, TILE_ROWS)],
                                     y_hbm.at[pl.ds(row, TILE_ROWS)], ssem.at[buf_slot])

    def wait_gather(buf_slot):
        pltpu.make_async_copy(h_hbm.at[pl.ds(0, buf_rows)], gbuf.at[buf_slot], gsem.at[buf_slot]).wait()

    def wait_scatter(buf_slot):
        pltpu.make_async_copy(ybuf.at[buf_slot], y_hbm.at[pl.ds(0, buf_rows)], ssem.at[buf_slot]).wait()

    def start_gather_tok(tile, buf_slot, r):
        row = row_ref[tile * TMOE + r]
        gather_tok(jnp.where(row < T_ALL * TILE_ROWS, row, 0), buf_slot, r).start()

    @pl.when(i == 0)
    def _():
        ybuf[1] = jnp.zeros((buf_rows, LANES), jnp.float32)
        for r in range(TMOE):
            start_gather_tok(0, 0, r)

    first_of_group = (i == 0) | (tg_ref[i] != tg_ref[prv])

    @pl.when(first_of_group)
    def _():
        wg_bf[...] = wg_ref[0].astype(jnp.bfloat16)
        wu_bf[...] = wu_ref[0].astype(jnp.bfloat16)
        wd_bf[...] = wd_ref[0].astype(jnp.bfloat16)

    wait_gather(slot)

    h2 = _load_token_tiles(gbuf.at[slot], TMOE)
    h = h2.astype(jnp.bfloat16)
    routing = _gates_in_group(_router_logits(h2, wr_hi_ref[0], wr_lo_ref[0], br_ref[0]), tg_ref[i])
    lane = lax.broadcasted_iota(jnp.int32, (1, LANES), 1)
    lane0 = GATE_LANE0 + tg_ref[i] * EXPERTS_PER_GROUP
    rows_per_chunk = TMOE // EXPERTS_PER_GROUP
    y = None
    for j in range(EXPERTS_PER_GROUP):
        for r in range(j * rows_per_chunk, (j + 1) * rows_per_chunk):
            start_gather_tok(nxt, other, r)
            dst = jnp.where(i > 0, row_ref[prv * TMOE + r], (MOE_SLOTS + r) * TILE_ROWS)
            scatter_tok(dst, other, r).start()
        gate = jnp.where(lane == lane0 + j, routing, 0.0).sum(axis=-1, keepdims=True)
        aj = _bdot(h, wg_bf[j])
        bj = _bdot(h, wu_bf[j])
        act = ((aj / (1.0 + jnp.exp(-aj))) * bj * gate).astype(jnp.bfloat16)
        yj = _bdot(act, wd_bf[j])
        y = yj if y is None else y + yj

    @pl.when(i > 0)
    def _():
        wait_scatter(slot)

    _store_token_tiles(ybuf.at[slot], y)

    @pl.when(i == last)
    def _():
        for r in range(TMOE):
            scatter_tok(row_ref[last * TMOE + r], slot, r).start()
        wait_scatter(other)
        wait_scatter(slot)
        wait_gather(other)


def _moe(layer, h2_tiles, tile_group, row_of_slot, wr_hi, wr_lo, b_router, w_gate, w_up, w_down):
    def wspec(shape):
        return pl.BlockSpec((1, EXPERTS_PER_GROUP) + shape, lambda i, tg, rows: (layer, tg[i], 0, 0))

    def rspec(rows):
        return pl.BlockSpec((1, rows, LANES), lambda i, tg, rows_: (layer, 0, 0))

    grid_spec = pltpu.PrefetchScalarGridSpec(
        num_scalar_prefetch=2,
        grid=(NT_MOE,),
        in_specs=[pl.BlockSpec(memory_space=pl.ANY),
                  rspec(D_MODEL), rspec(D_MODEL), rspec(1),
                  wspec((D_MODEL, EXPERT_HIDDEN)), wspec((D_MODEL, EXPERT_HIDDEN)),
                  wspec((EXPERT_HIDDEN, D_MODEL))],
        out_specs=pl.BlockSpec(memory_space=pl.ANY),
        scratch_shapes=[pltpu.VMEM((2, TMOE * TILE_ROWS, LANES), jnp.float32),
                        pltpu.VMEM((2, TMOE * TILE_ROWS, LANES), jnp.float32),
                        pltpu.VMEM((EXPERTS_PER_GROUP, D_MODEL, EXPERT_HIDDEN), jnp.bfloat16),
                        pltpu.VMEM((EXPERTS_PER_GROUP, D_MODEL, EXPERT_HIDDEN), jnp.bfloat16),
                        pltpu.VMEM((EXPERTS_PER_GROUP, EXPERT_HIDDEN, D_MODEL), jnp.bfloat16),
                        pltpu.SemaphoreType.DMA((2,)),
                        pltpu.SemaphoreType.DMA((2,))])
    return pl.pallas_call(
        _moe_kernel,
        grid_spec=grid_spec,
        out_shape=jax.ShapeDtypeStruct(((MOE_SLOTS + TMOE) * TILE_ROWS, LANES), jnp.float32),
        compiler_params=_cparams("arbitrary"),
        name="moe",
    )(tile_group, row_of_slot, h2_tiles, wr_hi, wr_lo, b_router, w_gate, w_up, w_down)


def _final_kernel(x1_ref, y_ref, mod_ref, g_ref, o_ref):
    x = x1_ref[...] + mod_ref[0, :, 5 * D_MODEL:6 * D_MODEL] * _load_token_tiles(y_ref, TM)
    ms = jnp.mean(x * x, axis=-1, keepdims=True)
    o_ref[...] = x * lax.rsqrt(ms + EPS) * g_ref[...]


def _final_norm(x1, y, mods3, g):
    return pl.pallas_call(
        _final_kernel,
        grid=(T_ALL // TM,),
        in_specs=[pl.BlockSpec((TM, D_MODEL), lambda i: (i, 0)),
                  pl.BlockSpec((TM * TILE_ROWS, LANES), lambda i: (i, 0)),
                  pl.BlockSpec((1, 1, 6 * D_MODEL), lambda i: (_modset(i), 0, 0)),
                  pl.BlockSpec((1, D_MODEL), lambda i: (0, 0))],
        out_specs=pl.BlockSpec((TM, D_MODEL), lambda i: (i, 0)),
        out_shape=jax.ShapeDtypeStruct((T_ALL, D_MODEL), jnp.float32),
        compiler_params=_cparams("arbitrary"),
        name="final_norm",
    )(x1, y, mods3, g)


_QG_ORDER = np.array([0, 3, 1, 4, 2, 5])


def _permute_qg_columns(w_in):
    c1 = SSM_WIDTH + 3 * NA_WIDTH
    qg = w_in[..., c1:c1 + GQA_WIDTH].reshape(*w_in.shape[:-1], GQA_HEADS, HEAD_DIM)
    qg = qg[..., _QG_ORDER, :].reshape(*w_in.shape[:-1], GQA_WIDTH)
    return jnp.concatenate([w_in[..., :c1], qg, w_in[..., c1 + GQA_WIDTH:]], axis=-1)


def _permute_yg_rows(w_out):
    r0 = SSM_WIDTH + NA_WIDTH
    yg = w_out[:, r0:, :].reshape(DEPTH, GQA_HEADS, HEAD_DIM, D_MODEL)[:, _QG_ORDER]
    return jnp.concatenate([w_out[:, :r0, :], yg.reshape(DEPTH, GQA_WIDTH, D_MODEL)], axis=1)


def _rope_tables():
    t = jnp.arange(DEC_SEQ)
    row = (t // GRID_W).astype(jnp.float32)
    col = (t % GRID_W).astype(jnp.float32)
    freqs = ROPE_THETA ** (-jnp.arange(ROPE_F, dtype=jnp.float32) / ROPE_F)
    ang_r = row[:, None] * freqs
    ang_c = col[:, None] * freqs
    ang = jnp.concatenate([ang_r, ang_r, ang_c, ang_c], axis=1)
    cos = jnp.cos(ang)
    sin = jnp.sin(ang)
    sign = jnp.tile(jnp.concatenate([-jnp.ones(ROPE_F), jnp.ones(ROPE_F)]), 2).astype(jnp.float32)
    sin = sin * sign
    cos = jnp.tile(cos, (1, 2))
    sin = jnp.tile(sin, (1, 2))
    cos = jnp.concatenate([jnp.ones((TM, LANES), jnp.float32), cos], axis=0)
    sin = jnp.concatenate([jnp.zeros((TM, LANES), jnp.float32), sin], axis=0)
    return cos, sin


def _pair_layout(cache):
    b, l, h, n, dh = cache.shape
    c = cache.reshape(b, l, h // 2, 2, n, dh).transpose(1, 0, 2, 4, 3, 5)
    return c.reshape(l, b, h // 2, n, 2 * dh).astype(jnp.bfloat16)


def _block_diag_b(bbar):
    bb = bbar.reshape(DEPTH, 2, SSM_GROUP_CH, SSM_GROUPS, SSM_STATE)
    eye = jnp.eye(SSM_GROUPS, dtype=bbar.dtype)
    out = eye[None, None, :, None, :, None] * bb.transpose(0, 1, 3, 2, 4)[:, :, :, :, None, :]
    return out.reshape(DEPTH, 2, SSM_WIDTH, S5_HALF)


def _block_diag_c(c):
    eye = jnp.eye(SSM_GROUPS, dtype=c.dtype)
    out = eye[None, None, :, None, :, None] * c.transpose(0, 1, 2, 4, 3)[:, :, :, :, None, :]
    return out.reshape(DEPTH, 2, S5_HALF, SSM_WIDTH)


def kernel(x_prompt, x_sample, cache_na_k, cache_na_v, cache_gqa_k, cache_gqa_v, state_ssm, c, c_ctx, norm_mix_g, norm_ffn_g, w_ada, b_ada, w_in, ssm_a_re, ssm_a_im, ssm_log_dt, ssm_b_re, ssm_b_im, ssm_c_re, ssm_c_im, ssm_d, w_glu, b_glu, na_rpb, q_norm_g, k_norm_g, w_out, moe_w_group, moe_b_group, moe_w_expert, moe_b_expert, moe_w_gate, moe_w_up, moe_w_down, final_norm_g):
    f32, bf16 = jnp.float32, jnp.bfloat16
    x = jnp.concatenate([x_prompt.reshape(T_CTX, D_MODEL), x_sample.reshape(T_LAT, D_MODEL)], axis=0)

    cvec8 = jnp.concatenate([c_ctx[None, :], c, jnp.zeros((SUBLANES - N_MODSETS, D_MODEL), f32)], axis=0)
    mods_all = _adaln_all(cvec8, w_ada, b_ada)

    w_in_bf = _permute_qg_columns(w_in).astype(bf16)
    w_out_bf = _permute_yg_rows(w_out).astype(bf16)
    w_glu_bf = w_glu.astype(bf16)
    w_router = jnp.concatenate(
        [moe_w_group, moe_w_expert, jnp.zeros((DEPTH, D_MODEL, LANES - N_GROUPS - N_EXPERTS), f32)], axis=-1)
    wr_hi = w_router.astype(bf16)
    wr_lo = (w_router - wr_hi.astype(f32)).astype(bf16)
    b_router = jnp.concatenate(
        [moe_b_group, moe_b_expert, jnp.zeros((DEPTH, LANES - N_GROUPS - N_EXPERTS), f32)], axis=-1)[:, None, :]
    q_gain = jnp.tile(q_norm_g, (1, 2))
    k_gain = jnp.tile(k_norm_g, (1, 2))
    lane = np.arange(LANES)
    gsum = jnp.asarray((lane[:, None] // HEAD_DIM) == (lane[None, :] // HEAD_DIM), bf16)
    rope_cos, rope_sin = _rope_tables()

    abar_re, abar_im, bbar_re, bbar_im = _s5_prep(ssm_a_re, ssm_a_im, ssm_log_dt, ssm_b_re, ssm_b_im)
    abar = jnp.concatenate([abar_re, abar_im], axis=-1).reshape(DEPTH, 2, 1, S5_CPLX)
    bblk = jnp.concatenate([_block_diag_b(bbar_re), _block_diag_b(bbar_im)], axis=-1).astype(bf16)
    cblk = jnp.concatenate([_block_diag_c(ssm_c_re), -_block_diag_c(ssm_c_im)], axis=2).astype(bf16)
    h0 = state_ssm.transpose(1, 2, 0, 5, 3, 4).reshape(DEPTH, 2, DEC_BATCH, S5_CPLX)

    ck_na, cv_na = _pair_layout(cache_na_k), _pair_layout(cache_na_v)
    ck_g, cv_g = _pair_layout(cache_gqa_k)[:, :, 0], _pair_layout(cache_gqa_v)[:, :, 0]
    first_half = jnp.asarray(np.arange(LANES) < HEAD_DIM)
    cv_g = jnp.stack([jnp.where(first_half, cv_g, 1.0), jnp.where(first_half, 1.0, cv_g)],
                     axis=2).astype(bf16)
    na_bias = _na_bias_table(na_rpb)

    kv32_l, fin_l = [], []
    moe_residual = None
    for l in range(DEPTH):
        mods3 = mods_all[l, :N_MODSETS].reshape(N_MODSETS, 1, 6 * D_MODEL)
        u, qkv, kv32, x = _in_proj(x, moe_residual, mods3, norm_mix_g[l][None], w_in_bf[l], q_gain[l][None],
                                   k_gain[l][None], gsum, rope_cos, rope_sin)
        kv32_l.append(kv32)

        u_tm = jnp.pad(u.reshape(S5_SEQS, S5_CHUNK, SSM_WIDTH).transpose(1, 0, 2),
                       ((0, 0), (0, S5_ROWS - S5_SEQS), (0, 0)))
        y_dir, powers, finals = _s5_scan(u_tm.reshape(S5_CHUNK * S5_ROWS, SSM_WIDTH), bblk[l], abar[l], cblk[l])
        fin_l.append(finals[:, :BATCH])
        y_tm = _s5_finish(u_tm, y_dir.reshape(2, S5_CHUNK, S5_ROWS, SSM_WIDTH), powers, finals, h0[l], cblk[l],
                          ssm_d[l][None], w_glu_bf[l], b_glu[l][None])
        y_ssm = y_tm[:, :S5_SEQS].transpose(1, 0, 2).reshape(T_ALL, SSM_WIDTH)

        y_ctx = _attn_ctx(qkv)
        y_g_lat = _gqa_lat(qkv, ck_g[l], cv_g[l])
        y_na_lat = _na_lat(l, qkv, ck_na[l], cv_na[l], na_bias)

        x1, h2_tiles, group = _out_proj(l, x, y_ssm, y_ctx, y_na_lat, y_g_lat, mods3, norm_ffn_g[l][None],
                                        w_out_bf[l], wr_hi, wr_lo, b_router)
        tile_group, row_of_slot = _moe_dispatch(group[:, 0])
        y_moe = _moe(l, h2_tiles, tile_group, row_of_slot, wr_hi, wr_lo, b_router,
                     moe_w_gate, moe_w_up, moe_w_down)
        x, moe_residual = x1, (y_moe, mods3)

    y = _final_norm(x, moe_residual[0], moe_residual[1], final_norm_g[None])
    y_prompt = y[:T_CTX].reshape(BATCH, SEQ, D_MODEL)
    y_sample = y[T_CTX:].reshape(DEC_BATCH, DEC_SEQ, D_MODEL)

    kv32 = jnp.stack(kv32_l, axis=0).reshape(DEPTH, BATCH, SEQ, KV32_WIDTH)

    def heads(t, n_heads):
        return t.reshape(DEPTH, BATCH, SEQ, n_heads, HEAD_DIM).transpose(1, 0, 3, 2, 4)

    new_na_k = heads(kv32[..., 0:NA_WIDTH], NA_HEADS)
    new_na_v = heads(kv32[..., NA_WIDTH:2 * NA_WIDTH], NA_HEADS)
    new_gqa_k = heads(kv32[..., 2 * NA_WIDTH:2 * NA_WIDTH + KV_WIDTH], GQA_KV_HEADS)
    new_gqa_v = heads(kv32[..., 2 * NA_WIDTH + KV_WIDTH:], GQA_KV_HEADS)
    fin = jnp.stack(fin_l, axis=0).reshape(DEPTH, 2, BATCH, 2, SSM_GROUPS, SSM_STATE)
    new_state = fin.transpose(2, 0, 1, 4, 5, 3)
    return (y_prompt, y_sample, new_na_k, new_na_v, new_gqa_k, new_gqa_v, new_state)
```

```python
import functools
import math

import jax
import jax.numpy as jnp
import numpy as np
from jax import lax
from jax.experimental import pallas as pl
from jax.experimental.pallas import tpu as pltpu

D_MODEL = 1024
BATCH = 16
SEQ = 256
DEPTH = 4
DEC_BATCH = 2
DEC_SEQ = 4096
PAST_LEN = 512
GRID_W = 64
GRID_H = DEC_SEQ // GRID_W
HEAD_DIM = 64
SSM_WIDTH = 256
SSM_GROUP_CH = 16
SSM_GROUPS = 16
SSM_STATE = 64
NA_HEADS = 6
NA_WIDTH = 384
NA_KH = 8
NA_KW = 16
GQA_HEADS = 6
GQA_KV_HEADS = 2
GQA_REP = 3
GQA_WIDTH = 384
KV_WIDTH = 128
IN_WIDTH = 2048
ROPE_THETA = 10000.0
ROPE_F = 16
N_GROUPS = 4
EXPERTS_PER_GROUP = 4
N_EXPERTS = 16
EXPERT_HIDDEN = 256
EPS = 1e-6
NEG_INF = -1e30

T_CTX = BATCH * SEQ
T_LAT = DEC_BATCH * DEC_SEQ
T_ALL = T_CTX + T_LAT
N_MODSETS = 1 + DEC_BATCH
LANES = 128
SUBLANES = 8
PAIR = 2 * HEAD_DIM
ATTN_SCALE = HEAD_DIM ** -0.5

S5_CHUNK = 256
S5_SEQS = T_ALL // S5_CHUNK
S5_ROWS = 56
S5_POW_ROW = 48
S5_LAT0 = BATCH
S5_NLAT = S5_SEQS - BATCH
S5_TB = 16
S5_NBLK = S5_CHUNK // S5_TB
S5_CPLX = 2 * SSM_GROUPS * SSM_STATE
S5_HALF = SSM_GROUPS * SSM_STATE

TM = 512
VMEM_LIMIT = 56 * 1024 * 1024


def _cparams(*sem):
    return pltpu.CompilerParams(dimension_semantics=sem, vmem_limit_bytes=VMEM_LIMIT)


def _modset(i):
    return jnp.where(i < T_CTX // TM, 0, 1 + (i - T_CTX // TM) // (DEC_SEQ // TM))


def _bdot(a, b):
    return jnp.dot(a, b, preferred_element_type=jnp.float32)


def _bdot_t(a, b):
    return lax.dot_general(a, b, (((1,), (1,)), ((), ())), preferred_element_type=jnp.float32)


def _ada_kernel(c_ref, w_ref, b_ref, o_ref):
    c = c_ref[...]
    s = c / (1.0 + jnp.exp(-c))
    o_ref[0] = _bdot(s.astype(jnp.bfloat16), w_ref[0].astype(jnp.bfloat16)) + b_ref[0]


def _adaln_all(cvec8, w_ada, b_ada):
    tn = 1024
    return pl.pallas_call(
        _ada_kernel,
        grid=(DEPTH, 6 * D_MODEL // tn),
        in_specs=[pl.BlockSpec((SUBLANES, D_MODEL), lambda l, j: (0, 0)),
                  pl.BlockSpec((1, D_MODEL, tn), lambda l, j: (l, 0, j)),
                  pl.BlockSpec((1, 1, tn), lambda l, j: (l, 0, j))],
        out_specs=pl.BlockSpec((1, SUBLANES, tn), lambda l, j: (l, 0, j)),
        out_shape=jax.ShapeDtypeStruct((DEPTH, SUBLANES, 6 * D_MODEL), jnp.float32),
        compiler_params=_cparams("arbitrary", "arbitrary"),
        name="adaln",
    )(cvec8, w_ada, b_ada.reshape(DEPTH, 1, 6 * D_MODEL))


def _s5prep_kernel(ar_ref, ai_ref, ldt_ref, br_ref, bi_ref, abar_re, abar_im, bbar_re, bbar_im):
    ar, ai = ar_ref[0], ai_ref[0]
    dt = jnp.exp(ldt_ref[0])
    mag = jnp.exp(ar * dt)
    are = mag * jnp.cos(ai * dt)
    aim = mag * jnp.sin(ai * dt)
    zr, zi = are - 1.0, aim
    den = ar * ar + ai * ai
    cre = (zr * ar + zi * ai) / den
    cim = (zi * ar - zr * ai) / den
    abar_re[0] = are
    abar_im[0] = aim
    br, bi = br_ref[0], bi_ref[0]
    bbar_re[0] = cre * br - cim * bi
    bbar_im[0] = cre * bi + cim * br


def _s5_prep(a_re, a_im, log_dt, b_re, b_im):
    n = DEPTH * 2
    ar = a_re.reshape(n, 1, S5_HALF)
    ai = a_im.reshape(n, 1, S5_HALF)
    ldt = jnp.broadcast_to(log_dt[..., None], (DEPTH, 2, SSM_GROUPS, SSM_STATE)).reshape(n, 1, S5_HALF)
    br = b_re.transpose(0, 1, 4, 2, 3).reshape(n, SSM_GROUP_CH, S5_HALF)
    bi = b_im.transpose(0, 1, 4, 2, 3).reshape(n, SSM_GROUP_CH, S5_HALF)
    vec = jax.ShapeDtypeStruct((n, 1, S5_HALF), jnp.float32)
    mat = jax.ShapeDtypeStruct((n, SSM_GROUP_CH, S5_HALF), jnp.float32)
    vspec = pl.BlockSpec((1, 1, S5_HALF), lambda j: (j, 0, 0))
    mspec = pl.BlockSpec((1, SSM_GROUP_CH, S5_HALF), lambda j: (j, 0, 0))
    return pl.pallas_call(
        _s5prep_kernel,
        grid=(n,),
        in_specs=[vspec, vspec, vspec, mspec, mspec],
        out_specs=[vspec, vspec, mspec, mspec],
        out_shape=(vec, vec, mat, mat),
        compiler_params=_cparams("arbitrary"),
        name="s5prep",
    )(ar, ai, ldt, br, bi)


def _head_rms(blk, gsum, gain):
    sq = blk * blk
    hi = sq.astype(jnp.bfloat16)
    lo = (sq - hi.astype(jnp.float32)).astype(jnp.bfloat16)
    ms = (_bdot(hi, gsum) + _bdot(lo, gsum)) * (1.0 / HEAD_DIM)
    return blk * lax.rsqrt(ms + EPS) * gain


def _rope(blk, cos, sin_signed, first_half):
    partner = jnp.where(first_half, pltpu.roll(blk, LANES - ROPE_F, 1), pltpu.roll(blk, ROPE_F, 1))
    return blk * cos + partner * sin_signed


def _inproj_kernel(has_moe_residual, *refs):
    if has_moe_residual:
        (x_ref, y_ref, pmod_ref, mod_ref, g_ref, w_ref, qg_ref, kg_ref, gsum_ref, cos_ref, sin_ref,
         u_ref, qkv_ref, kv32_ref, xo_ref) = refs
        x = x_ref[...] + pmod_ref[0, :, 5 * D_MODEL:6 * D_MODEL] * _load_token_tiles(y_ref, TM)
        xo_ref[...] = x
    else:
        (x_ref, mod_ref, g_ref, w_ref, qg_ref, kg_ref, gsum_ref, cos_ref, sin_ref,
         u_ref, qkv_ref, kv32_ref) = refs
        x = x_ref[...]
    i = pl.program_id(0)
    shift = mod_ref[0, :, 0:D_MODEL]
    scale = mod_ref[0, :, D_MODEL:2 * D_MODEL]
    ms = jnp.mean(x * x, axis=-1, keepdims=True)
    h = (x * lax.rsqrt(ms + EPS) * g_ref[...]) * (1.0 + scale) + shift
    proj = _bdot(h.astype(jnp.bfloat16), w_ref[...])
    u_ref[...] = proj[:, 0:SSM_WIDTH]

    c0 = SSM_WIDTH
    qn = proj[:, c0:c0 + NA_WIDTH] * ATTN_SCALE
    kn = proj[:, c0 + NA_WIDTH:c0 + 2 * NA_WIDTH]
    vn = proj[:, c0 + 2 * NA_WIDTH:c0 + 3 * NA_WIDTH]
    c1 = c0 + 3 * NA_WIDTH
    vg = proj[:, c1 + GQA_WIDTH + KV_WIDTH:c1 + GQA_WIDTH + 2 * KV_WIDTH]

    gsum = gsum_ref[...]
    cos, sin = cos_ref[...], sin_ref[...]
    lane = lax.broadcasted_iota(jnp.int32, (1, LANES), 1)
    first_half = (lane & (2 * ROPE_F - 1)) < ROPE_F
    qg_blocks = []
    for p in range(GQA_WIDTH // PAIR):
        blk = proj[:, c1 + p * PAIR:c1 + (p + 1) * PAIR]
        blk = _rope(_head_rms(blk, gsum, qg_ref[...]), cos, sin, first_half)
        qg_blocks.append(blk * ATTN_SCALE)
    kg = proj[:, c1 + GQA_WIDTH:c1 + GQA_WIDTH + KV_WIDTH]
    kg = _rope(_head_rms(kg, gsum, kg_ref[...]), cos, sin, first_half)

    o = 0
    vg_ones = [jnp.where(lane < HEAD_DIM, vg, 1.0), jnp.where(lane >= HEAD_DIM, vg, 1.0)]
    for piece in [qn, kn, vn] + qg_blocks + [kg] + vg_ones:
        w = piece.shape[1]
        qkv_ref[:, o:o + w] = piece.astype(jnp.bfloat16)
        o += w

    @pl.when(i < T_CTX // TM)
    def _():
        kv32_ref[:, 0:NA_WIDTH] = kn
        kv32_ref[:, NA_WIDTH:2 * NA_WIDTH] = vn
        kv32_ref[:, 2 * NA_WIDTH:2 * NA_WIDTH + KV_WIDTH] = kg
        kv32_ref[:, 2 * NA_WIDTH + KV_WIDTH:2 * NA_WIDTH + 2 * KV_WIDTH] = vg


QKV_WIDTH = 3 * NA_WIDTH + GQA_WIDTH + 3 * KV_WIDTH
KV32_WIDTH = 2 * NA_WIDTH + 2 * KV_WIDTH
QN_BLK, KN_BLK, VN_BLK, QG_BLK, KG_BLK, VG1_BLK = 0, 3, 6, 9, 12, 13


def _rope_block(i):
    n_ctx = T_CTX // TM
    return jnp.where(i < n_ctx, 0, 1 + (i - n_ctx) % (DEC_SEQ // TM))


def _in_proj(x, moe_residual, mods3, norm_g, w_in_bf, q_gain, k_gain, gsum, rope_cos, rope_sin):
    n_ctx = T_CTX // TM
    row_spec = pl.BlockSpec((TM, D_MODEL), lambda i: (i, 0))
    mod_spec = pl.BlockSpec((1, 1, 6 * D_MODEL), lambda i: (_modset(i), 0, 0))
    args, in_specs = [x], [row_spec]
    out_specs = [pl.BlockSpec((TM, SSM_WIDTH), lambda i: (i, 0)),
                 pl.BlockSpec((TM, QKV_WIDTH), lambda i: (i, 0)),
                 pl.BlockSpec((TM, KV32_WIDTH), lambda i: (jnp.minimum(i, n_ctx - 1), 0))]
    out_shape = [jax.ShapeDtypeStruct((T_ALL, SSM_WIDTH), jnp.float32),
                 jax.ShapeDtypeStruct((T_ALL, QKV_WIDTH), jnp.bfloat16),
                 jax.ShapeDtypeStruct((T_CTX, KV32_WIDTH), jnp.float32)]
    if moe_residual is not None:
        args += list(moe_residual)
        in_specs += [pl.BlockSpec((TM * TILE_ROWS, LANES), lambda i: (i, 0)), mod_spec]
        out_specs.append(row_spec)
        out_shape.append(jax.ShapeDtypeStruct((T_ALL, D_MODEL), jnp.float32))
    args += [mods3, norm_g, w_in_bf, q_gain, k_gain, gsum, rope_cos, rope_sin]
    in_specs += [mod_spec,
                 pl.BlockSpec((1, D_MODEL), lambda i: (0, 0)),
                 pl.BlockSpec((D_MODEL, IN_WIDTH), lambda i: (0, 0)),
                 pl.BlockSpec((1, LANES), lambda i: (0, 0)),
                 pl.BlockSpec((1, LANES), lambda i: (0, 0)),
                 pl.BlockSpec((LANES, LANES), lambda i: (0, 0)),
                 pl.BlockSpec((TM, LANES), lambda i: (_rope_block(i), 0)),
                 pl.BlockSpec((TM, LANES), lambda i: (_rope_block(i), 0))]
    outs = pl.pallas_call(
        functools.partial(_inproj_kernel, moe_residual is not None),
        grid=(T_ALL // TM,),
        in_specs=in_specs,
        out_specs=out_specs,
        out_shape=out_shape,
        compiler_params=_cparams("arbitrary"),
        name="in_proj",
    )(*args)
    return outs if moe_residual is not None else (*outs, x)


def _s5scan_kernel(u_ref, b_ref, a_ref, c_ref, y_ref, p_ref, fin_ref, hbuf):
    d = pl.program_id(0)
    k = pl.program_id(1)
    rows = S5_TB * S5_ROWS
    a_re = jnp.broadcast_to(a_ref[0, :, 0:S5_HALF], (SUBLANES, S5_HALF))
    a_im = jnp.broadcast_to(a_ref[0, :, S5_HALF:S5_CPLX], (SUBLANES, S5_HALF))
    ub = u_ref[...].astype(jnp.bfloat16)

    def run(forward):
        data0 = S5_ROWS if forward else 0
        prev0 = 0 if forward else rows
        last0 = rows if forward else 0

        @pl.when(k == 0)
        def _():
            r = lax.broadcasted_iota(jnp.int32, (S5_ROWS, S5_CPLX), 0)
            l = lax.broadcasted_iota(jnp.int32, (S5_ROWS, S5_CPLX), 1)
            hbuf[prev0:prev0 + S5_ROWS, :] = jnp.where((r == S5_POW_ROW) & (l < S5_HALF), 1.0, 0.0)

        @pl.when(k > 0)
        def _():
            hbuf[prev0:prev0 + S5_ROWS, :] = hbuf[last0:last0 + S5_ROWS, :]

        hbuf[data0:data0 + rows, :] = _bdot(ub, b_ref[0])

        def step(i, carry):
            t = i if forward else S5_TB - 1 - i
            cur = pl.multiple_of(data0 + t * S5_ROWS, SUBLANES)
            prv = pl.multiple_of(cur - S5_ROWS if forward else cur + S5_ROWS, SUBLANES)
            for st in range(S5_ROWS // SUBLANES):
                rc = pl.ds(cur + st * SUBLANES, SUBLANES)
                rp = pl.ds(prv + st * SUBLANES, SUBLANES)
                hr = hbuf[rp, 0:S5_HALF]
                hi = hbuf[rp, S5_HALF:S5_CPLX]
                nr = a_re * hr - a_im * hi + hbuf[rc, 0:S5_HALF]
                ni = a_re * hi + a_im * hr + hbuf[rc, S5_HALF:S5_CPLX]
                hbuf[rc, 0:S5_HALF] = nr
                hbuf[rc, S5_HALF:S5_CPLX] = ni
            return carry

        lax.fori_loop(0, S5_TB, step, 0, unroll=True)
        y_ref[0] = _bdot(hbuf[data0:data0 + rows, :].astype(jnp.bfloat16), c_ref[0])
        for t in range(S5_TB):
            r0 = data0 + t * S5_ROWS + S5_POW_ROW
            p_ref[0, t:t + 1, :] = hbuf[r0:r0 + 1, :]

        @pl.when(k == S5_NBLK - 1)
        def _():
            fin_ref[0] = hbuf[last0:last0 + S5_ROWS, :]

    @pl.when(d == 0)
    def _():
        run(True)

    @pl.when(d == 1)
    def _():
        run(False)


def _s5_scan(u_tm, bblk, abar, cblk):
    rows = S5_TB * S5_ROWS

    def tblk(d, k):
        return jnp.where(d == 0, k, S5_NBLK - 1 - k)

    return pl.pallas_call(
        _s5scan_kernel,
        grid=(2, S5_NBLK),
        in_specs=[pl.BlockSpec((rows, SSM_WIDTH), lambda d, k: (tblk(d, k), 0)),
                  pl.BlockSpec((1, SSM_WIDTH, S5_CPLX), lambda d, k: (d, 0, 0)),
                  pl.BlockSpec((1, 1, S5_CPLX), lambda d, k: (d, 0, 0)),
                  pl.BlockSpec((1, S5_CPLX, SSM_WIDTH), lambda d, k: (d, 0, 0))],
        out_specs=[pl.BlockSpec((1, rows, SSM_WIDTH), lambda d, k: (d, tblk(d, k), 0)),
                   pl.BlockSpec((1, S5_TB, S5_CPLX), lambda d, k: (d, tblk(d, k), 0)),
                   pl.BlockSpec((1, S5_ROWS, S5_CPLX), lambda d, k: (d, 0, 0))],
        out_shape=(jax.ShapeDtypeStruct((2, S5_CHUNK * S5_ROWS, SSM_WIDTH), jnp.float32),
                   jax.ShapeDtypeStruct((2, S5_CHUNK, S5_CPLX), jnp.float32),
                   jax.ShapeDtypeStruct((2, S5_ROWS, S5_CPLX), jnp.float32)),
        scratch_shapes=[pltpu.VMEM((rows + S5_ROWS, S5_CPLX), jnp.float32)],
        compiler_params=_cparams("arbitrary", "arbitrary"),
        name="s5_scan",
    )(u_tm, bblk, abar, cblk)


def _cmul(pr, pi, cr, ci):
    return pr * cr - pi * ci, pr * ci + pi * cr


def _s5fin_kernel(u_ref, yf_ref, yb_ref, p_ref, fin_ref, h0_ref, c_ref, d_ref, wg_ref, bg_ref,
                  o_ref, carry, hc, ybuf):
    k = pl.program_id(0)
    nchunk = DEC_SEQ // S5_CHUNK

    @pl.when(k == 0)
    def _():
        for d in range(2):
            t_full = S5_CHUNK - 1 if d == 0 else 0
            pr = p_ref[d, t_full:t_full + 1, 0:S5_HALF]
            pi = p_ref[d, t_full:t_full + 1, S5_HALF:S5_CPLX]
            for b in range(DEC_BATCH):
                cr = h0_ref[d, b:b + 1, 0:S5_HALF]
                ci = h0_ref[d, b:b + 1, S5_HALF:S5_CPLX]
                order = range(nchunk) if d == 0 else range(nchunk - 1, -1, -1)
                for c in order:
                    row = b * nchunk + c
                    carry[d, row:row + 1, 0:S5_HALF] = cr
                    carry[d, row:row + 1, S5_HALF:S5_CPLX] = ci
                    fr = fin_ref[d, S5_LAT0 + row:S5_LAT0 + row + 1, 0:S5_HALF]
                    fi = fin_ref[d, S5_LAT0 + row:S5_LAT0 + row + 1, S5_HALF:S5_CPLX]
                    mr, mi = _cmul(pr, pi, cr, ci)
                    cr, ci = fr + mr, fi + mi

    ybuf[...] = yf_ref[0] + yb_ref[0] + u_ref[...] * d_ref[...]
    for d in range(2):
        cr = carry[d, :, 0:S5_HALF]
        ci = carry[d, :, S5_HALF:S5_CPLX]
        for t in range(S5_TB):
            pr = p_ref[d, pl.ds(k * S5_TB + t, 1), 0:S5_HALF]
            pi = p_ref[d, pl.ds(k * S5_TB + t, 1), S5_HALF:S5_CPLX]
            mr, mi = _cmul(pr, pi, cr, ci)
            hc[t * S5_NLAT:(t + 1) * S5_NLAT, 0:S5_HALF] = mr.astype(jnp.bfloat16)
            hc[t * S5_NLAT:(t + 1) * S5_NLAT, S5_HALF:S5_CPLX] = mi.astype(jnp.bfloat16)
        corr = _bdot(hc[...], c_ref[d])
        ybuf[:, S5_LAT0:S5_LAT0 + S5_NLAT, :] += corr.reshape(S5_TB, S5_NLAT, SSM_WIDTH)

    y = ybuf[...].reshape(S5_TB * S5_ROWS, SSM_WIDTH)
    yg = 0.5 * y * (1.0 + jnp.tanh(math.sqrt(2.0 / math.pi) * (y + 0.044715 * (y * y * y))))
    z = _bdot(yg.astype(jnp.bfloat16), wg_ref[...]) + bg_ref[...]
    out = yg / (1.0 + jnp.exp(-z))
    o_ref[...] = out.reshape(S5_TB, S5_ROWS, SSM_WIDTH).astype(jnp.bfloat16)


def _s5_finish(u_tm3, y_dir4, powers, finals, h0, cblk, ssm_d, w_glu_bf, b_glu):
    blk3 = (S5_TB, S5_ROWS, SSM_WIDTH)
    return pl.pallas_call(
        _s5fin_kernel,
        grid=(S5_NBLK,),
        in_specs=[pl.BlockSpec(blk3, lambda k: (k, 0, 0)),
                  pl.BlockSpec((1,) + blk3, lambda k: (0, k, 0, 0)),
                  pl.BlockSpec((1,) + blk3, lambda k: (1, k, 0, 0)),
                  pl.BlockSpec((2, S5_CHUNK, S5_CPLX), lambda k: (0, 0, 0)),
                  pl.BlockSpec((2, S5_ROWS, S5_CPLX), lambda k: (0, 0, 0)),
                  pl.BlockSpec((2, DEC_BATCH, S5_CPLX), lambda k: (0, 0, 0)),
                  pl.BlockSpec((2, S5_CPLX, SSM_WIDTH), lambda k: (0, 0, 0)),
                  pl.BlockSpec((1, SSM_WIDTH), lambda k: (0, 0)),
                  pl.BlockSpec((SSM_WIDTH, SSM_WIDTH), lambda k: (0, 0)),
                  pl.BlockSpec((1, SSM_WIDTH), lambda k: (0, 0))],
        out_specs=pl.BlockSpec(blk3, lambda k: (k, 0, 0)),
        out_shape=jax.ShapeDtypeStruct((S5_CHUNK, S5_ROWS, SSM_WIDTH), jnp.bfloat16),
        scratch_shapes=[pltpu.VMEM((2, S5_NLAT, S5_CPLX), jnp.float32),
                        pltpu.VMEM((S5_TB * S5_NLAT, S5_CPLX), jnp.bfloat16),
                        pltpu.VMEM(blk3, jnp.float32)],
        compiler_params=_cparams("arbitrary"),
        name="s5_finish",
    )(u_tm3, y_dir4, y_dir4, powers, finals, h0, cblk, ssm_d, w_glu_bf, b_glu)


def _pair_attention(q, kvs, biases):
    lane = lax.broadcasted_iota(jnp.int32, (1, PAIR), 1)
    outs = []
    for a in range(2):
        sel = (lane < HEAD_DIM) if a == 0 else (lane >= HEAD_DIM)
        qa = jnp.where(sel, q, jnp.zeros_like(q))
        ss = []
        for (kk, _), bias in zip(kvs, biases):
            s = _bdot_t(qa, kk)
            if bias is not None:
                s = s + bias[a]
            ss.append(s)
        m = ss[0].max(axis=-1, keepdims=True)
        for s in ss[1:]:
            m = jnp.maximum(m, s.max(axis=-1, keepdims=True))
        den = None
        acc = None
        for s, (_, vv) in zip(ss, kvs):
            p = jnp.exp(s - m)
            l = p.sum(axis=-1, keepdims=True)
            o = _bdot(p.astype(jnp.bfloat16), vv)
            den = l if den is None else den + l
            acc = o if acc is None else acc + o
        outs.append(acc / den)
    return jnp.where(lane < HEAD_DIM, outs[0], outs[1])


def _pair_attention_vones(q, kvs):
    lane = lax.broadcasted_iota(jnp.int32, (1, PAIR), 1)
    outs = []
    for a in range(2):
        sel = (lane < HEAD_DIM) if a == 0 else (lane >= HEAD_DIM)
        qa = jnp.where(sel, q, jnp.zeros_like(q))
        ss = [_bdot_t(qa, kv[0]) for kv in kvs]
        m = ss[0].max(axis=-1, keepdims=True)
        for s in ss[1:]:
            m = jnp.maximum(m, s.max(axis=-1, keepdims=True))
        acc = None
        for s, kv in zip(ss, kvs):
            o = _bdot(jnp.exp((s - m).astype(jnp.bfloat16)), kv[1 + a])
            acc = o if acc is None else acc + o
        den = pltpu.roll(acc, HEAD_DIM, 1)
        outs.append(acc / den)
    return jnp.where(lane < HEAD_DIM, outs[0], outs[1])


ATTN_WIDTH = NA_WIDTH + GQA_WIDTH
NPAIR = NA_WIDTH // PAIR


def _attn_ctx_kernel(qkv_ref, o_ref):
    def blk(j):
        return qkv_ref[:, j * PAIR:(j + 1) * PAIR]

    for p in range(NPAIR):
        o = _pair_attention(blk(QN_BLK + p), [(blk(KN_BLK + p), blk(VN_BLK + p))], [None])
        o_ref[:, p * PAIR:(p + 1) * PAIR] = o.astype(jnp.bfloat16)
    for p in range(NPAIR):
        o = _pair_attention_vones(blk(QG_BLK + p), [(blk(KG_BLK), blk(VG1_BLK), blk(VG1_BLK + 1))])
        o_ref[:, (NPAIR + p) * PAIR:(NPAIR + p + 1) * PAIR] = o.astype(jnp.bfloat16)


def _attn_ctx(qkv):
    return pl.pallas_call(
        _attn_ctx_kernel,
        grid=(BATCH,),
        in_specs=[pl.BlockSpec((SEQ, QKV_WIDTH), lambda b: (b, 0))],
        out_specs=pl.BlockSpec((SEQ, ATTN_WIDTH), lambda b: (b, 0)),
        out_shape=jax.ShapeDtypeStruct((T_CTX, ATTN_WIDTH), jnp.bfloat16),
        compiler_params=_cparams("arbitrary"),
        name="attn_ctx",
    )(qkv)


TQ = 512


def _gqa_lat_kernel(q_ref, k_ref, v0_ref, v1_ref, ck_ref, cv_ref, o_ref):
    kvs = [(ck_ref[0], cv_ref[0, 0], cv_ref[0, 1]), (k_ref[...], v0_ref[...], v1_ref[...])]
    o_ref[...] = _pair_attention_vones(q_ref[...], kvs).astype(jnp.bfloat16)


def _gqa_lat(qkv, ck, cv_ones):
    nq = DEC_SEQ // TQ
    ctx_blocks = T_CTX // DEC_SEQ
    return pl.pallas_call(
        _gqa_lat_kernel,
        grid=(DEC_BATCH, NPAIR, nq),
        in_specs=[pl.BlockSpec((TQ, PAIR), lambda b, p, i: (T_CTX // TQ + b * nq + i, QG_BLK + p)),
                  pl.BlockSpec((DEC_SEQ, PAIR), lambda b, p, i: (ctx_blocks + b, KG_BLK)),
                  pl.BlockSpec((DEC_SEQ, PAIR), lambda b, p, i: (ctx_blocks + b, VG1_BLK)),
                  pl.BlockSpec((DEC_SEQ, PAIR), lambda b, p, i: (ctx_blocks + b, VG1_BLK + 1)),
                  pl.BlockSpec((1, PAST_LEN, PAIR), lambda b, p, i: (b, 0, 0)),
                  pl.BlockSpec((1, 2, PAST_LEN, PAIR), lambda b, p, i: (b, 0, 0, 0))],
        out_specs=pl.BlockSpec((TQ, PAIR), lambda b, p, i: (b * nq + i, p)),
        out_shape=jax.ShapeDtypeStruct((T_LAT, GQA_WIDTH), jnp.bfloat16),
        compiler_params=_cparams("arbitrary", "arbitrary", "arbitrary"),
        name="gqa_lat",
    )(qkv, qkv, qkv, qkv, ck, cv_ones)


NA_BLK = 4
NA_NBLK = GRID_H // NA_BLK
NA_SLAB_ROWS = NA_KH + NA_BLK - 1
NA_SLAB = NA_SLAB_ROWS * GRID_W
NA_QROWS = NA_BLK * GRID_W
NA_BLOCK_TYPES = (0, 2, NA_NBLK - 1)


def _na_slab_start(j):
    return jnp.clip(j * NA_BLK - NA_KH // 2, 0, GRID_H - NA_SLAB_ROWS)


def _na_lat_kernel(q_ref, k_ref, v_ref, ck_ref, cv_ref, bias_ref, o_ref):
    ck, cv = ck_ref[0, 0], cv_ref[0, 0]
    lane = lax.broadcasted_iota(jnp.int32, (1, PAIR), 1)

    def block(j, carry):
        q0 = pl.multiple_of(j * NA_QROWS, NA_QROWS)
        k0 = pl.multiple_of(_na_slab_start(j) * GRID_W, GRID_W)
        btype = jnp.where(j == 0, 0, jnp.where(j == NA_NBLK - 1, 2, 1))
        q = q_ref[pl.ds(q0, NA_QROWS), :]
        lhs = jnp.concatenate([jnp.where(lane < HEAD_DIM, q, jnp.zeros_like(q)),
                               jnp.where(lane >= HEAD_DIM, q, jnp.zeros_like(q))], axis=0)
        ks = k_ref[pl.ds(k0, NA_SLAB), :]
        vs = v_ref[pl.ds(k0, NA_SLAB), :]
        s_lat = _bdot_t(lhs, ks) + bias_ref[0, 0, btype]
        s_ctx = _bdot_t(lhs, ck)
        m = jnp.maximum(s_lat.max(axis=-1, keepdims=True), s_ctx.max(axis=-1, keepdims=True))
        p_lat = jnp.exp((s_lat - m).astype(jnp.bfloat16))
        p_ctx = jnp.exp((s_ctx - m).astype(jnp.bfloat16))
        den = (_bdot(p_lat, jnp.ones((NA_SLAB, PAIR), jnp.bfloat16))
               + _bdot(p_ctx, jnp.ones((PAST_LEN, PAIR), jnp.bfloat16)))
        o = (_bdot(p_lat, vs) + _bdot(p_ctx, cv)) / den
        out = jnp.where(lane < HEAD_DIM, o[0:NA_QROWS], o[NA_QROWS:2 * NA_QROWS])
        o_ref[pl.ds(q0, NA_QROWS), :] = out.astype(jnp.bfloat16)
        return carry

    lax.fori_loop(0, NA_NBLK, block, 0)


def _na_lat(layer, qkv, ck, cv, bias):
    ctx_blocks = T_CTX // DEC_SEQ
    return pl.pallas_call(
        _na_lat_kernel,
        grid=(DEC_BATCH, NPAIR),
        in_specs=[pl.BlockSpec((DEC_SEQ, PAIR), lambda b, p: (ctx_blocks + b, QN_BLK + p)),
                  pl.BlockSpec((DEC_SEQ, PAIR), lambda b, p: (ctx_blocks + b, KN_BLK + p)),
                  pl.BlockSpec((DEC_SEQ, PAIR), lambda b, p: (ctx_blocks + b, VN_BLK + p)),
                  pl.BlockSpec((1, 1, PAST_LEN, PAIR), lambda b, p: (b, p, 0, 0)),
                  pl.BlockSpec((1, 1, PAST_LEN, PAIR), lambda b, p: (b, p, 0, 0)),
                  pl.BlockSpec((1, 1, len(NA_BLOCK_TYPES), 2 * NA_QROWS, NA_SLAB),
                               lambda b, p: (layer, p, 0, 0, 0))],
        out_specs=pl.BlockSpec((DEC_SEQ, PAIR), lambda b, p: (b, p)),
        out_shape=jax.ShapeDtypeStruct((T_LAT, NA_WIDTH), jnp.bfloat16),
        compiler_params=_cparams("arbitrary", "arbitrary"),
        name="na_lat",
    )(qkv, qkv, qkv, ck, cv, bias)


def _na_bias_table(rpb):
    nt = len(NA_BLOCK_TYPES)
    nl = rpb.shape[0]
    j = np.array(NA_BLOCK_TYPES)[:, None]
    r = j * NA_BLK + np.arange(NA_BLK)[None, :]
    slab0 = np.clip(j * NA_BLK - NA_KH // 2, 0, GRID_H - NA_SLAB_ROWS)
    win0 = np.clip(r - NA_KH // 2, 0, GRID_H - NA_KH)
    key_row = slab0[:, :, None] + np.arange(NA_SLAB_ROWS)[None, None, :]
    row_in = (key_row >= win0[:, :, None]) & (key_row < win0[:, :, None] + NA_KH)
    dr = np.clip(key_row - r[:, :, None] + (NA_KH - 1), 0, 2 * NA_KH - 2)
    cq = np.arange(GRID_W)
    ck = np.arange(GRID_W)
    cs = np.clip(cq - NA_KW // 2, 0, GRID_W - NA_KW)
    col_in = (ck[None, :] >= cs[:, None]) & (ck[None, :] < cs[:, None] + NA_KW)
    dc = np.clip(ck[None, :] - cq[:, None] + NA_KW - 1, 0, 2 * NA_KW - 2)
    sel_c = jnp.asarray(np.eye(2 * NA_KW - 1, dtype=np.float32)[dc])
    cols = jnp.einsum("lhdc,qkc->lhdqk", rpb.astype(jnp.float32), sel_c, precision=lax.Precision.HIGHEST)
    cols = jnp.where(jnp.asarray(col_in), cols, NEG_INF)
    n_off = 2 * NA_KH - 1
    cols = cols.reshape(nl, NA_HEADS // 2, 2, n_off, GRID_W, GRID_W)
    masked = jnp.full((nl, NA_HEADS // 2, 2, 1, GRID_W, GRID_W), NEG_INF, jnp.float32)
    cols = jnp.concatenate([cols, masked], axis=3)
    which = jnp.asarray(np.where(row_in, dr, n_off).reshape(-1), jnp.int32)

    def assemble(which_ref, cols_ref, o_ref):
        t = pl.program_id(2)
        for e in range(2):
            for n in range(NA_BLK):
                r0 = (e * NA_BLK + n) * GRID_W
                for i in range(NA_SLAB_ROWS):
                    d = which_ref[(t * NA_BLK + n) * NA_SLAB_ROWS + i]
                    o_ref[0, 0, 0, r0:r0 + GRID_W, i * GRID_W:(i + 1) * GRID_W] = cols_ref[0, 0, e, d]

    grid_spec = pltpu.PrefetchScalarGridSpec(
        num_scalar_prefetch=1,
        grid=(nl, NA_HEADS // 2, nt),
        in_specs=[pl.BlockSpec((1, 1, 2, n_off + 1, GRID_W, GRID_W), lambda l, p, t, w: (l, p, 0, 0, 0, 0))],
        out_specs=pl.BlockSpec((1, 1, 1, 2 * NA_QROWS, NA_SLAB), lambda l, p, t, w: (l, p, t, 0, 0)))
    return pl.pallas_call(
        assemble,
        grid_spec=grid_spec,
        out_shape=jax.ShapeDtypeStruct((nl, NA_HEADS // 2, nt, 2 * NA_QROWS, NA_SLAB), jnp.float32),
        compiler_params=_cparams("arbitrary", "arbitrary", "arbitrary"),
        name="na_bias",
    )(which, cols)


GATE_LANE0 = N_GROUPS


def _router_logits(h2, wr_hi, wr_lo, br):
    h_hi = h2.astype(jnp.bfloat16)
    h_lo = (h2 - h_hi.astype(jnp.float32)).astype(jnp.bfloat16)
    return _bdot(h_hi, wr_hi) + _bdot(h_lo, wr_hi) + _bdot(h_hi, wr_lo) + br


def _top_group(logits):
    lane_i = lax.broadcasted_iota(jnp.int32, logits.shape, 1)
    is_g = lane_i < N_GROUPS
    lg = jnp.where(is_g, logits, NEG_INF)
    mg = lg.max(axis=-1, keepdims=True)
    return jnp.where(is_g & (lg == mg), lane_i.astype(jnp.float32), float(LANES)).min(axis=-1, keepdims=True)


def _gates_in_group(logits, group):
    lane_i = lax.broadcasted_iota(jnp.int32, logits.shape, 1)
    lane = lane_i.astype(jnp.float32)
    big = float(LANES)
    is_g = lane_i < N_GROUPS
    lg = jnp.where(is_g, logits, NEG_INF)
    mg = lg.max(axis=-1, keepdims=True)
    own = jnp.where(lane_i == group, logits, 0.0).sum(axis=-1, keepdims=True)
    pg_top = jnp.exp(own - mg) / jnp.where(is_g, jnp.exp(lg - mg), 0.0).sum(axis=-1, keepdims=True)
    lane0 = GATE_LANE0 + group * EXPERTS_PER_GROUP
    sel = (lane_i >= lane0) & (lane_i < lane0 + EXPERTS_PER_GROUP)
    le = jnp.where(sel, logits, NEG_INF)
    me = le.max(axis=-1, keepdims=True)
    ex = jnp.where(sel, jnp.exp(le - me), 0.0)
    pe = ex / ex.sum(axis=-1, keepdims=True)
    p1 = pe.max(axis=-1, keepdims=True)
    i1 = jnp.where(sel & (pe == p1), lane, big).min(axis=-1, keepdims=True)
    rest = sel & (lane != i1)
    pr = jnp.where(rest, pe, -1.0)
    p2 = pr.max(axis=-1, keepdims=True)
    i2 = jnp.where(rest & (pr == p2), lane, big).min(axis=-1, keepdims=True)
    tot = p1 + p2
    within = jnp.where(lane == i1, p1 / tot, 0.0) + jnp.where(lane == i2, p2 / tot, 0.0)
    return pg_top * within


TILE_ROWS = D_MODEL // LANES


def _store_token_tiles(ref, x):
    n = x.shape[0]
    for c in range(TILE_ROWS):
        ref[pl.ds(c, n, stride=TILE_ROWS), :] = x[:, c * LANES:(c + 1) * LANES]


def _load_token_tiles(ref, n):
    return jnp.concatenate([ref[pl.ds(c, n, stride=TILE_ROWS), :] for c in range(TILE_ROWS)], axis=1)


def _outproj_kernel(x_ref, ys_ref, yc_ref, yn_ref, yg_ref, mod_ref, g_ref, w_ref, wr_hi_ref, wr_lo_ref, br_ref,
                    x1_ref, h2t_ref, grp_ref):
    is_ctx = pl.program_id(0) < T_CTX // TM
    y_na = jnp.where(is_ctx, yc_ref[:, 0:NA_WIDTH], yn_ref[...])
    y_g = jnp.where(is_ctx, yc_ref[:, NA_WIDTH:ATTN_WIDTH], yg_ref[...])
    mix = (_bdot(ys_ref[...], w_ref[0:SSM_WIDTH, :])
           + _bdot(y_na, w_ref[SSM_WIDTH:SSM_WIDTH + NA_WIDTH, :])
           + _bdot(y_g, w_ref[SSM_WIDTH + NA_WIDTH:, :]))
    gate1 = mod_ref[0, :, 2 * D_MODEL:3 * D_MODEL]
    shift2 = mod_ref[0, :, 3 * D_MODEL:4 * D_MODEL]
    scale2 = mod_ref[0, :, 4 * D_MODEL:5 * D_MODEL]
    x1 = x_ref[...] + gate1 * mix
    x1_ref[...] = x1
    ms = jnp.mean(x1 * x1, axis=-1, keepdims=True)
    h2 = (x1 * lax.rsqrt(ms + EPS) * g_ref[...]) * (1.0 + scale2) + shift2
    _store_token_tiles(h2t_ref, h2)
    logits = _router_logits(h2, wr_hi_ref[0], wr_lo_ref[0], br_ref[0])
    grp_ref[...] = jnp.broadcast_to(_top_group(logits), (TM, LANES))


def _out_proj(layer, x, y_ssm, y_ctx, y_na_lat, y_g_lat, mods3, norm_g, w_out_bf, wr_hi, wr_lo, b_router):
    n_ctx = T_CTX // TM
    return pl.pallas_call(
        _outproj_kernel,
        grid=(T_ALL // TM,),
        in_specs=[pl.BlockSpec((TM, D_MODEL), lambda i: (i, 0)),
                  pl.BlockSpec((TM, SSM_WIDTH), lambda i: (i, 0)),
                  pl.BlockSpec((TM, ATTN_WIDTH), lambda i: (jnp.minimum(i, n_ctx - 1), 0)),
                  pl.BlockSpec((TM, NA_WIDTH), lambda i: (jnp.maximum(i - n_ctx, 0), 0)),
                  pl.BlockSpec((TM, GQA_WIDTH), lambda i: (jnp.maximum(i - n_ctx, 0), 0)),
                  pl.BlockSpec((1, 1, 6 * D_MODEL), lambda i: (_modset(i), 0, 0)),
                  pl.BlockSpec((1, D_MODEL), lambda i: (0, 0)),
                  pl.BlockSpec((D_MODEL, D_MODEL), lambda i: (0, 0)),
                  pl.BlockSpec((1, D_MODEL, LANES), lambda i: (layer, 0, 0)),
                  pl.BlockSpec((1, D_MODEL, LANES), lambda i: (layer, 0, 0)),
                  pl.BlockSpec((1, 1, LANES), lambda i: (layer, 0, 0))],
        out_specs=[pl.BlockSpec((TM, D_MODEL), lambda i: (i, 0)),
                   pl.BlockSpec((TM * TILE_ROWS, LANES), lambda i: (i, 0)),
                   pl.BlockSpec((TM, LANES), lambda i: (i, 0))],
        out_shape=(jax.ShapeDtypeStruct((T_ALL, D_MODEL), jnp.float32),
                   jax.ShapeDtypeStruct((T_ALL * TILE_ROWS, LANES), jnp.float32),
                   jax.ShapeDtypeStruct((T_ALL, LANES), jnp.float32)),
        compiler_params=_cparams("arbitrary"),
        name="out_proj",
    )(x, y_ssm, y_ctx, y_na_lat, y_g_lat, mods3, norm_g, w_out_bf, wr_hi, wr_lo, b_router)


TMOE = 256
NT_MOE = T_ALL // TMOE + N_GROUPS
MOE_SLOTS = NT_MOE * TMOE


def _moe_dispatch(group_f):
    i32 = jnp.int32
    g = group_f.astype(i32)
    onehot = (g[:, None] == jnp.arange(N_GROUPS, dtype=i32)[None, :]).astype(i32)
    csum = jnp.cumsum(onehot, axis=0)
    counts = csum[-1]
    rank = jnp.sum(onehot * csum, axis=1) - 1
    tiles = (counts + TMOE - 1) // TMOE
    tile_end = jnp.cumsum(tiles)
    tile_start = tile_end - tiles
    slot = jnp.sum(onehot * tile_start[None, :], axis=1) * TMOE + rank
    tok = jnp.full((MOE_SLOTS,), -1, i32).at[slot].set(jnp.arange(T_ALL, dtype=i32))
    pad = tok < 0
    pad_rank = jnp.cumsum(pad.astype(i32)) - 1
    row_of_slot = jnp.where(pad, T_ALL + pad_rank, tok) * TILE_ROWS
    tile_group = jnp.minimum(
        jnp.sum(jnp.arange(NT_MOE, dtype=i32)[:, None] >= tile_end[None, :], axis=1), N_GROUPS - 1).astype(i32)
    return tile_group, row_of_slot


def _moe_kernel(tg_ref, row_ref, h_hbm, wr_hi_ref, wr_lo_ref, br_ref, wg_ref, wu_ref, wd_ref, y_hbm,
                gbuf, ybuf, wg_bf, wu_bf, wd_bf, gsem, ssem):
    i = pl.program_id(0)
    slot = lax.rem(i, 2)
    other = 1 - slot
    last = NT_MOE - 1
    nxt = jnp.minimum(i + 1, last)
    prv = jnp.maximum(i - 1, 0)
    buf_rows = TMOE * TILE_ROWS

    def gather_tok(row, buf_slot, r):
        return pltpu.make_async_copy(h_hbm.at[pl.ds(row, TILE_ROWS)],
                                     gbuf.at[buf_slot, pl.ds(r * TILE_ROWS, TILE_ROWS)], gsem.at[buf_slot])

    def scatter_tok(row, buf_slot, r):
        return pltpu.make_async_copy(ybuf.at[buf_slot, pl.ds(r * TILE_ROWS, TILE_ROWS)],
                                     y_hbm.at[pl.ds(row, TILE_ROWS)], ssem.at[buf_slot])

    def wait_gather(buf_slot):
        pltpu.make_async_copy(h_hbm.at[pl.ds(0, buf_rows)], gbuf.at[buf_slot], gsem.at[buf_slot]).wait()

    def wait_scatter(buf_slot):
        pltpu.make_async_copy(ybuf.at[buf_slot], y_hbm.at[pl.ds(0, buf_rows)], ssem.at[buf_slot]).wait()

    def start_gather_tok(tile, buf_slot, r):
        row = row_ref[tile * TMOE + r]
        gather_tok(jnp.where(row < T_ALL * TILE_ROWS, row, 0), buf_slot, r).start()

    @pl.when(i == 0)
    def _():
        ybuf[1] = jnp.zeros((buf_rows, LANES), jnp.float32)
        for r in range(TMOE):
            start_gather_tok(0, 0, r)

    first_of_group = (i == 0) | (tg_ref[i] != tg_ref[prv])

    @pl.when(first_of_group)
    def _():
        wg_bf[...] = wg_ref[0].astype(jnp.bfloat16)
        wu_bf[...] = wu_ref[0].astype(jnp.bfloat16)
        wd_bf[...] = wd_ref[0].astype(jnp.bfloat16)

    wait_gather(slot)

    h2 = _load_token_tiles(gbuf.at[slot], TMOE)
    h = h2.astype(jnp.bfloat16)
    routing = _gates_in_group(_router_logits(h2, wr_hi_ref[0], wr_lo_ref[0], br_ref[0]), tg_ref[i])
    lane = lax.broadcasted_iota(jnp.int32, (1, LANES), 1)
    lane0 = GATE_LANE0 + tg_ref[i] * EXPERTS_PER_GROUP
    rows_per_chunk = TMOE // EXPERTS_PER_GROUP
    y = None
    for j in range(EXPERTS_PER_GROUP):
        for r in range(j * rows_per_chunk, (j + 1) * rows_per_chunk):
            start_gather_tok(nxt, other, r)
            dst = jnp.where(i > 0, row_ref[prv * TMOE + r], (MOE_SLOTS + r) * TILE_ROWS)
            scatter_tok(dst, other, r).start()
        gate = jnp.where(lane == lane0 + j, routing, 0.0).sum(axis=-1, keepdims=True)
        aj = _bdot(h, wg_bf[j])
        bj = _bdot(h, wu_bf[j])
        act = ((aj / (1.0 + jnp.exp(-aj))) * bj * gate).astype(jnp.bfloat16)
        yj = _bdot(act, wd_bf[j])
        y = yj if y is None else y + yj

    @pl.when(i > 0)
    def _():
        wait_scatter(slot)

    _store_token_tiles(ybuf.at[slot], y)

    @pl.when(i == last)
    def _():
        for r in range(TMOE):
            scatter_tok(row_ref[last * TMOE + r], slot, r).start()
        wait_scatter(other)
        wait_scatter(slot)
        wait_gather(other)


def _moe(layer, h2_tiles, tile_group, row_of_slot, wr_hi, wr_lo, b_router, w_gate, w_up, w_down):
    def wspec(shape):
        return pl.BlockSpec((1, EXPERTS_PER_GROUP) + shape, lambda i, tg, rows: (layer, tg[i], 0, 0))

    def rspec(rows):
        return pl.BlockSpec((1, rows, LANES), lambda i, tg, rows_: (layer, 0, 0))

    grid_spec = pltpu.PrefetchScalarGridSpec(
        num_scalar_prefetch=2,
        grid=(NT_MOE,),
        in_specs=[pl.BlockSpec(memory_space=pl.ANY),
                  rspec(D_MODEL), rspec(D_MODEL), rspec(1),
                  wspec((D_MODEL, EXPERT_HIDDEN)), wspec((D_MODEL, EXPERT_HIDDEN)),
                  wspec((EXPERT_HIDDEN, D_MODEL))],
        out_specs=pl.BlockSpec(memory_space=pl.ANY),
        scratch_shapes=[pltpu.VMEM((2, TMOE * TILE_ROWS, LANES), jnp.float32),
                        pltpu.VMEM((2, TMOE * TILE_ROWS, LANES), jnp.float32),
                        pltpu.VMEM((EXPERTS_PER_GROUP, D_MODEL, EXPERT_HIDDEN), jnp.bfloat16),
                        pltpu.VMEM((EXPERTS_PER_GROUP, D_MODEL, EXPERT_HIDDEN), jnp.bfloat16),
                        pltpu.VMEM((EXPERTS_PER_GROUP, EXPERT_HIDDEN, D_MODEL), jnp.bfloat16),
                        pltpu.SemaphoreType.DMA((2,)),
                        pltpu.SemaphoreType.DMA((2,))])
    return pl.pallas_call(
        _moe_kernel,
        grid_spec=grid_spec,
        out_shape=jax.ShapeDtypeStruct(((MOE_SLOTS + TMOE) * TILE_ROWS, LANES), jnp.float32),
        compiler_params=_cparams("arbitrary"),
        name="moe",
    )(tile_group, row_of_slot, h2_tiles, wr_hi, wr_lo, b_router, w_gate, w_up, w_down)


def _final_kernel(x1_ref, y_ref, mod_ref, g_ref, o_ref):
    x = x1_ref[...] + mod_ref[0, :, 5 * D_MODEL:6 * D_MODEL] * _load_token_tiles(y_ref, TM)
    ms = jnp.mean(x * x, axis=-1, keepdims=True)
    o_ref[...] = x * lax.rsqrt(ms + EPS) * g_ref[...]


def _final_norm(x1, y, mods3, g):
    return pl.pallas_call(
        _final_kernel,
        grid=(T_ALL // TM,),
        in_specs=[pl.BlockSpec((TM, D_MODEL), lambda i: (i, 0)),
                  pl.BlockSpec((TM * TILE_ROWS, LANES), lambda i: (i, 0)),
                  pl.BlockSpec((1, 1, 6 * D_MODEL), lambda i: (_modset(i), 0, 0)),
                  pl.BlockSpec((1, D_MODEL), lambda i: (0, 0))],
        out_specs=pl.BlockSpec((TM, D_MODEL), lambda i: (i, 0)),
        out_shape=jax.ShapeDtypeStruct((T_ALL, D_MODEL), jnp.float32),
        compiler_params=_cparams("arbitrary"),
        name="final_norm",
    )(x1, y, mods3, g)


_QG_ORDER = np.array([0, 3, 1, 4, 2, 5])


def _permute_qg_columns(w_in):
    c1 = SSM_WIDTH + 3 * NA_WIDTH
    qg = w_in[..., c1:c1 + GQA_WIDTH].reshape(*w_in.shape[:-1], GQA_HEADS, HEAD_DIM)
    qg = qg[..., _QG_ORDER, :].reshape(*w_in.shape[:-1], GQA_WIDTH)
    return jnp.concatenate([w_in[..., :c1], qg, w_in[..., c1 + GQA_WIDTH:]], axis=-1)


def _permute_yg_rows(w_out):
    r0 = SSM_WIDTH + NA_WIDTH
    yg = w_out[:, r0:, :].reshape(DEPTH, GQA_HEADS, HEAD_DIM, D_MODEL)[:, _QG_ORDER]
    return jnp.concatenate([w_out[:, :r0, :], yg.reshape(DEPTH, GQA_WIDTH, D_MODEL)], axis=1)


def _rope_tables():
    t = jnp.arange(DEC_SEQ)
    row = (t // GRID_W).astype(jnp.float32)
    col = (t % GRID_W).astype(jnp.float32)
    freqs = ROPE_THETA ** (-jnp.arange(ROPE_F, dtype=jnp.float32) / ROPE_F)
    ang_r = row[:, None] * freqs
    ang_c = col[:, None] * freqs
    ang = jnp.concatenate([ang_r, ang_r, ang_c, ang_c], axis=1)
    cos = jnp.cos(ang)
    sin = jnp.sin(ang)
    sign = jnp.tile(jnp.concatenate([-jnp.ones(ROPE_F), jnp.ones(ROPE_F)]), 2).astype(jnp.float32)
    sin = sin * sign
    cos = jnp.tile(cos, (1, 2))
    sin = jnp.tile(sin, (1, 2))
    cos = jnp.concatenate([jnp.ones((TM, LANES), jnp.float32), cos], axis=0)
    sin = jnp.concatenate([jnp.zeros((TM, LANES), jnp.float32), sin], axis=0)
    return cos, sin


def _pair_layout(cache):
    b, l, h, n, dh = cache.shape
    c = cache.reshape(b, l, h // 2, 2, n, dh).transpose(1, 0, 2, 4, 3, 5)
    return c.reshape(l, b, h // 2, n, 2 * dh).astype(jnp.bfloat16)


def _block_diag_b(bbar):
    bb = bbar.reshape(DEPTH, 2, SSM_GROUP_CH, SSM_GROUPS, SSM_STATE)
    eye = jnp.eye(SSM_GROUPS, dtype=bbar.dtype)
    out = eye[None, None, :, None, :, None] * bb.transpose(0, 1, 3, 2, 4)[:, :, :, :, None, :]
    return out.reshape(DEPTH, 2, SSM_WIDTH, S5_HALF)


def _block_diag_c(c):
    eye = jnp.eye(SSM_GROUPS, dtype=c.dtype)
    out = eye[None, None, :, None, :, None] * c.transpose(0, 1, 2, 4, 3)[:, :, :, :, None, :]
    return out.reshape(DEPTH, 2, S5_HALF, SSM_WIDTH)


def kernel(x_prompt, x_sample, cache_na_k, cache_na_v, cache_gqa_k, cache_gqa_v, state_ssm, c, c_ctx, norm_mix_g, norm_ffn_g, w_ada, b_ada, w_in, ssm_a_re, ssm_a_im, ssm_log_dt, ssm_b_re, ssm_b_im, ssm_c_re, ssm_c_im, ssm_d, w_glu, b_glu, na_rpb, q_norm_g, k_norm_g, w_out, moe_w_group, moe_b_group, moe_w_expert, moe_b_expert, moe_w_gate, moe_w_up, moe_w_down, final_norm_g):
    f32, bf16 = jnp.float32, jnp.bfloat16
    x = jnp.concatenate([x_prompt.reshape(T_CTX, D_MODEL), x_sample.reshape(T_LAT, D_MODEL)], axis=0)

    cvec8 = jnp.concatenate([c_ctx[None, :], c, jnp.zeros((SUBLANES - N_MODSETS, D_MODEL), f32)], axis=0)
    mods_all = _adaln_all(cvec8, w_ada, b_ada)

    w_in_bf = _permute_qg_columns(w_in).astype(bf16)
    w_out_bf = _permute_yg_rows(w_out).astype(bf16)
    w_glu_bf = w_glu.astype(bf16)
    w_router = jnp.concatenate(
        [moe_w_group, moe_w_expert, jnp.zeros((DEPTH, D_MODEL, LANES - N_GROUPS - N_EXPERTS), f32)], axis=-1)
    wr_hi = w_router.astype(bf16)
    wr_lo = (w_router - wr_hi.astype(f32)).astype(bf16)
    b_router = jnp.concatenate(
        [moe_b_group, moe_b_expert, jnp.zeros((DEPTH, LANES - N_GROUPS - N_EXPERTS), f32)], axis=-1)[:, None, :]
    q_gain = jnp.tile(q_norm_g, (1, 2))
    k_gain = jnp.tile(k_norm_g, (1, 2))
    lane = np.arange(LANES)
    gsum = jnp.asarray((lane[:, None] // HEAD_DIM) == (lane[None, :] // HEAD_DIM), bf16)
    rope_cos, rope_sin = _rope_tables()

    abar_re, abar_im, bbar_re, bbar_im = _s5_prep(ssm_a_re, ssm_a_im, ssm_log_dt, ssm_b_re, ssm_b_im)
    abar = jnp.concatenate([abar_re, abar_im], axis=-1).reshape(DEPTH, 2, 1, S5_CPLX)
    bblk = jnp.concatenate([_block_diag_b(bbar_re), _block_diag_b(bbar_im)], axis=-1).astype(bf16)
    cblk = jnp.concatenate([_block_diag_c(ssm_c_re), -_block_diag_c(ssm_c_im)], axis=2).astype(bf16)
    h0 = state_ssm.transpose(1, 2, 0, 5, 3, 4).reshape(DEPTH, 2, DEC_BATCH, S5_CPLX)

    ck_na, cv_na = _pair_layout(cache_na_k), _pair_layout(cache_na_v)
    ck_g, cv_g = _pair_layout(cache_gqa_k)[:, :, 0], _pair_layout(cache_gqa_v)[:, :, 0]
    first_half = jnp.asarray(np.arange(LANES) < HEAD_DIM)
    cv_g = jnp.stack([jnp.where(first_half, cv_g, 1.0), jnp.where(first_half, 1.0, cv_g)],
                     axis=2).astype(bf16)
    na_bias = _na_bias_table(na_rpb)

    kv32_l, fin_l = [], []
    moe_residual = None
    for l in range(DEPTH):
        mods3 = mods_all[l, :N_MODSETS].reshape(N_MODSETS, 1, 6 * D_MODEL)
        u, qkv, kv32, x = _in_proj(x, moe_residual, mods3, norm_mix_g[l][None], w_in_bf[l], q_gain[l][None],
                                   k_gain[l][None], gsum, rope_cos, rope_sin)
        kv32_l.append(kv32)

        u_tm = jnp.pad(u.reshape(S5_SEQS, S5_CHUNK, SSM_WIDTH).transpose(1, 0, 2),
                       ((0, 0), (0, S5_ROWS - S5_SEQS), (0, 0)))
        y_dir, powers, finals = _s5_scan(u_tm.reshape(S5_CHUNK * S5_ROWS, SSM_WIDTH), bblk[l], abar[l], cblk[l])
        fin_l.append(finals[:, :BATCH])
        y_tm = _s5_finish(u_tm, y_dir.reshape(2, S5_CHUNK, S5_ROWS, SSM_WIDTH), powers, finals, h0[l], cblk[l],
                          ssm_d[l][None], w_glu_bf[l], b_glu[l][None])
        y_ssm = y_tm[:, :S5_SEQS].transpose(1, 0, 2).reshape(T_ALL, SSM_WIDTH)

        y_ctx = _attn_ctx(qkv)
        y_g_lat = _gqa_lat(qkv, ck_g[l], cv_g[l])
        y_na_lat = _na_lat(l, qkv, ck_na[l], cv_na[l], na_bias)

        x1, h2_tiles, group = _out_proj(l, x, y_ssm, y_ctx, y_na_lat, y_g_lat, mods3, norm_ffn_g[l][None],
                                        w_out_bf[l], wr_hi, wr_lo, b_router)
        tile_group, row_of_slot = _moe_dispatch(group[:, 0])
        y_moe = _moe(l, h2_tiles, tile_group, row_of_slot, wr_hi, wr_lo, b_router,
                     moe_w_gate, moe_w_up, moe_w_down)
        x, moe_residual = x1, (y_moe, mods3)

    y = _final_norm(x, moe_residual[0], moe_residual[1], final_norm_g[None])
    y_prompt = y[:T_CTX].reshape(BATCH, SEQ, D_MODEL)
    y_sample = y[T_CTX:].reshape(DEC_BATCH, DEC_SEQ, D_MODEL)

    kv32 = jnp.stack(kv32_l, axis=0).reshape(DEPTH, BATCH, SEQ, KV32_WIDTH)

    def heads(t, n_heads):
        return t.reshape(DEPTH, BATCH, SEQ, n_heads, HEAD_DIM).transpose(1, 0, 3, 2, 4)

    new_na_k = heads(kv32[..., 0:NA_WIDTH], NA_HEADS)
    new_na_v = heads(kv32[..., NA_WIDTH:2 * NA_WIDTH], NA_HEADS)
    new_gqa_k = heads(kv32[..., 2 * NA_WIDTH:2 * NA_WIDTH + KV_WIDTH], GQA_KV_HEADS)
    new_gqa_v = heads(kv32[..., 2 * NA_WIDTH + KV_WIDTH:], GQA_KV_HEADS)
    fin = jnp.stack(fin_l, axis=0).reshape(DEPTH, 2, BATCH, 2, SSM_GROUPS, SSM_STATE)
    new_state = fin.transpose(2, 0, 1, 4, 5, 3)
    return (y_prompt, y_sample, new_na_k, new_na_v, new_gqa_k, new_gqa_v, new_state)
```

```python
import functools
import math

import jax
import jax.numpy as jnp
import numpy as np
from jax import lax
from jax.experimental import pallas as pl
from jax.experimental.pallas import tpu as pltpu

D_MODEL = 1024
BATCH = 16
SEQ = 256
DEPTH = 4
DEC_BATCH = 2
DEC_SEQ = 4096
PAST_LEN = 512
GRID_W = 64
GRID_H = DEC_SEQ // GRID_W
HEAD_DIM = 64
SSM_WIDTH = 256
SSM_GROUP_CH = 16
SSM_GROUPS = 16
SSM_STATE = 64
NA_HEADS = 6
NA_WIDTH = 384
NA_KH = 8
NA_KW = 16
GQA_HEADS = 6
GQA_KV_HEADS = 2
GQA_REP = 3
GQA_WIDTH = 384
KV_WIDTH = 128
IN_WIDTH = 2048
ROPE_THETA = 10000.0
ROPE_F = 16
N_GROUPS = 4
EXPERTS_PER_GROUP = 4
N_EXPERTS = 16
EXPERT_HIDDEN = 256
EPS = 1e-6
NEG_INF = -1e30

T_CTX = BATCH * SEQ
T_LAT = DEC_BATCH * DEC_SEQ
T_ALL = T_CTX + T_LAT
N_MODSETS = 1 + DEC_BATCH
LANES = 128
SUBLANES = 8
PAIR = 2 * HEAD_DIM
ATTN_SCALE = HEAD_DIM ** -0.5

S5_CHUNK = 256
S5_SEQS = T_ALL // S5_CHUNK
S5_ROWS = 56
S5_POW_ROW = 48
S5_LAT0 = BATCH
S5_NLAT = S5_SEQS - BATCH
S5_TB = 16
S5_NBLK = S5_CHUNK // S5_TB
S5_CPLX = 2 * SSM_GROUPS * SSM_STATE
S5_HALF = SSM_GROUPS * SSM_STATE

TM = 512
VMEM_LIMIT = 56 * 1024 * 1024


def _cparams(*sem):
    return pltpu.CompilerParams(dimension_semantics=sem, vmem_limit_bytes=VMEM_LIMIT)


def _modset(i):
    return jnp.where(i < T_CTX // TM, 0, 1 + (i - T_CTX // TM) // (DEC_SEQ // TM))


def _bdot(a, b):
    return jnp.dot(a, b, preferred_element_type=jnp.float32)


def _bdot_t(a, b):
    return lax.dot_general(a, b, (((1,), (1,)), ((), ())), preferred_element_type=jnp.float32)


def _ada_kernel(c_ref, w_ref, b_ref, o_ref):
    c = c_ref[...]
    s = c / (1.0 + jnp.exp(-c))
    o_ref[0] = _bdot(s.astype(jnp.bfloat16), w_ref[0].astype(jnp.bfloat16)) + b_ref[0]


def _adaln_all(cvec8, w_ada, b_ada):
    tn = 1024
    return pl.pallas_call(
        _ada_kernel,
        grid=(DEPTH, 6 * D_MODEL // tn),
        in_specs=[pl.BlockSpec((SUBLANES, D_MODEL), lambda l, j: (0, 0)),
                  pl.BlockSpec((1, D_MODEL, tn), lambda l, j: (l, 0, j)),
                  pl.BlockSpec((1, 1, tn), lambda l, j: (l, 0, j))],
        out_specs=pl.BlockSpec((1, SUBLANES, tn), lambda l, j: (l, 0, j)),
        out_shape=jax.ShapeDtypeStruct((DEPTH, SUBLANES, 6 * D_MODEL), jnp.float32),
        compiler_params=_cparams("arbitrary", "arbitrary"),
        name="adaln",
    )(cvec8, w_ada, b_ada.reshape(DEPTH, 1, 6 * D_MODEL))


def _s5prep_kernel(ar_ref, ai_ref, ldt_ref, br_ref, bi_ref, abar_re, abar_im, bbar_re, bbar_im):
    ar, ai = ar_ref[0], ai_ref[0]
    dt = jnp.exp(ldt_ref[0])
    mag = jnp.exp(ar * dt)
    are = mag * jnp.cos(ai * dt)
    aim = mag * jnp.sin(ai * dt)
    zr, zi = are - 1.0, aim
    den = ar * ar + ai * ai
    cre = (zr * ar + zi * ai) / den
    cim = (zi * ar - zr * ai) / den
    abar_re[0] = are
    abar_im[0] = aim
    br, bi = br_ref[0], bi_ref[0]
    bbar_re[0] = cre * br - cim * bi
    bbar_im[0] = cre * bi + cim * br


def _s5_prep(a_re, a_im, log_dt, b_re, b_im):
    n = DEPTH * 2
    ar = a_re.reshape(n, 1, S5_HALF)
    ai = a_im.reshape(n, 1, S5_HALF)
    ldt = jnp.broadcast_to(log_dt[..., None], (DEPTH, 2, SSM_GROUPS, SSM_STATE)).reshape(n, 1, S5_HALF)
    br = b_re.transpose(0, 1, 4, 2, 3).reshape(n, SSM_GROUP_CH, S5_HALF)
    bi = b_im.transpose(0, 1, 4, 2, 3).reshape(n, SSM_GROUP_CH, S5_HALF)
    vec = jax.ShapeDtypeStruct((n, 1, S5_HALF), jnp.float32)
    mat = jax.ShapeDtypeStruct((n, SSM_GROUP_CH, S5_HALF), jnp.float32)
    vspec = pl.BlockSpec((1, 1, S5_HALF), lambda j: (j, 0, 0))
    mspec = pl.BlockSpec((1, SSM_GROUP_CH, S5_HALF), lambda j: (j, 0, 0))
    return pl.pallas_call(
        _s5prep_kernel,
        grid=(n,),
        in_specs=[vspec, vspec, vspec, mspec, mspec],
        out_specs=[vspec, vspec, mspec, mspec],
        out_shape=(vec, vec, mat, mat),
        compiler_params=_cparams("arbitrary"),
        name="s5prep",
    )(ar, ai, ldt, br, bi)


def _head_rms(blk, gsum, gain):
    sq = blk * blk
    hi = sq.astype(jnp.bfloat16)
    lo = (sq - hi.astype(jnp.float32)).astype(jnp.bfloat16)
    ms = (_bdot(hi, gsum) + _bdot(lo, gsum)) * (1.0 / HEAD_DIM)
    return blk * lax.rsqrt(ms + EPS) * gain


def _rope(blk, cos, sin_signed, first_half):
    partner = jnp.where(first_half, pltpu.roll(blk, LANES - ROPE_F, 1), pltpu.roll(blk, ROPE_F, 1))
    return blk * cos + partner * sin_signed


def _inproj_kernel(has_moe_residual, *refs):
    if has_moe_residual:
        (x_ref, y_ref, pmod_ref, mod_ref, g_ref, w_ref, qg_ref, kg_ref, gsum_ref, cos_ref, sin_ref,
         u_ref, qkv_ref, kv32_ref, xo_ref) = refs
        x = x_ref[...] + pmod_ref[0, :, 5 * D_MODEL:6 * D_MODEL] * _load_token_tiles(y_ref, TM)
        xo_ref[...] = x
    else:
        (x_ref, mod_ref, g_ref, w_ref, qg_ref, kg_ref, gsum_ref, cos_ref, sin_ref,
         u_ref, qkv_ref, kv32_ref) = refs
        x = x_ref[...]
    i = pl.program_id(0)
    shift = mod_ref[0, :, 0:D_MODEL]
    scale = mod_ref[0, :, D_MODEL:2 * D_MODEL]
    ms = jnp.mean(x * x, axis=-1, keepdims=True)
    h = (x * lax.rsqrt(ms + EPS) * g_ref[...]) * (1.0 + scale) + shift
    proj = _bdot(h.astype(jnp.bfloat16), w_ref[...])
    u_ref[...] = proj[:, 0:SSM_WIDTH]

    c0 = SSM_WIDTH
    qn = proj[:, c0:c0 + NA_WIDTH] * ATTN_SCALE
    kn = proj[:, c0 + NA_WIDTH:c0 + 2 * NA_WIDTH]
    vn = proj[:, c0 + 2 * NA_WIDTH:c0 + 3 * NA_WIDTH]
    c1 = c0 + 3 * NA_WIDTH
    vg = proj[:, c1 + GQA_WIDTH + KV_WIDTH:c1 + GQA_WIDTH + 2 * KV_WIDTH]

    gsum = gsum_ref[...]
    cos, sin = cos_ref[...], sin_ref[...]
    lane = lax.broadcasted_iota(jnp.int32, (1, LANES), 1)
    first_half = (lane & (2 * ROPE_F - 1)) < ROPE_F
    qg_blocks = []
    for p in range(GQA_WIDTH // PAIR):
        blk = proj[:, c1 + p * PAIR:c1 + (p + 1) * PAIR]
        blk = _rope(_head_rms(blk, gsum, qg_ref[...]), cos, sin, first_half)
        qg_blocks.append(blk * ATTN_SCALE)
    kg = proj[:, c1 + GQA_WIDTH:c1 + GQA_WIDTH + KV_WIDTH]
    kg = _rope(_head_rms(kg, gsum, kg_ref[...]), cos, sin, first_half)

    o = 0
    vg_ones = [jnp.where(lane < HEAD_DIM, vg, 1.0), jnp.where(lane >= HEAD_DIM, vg, 1.0)]
    for piece in [qn, kn, vn] + qg_blocks + [kg] + vg_ones:
        w = piece.shape[1]
        qkv_ref[:, o:o + w] = piece.astype(jnp.bfloat16)
        o += w

    @pl.when(i < T_CTX // TM)
    def _():
        kv32_ref[:, 0:NA_WIDTH] = kn
        kv32_ref[:, NA_WIDTH:2 * NA_WIDTH] = vn
        kv32_ref[:, 2 * NA_WIDTH:2 * NA_WIDTH + KV_WIDTH] = kg
        kv32_ref[:, 2 * NA_WIDTH + KV_WIDTH:2 * NA_WIDTH + 2 * KV_WIDTH] = vg


QKV_WIDTH = 3 * NA_WIDTH + GQA_WIDTH + 3 * KV_WIDTH
KV32_WIDTH = 2 * NA_WIDTH + 2 * KV_WIDTH
QN_BLK, KN_BLK, VN_BLK, QG_BLK, KG_BLK, VG1_BLK = 0, 3, 6, 9, 12, 13


def _rope_block(i):
    n_ctx = T_CTX // TM
    return jnp.where(i < n_ctx, 0, 1 + (i - n_ctx) % (DEC_SEQ // TM))


def _in_proj(x, moe_residual, mods3, norm_g, w_in_bf, q_gain, k_gain, gsum, rope_cos, rope_sin):
    n_ctx = T_CTX // TM
    row_spec = pl.BlockSpec((TM, D_MODEL), lambda i: (i, 0))
    mod_spec = pl.BlockSpec((1, 1, 6 * D_MODEL), lambda i: (_modset(i), 0, 0))
    args, in_specs = [x], [row_spec]
    out_specs = [pl.BlockSpec((TM, SSM_WIDTH), lambda i: (i, 0)),
                 pl.BlockSpec((TM, QKV_WIDTH), lambda i: (i, 0)),
                 pl.BlockSpec((TM, KV32_WIDTH), lambda i: (jnp.minimum(i, n_ctx - 1), 0))]
    out_shape = [jax.ShapeDtypeStruct((T_ALL, SSM_WIDTH), jnp.float32),
                 jax.ShapeDtypeStruct((T_ALL, QKV_WIDTH), jnp.bfloat16),
                 jax.ShapeDtypeStruct((T_CTX, KV32_WIDTH), jnp.float32)]
    if moe_residual is not None:
        args += list(moe_residual)
        in_specs += [pl.BlockSpec((TM * TILE_ROWS, LANES), lambda i: (i, 0)), mod_spec]
        out_specs.append(row_spec)
        out_shape.append(jax.ShapeDtypeStruct((T_ALL, D_MODEL), jnp.float32))
    args += [mods3, norm_g, w_in_bf, q_gain, k_gain, gsum, rope_cos, rope_sin]
    in_specs += [mod_spec,
                 pl.BlockSpec((1, D_MODEL), lambda i: (0, 0)),
                 pl.BlockSpec((D_MODEL, IN_WIDTH), lambda i: (0, 0)),
                 pl.BlockSpec((1, LANES), lambda i: (0, 0)),
                 pl.BlockSpec((1, LANES), lambda i: (0, 0)),
                 pl.BlockSpec((LANES, LANES), lambda i: (0, 0)),
                 pl.BlockSpec((TM, LANES), lambda i: (_rope_block(i), 0)),
                 pl.BlockSpec((TM, LANES), lambda i: (_rope_block(i), 0))]
    outs = pl.pallas_call(
        functools.partial(_inproj_kernel, moe_residual is not None),
        grid=(T_ALL // TM,),
        in_specs=in_specs,
        out_specs=out_specs,
        out_shape=out_shape,
        compiler_params=_cparams("arbitrary"),
        name="in_proj",
    )(*args)
    return outs if moe_residual is not None else (*outs, x)


def _s5scan_kernel(u_ref, b_ref, a_ref, c_ref, y_ref, p_ref, fin_ref, hbuf):
    d = pl.program_id(0)
    k = pl.program_id(1)
    rows = S5_TB * S5_ROWS
    a_re = jnp.broadcast_to(a_ref[0, :, 0:S5_HALF], (SUBLANES, S5_HALF))
    a_im = jnp.broadcast_to(a_ref[0, :, S5_HALF:S5_CPLX], (SUBLANES, S5_HALF))
    ub = u_ref[...].astype(jnp.bfloat16)

    def run(forward):
        data0 = S5_ROWS if forward else 0
        prev0 = 0 if forward else rows
        last0 = rows if forward else 0

        @pl.when(k == 0)
        def _():
            r = lax.broadcasted_iota(jnp.int32, (S5_ROWS, S5_CPLX), 0)
            l = lax.broadcasted_iota(jnp.int32, (S5_ROWS, S5_CPLX), 1)
            hbuf[prev0:prev0 + S5_ROWS, :] = jnp.where((r == S5_POW_ROW) & (l < S5_HALF), 1.0, 0.0)

        @pl.when(k > 0)
        def _():
            hbuf[prev0:prev0 + S5_ROWS, :] = hbuf[last0:last0 + S5_ROWS, :]

        hbuf[data0:data0 + rows, :] = _bdot(ub, b_ref[0])

        def step(i, carry):
            t = i if forward else S5_TB - 1 - i
            cur = pl.multiple_of(data0 + t * S5_ROWS, SUBLANES)
            prv = pl.multiple_of(cur - S5_ROWS if forward else cur + S5_ROWS, SUBLANES)
            for st in range(S5_ROWS // SUBLANES):
                rc = pl.ds(cur + st * SUBLANES, SUBLANES)
                rp = pl.ds(prv + st * SUBLANES, SUBLANES)
                hr = hbuf[rp, 0:S5_HALF]
                hi = hbuf[rp, S5_HALF:S5_CPLX]
                nr = a_re * hr - a_im * hi + hbuf[rc, 0:S5_HALF]
                ni = a_re * hi + a_im * hr + hbuf[rc, S5_HALF:S5_CPLX]
                hbuf[rc, 0:S5_HALF] = nr
                hbuf[rc, S5_HALF:S5_CPLX] = ni
            return carry

        lax.fori_loop(0, S5_TB, step, 0, unroll=True)
        y_ref[0] = _bdot(hbuf[data0:data0 + rows, :].astype(jnp.bfloat16), c_ref[0])
        for t in range(S5_TB):
            r0 = data0 + t * S5_ROWS + S5_POW_ROW
            p_ref[0, t:t + 1, :] = hbuf[r0:r0 + 1, :]

        @pl.when(k == S5_NBLK - 1)
        def _():
            fin_ref[0] = hbuf[last0:last0 + S5_ROWS, :]

    @pl.when(d == 0)
    def _():
        run(True)

    @pl.when(d == 1)
    def _():
        run(False)


def _s5_scan(u_tm, bblk, abar, cblk):
    rows = S5_TB * S5_ROWS

    def tblk(d, k):
        return jnp.where(d == 0, k, S5_NBLK - 1 - k)

    return pl.pallas_call(
        _s5scan_kernel,
        grid=(2, S5_NBLK),
        in_specs=[pl.BlockSpec((rows, SSM_WIDTH), lambda d, k: (tblk(d, k), 0)),
                  pl.BlockSpec((1, SSM_WIDTH, S5_CPLX), lambda d, k: (d, 0, 0)),
                  pl.BlockSpec((1, 1, S5_CPLX), lambda d, k: (d, 0, 0)),
                  pl.BlockSpec((1, S5_CPLX, SSM_WIDTH), lambda d, k: (d, 0, 0))],
        out_specs=[pl.BlockSpec((1, rows, SSM_WIDTH), lambda d, k: (d, tblk(d, k), 0)),
                   pl.BlockSpec((1, S5_TB, S5_CPLX), lambda d, k: (d, tblk(d, k), 0)),
                   pl.BlockSpec((1, S5_ROWS, S5_CPLX), lambda d, k: (d, 0, 0))],
        out_shape=(jax.ShapeDtypeStruct((2, S5_CHUNK * S5_ROWS, SSM_WIDTH), jnp.float32),
                   jax.ShapeDtypeStruct((2, S5_CHUNK, S5_CPLX), jnp.float32),
                   jax.ShapeDtypeStruct((2, S5_ROWS, S5_CPLX), jnp.float32)),
        scratch_shapes=[pltpu.VMEM((rows + S5_ROWS, S5_CPLX), jnp.float32)],
        compiler_params=_cparams("arbitrary", "arbitrary"),
        name="s5_scan",
    )(u_tm, bblk, abar, cblk)


def _cmul(pr, pi, cr, ci):
    return pr * cr - pi * ci, pr * ci + pi * cr


def _s5fin_kernel(u_ref, yf_ref, yb_ref, p_ref, fin_ref, h0_ref, c_ref, d_ref, wg_ref, bg_ref,
                  o_ref, carry, hc, ybuf):
    k = pl.program_id(0)
    nchunk = DEC_SEQ // S5_CHUNK

    @pl.when(k == 0)
    def _():
        for d in range(2):
            t_full = S5_CHUNK - 1 if d == 0 else 0
            pr = p_ref[d, t_full:t_full + 1, 0:S5_HALF]
            pi = p_ref[d, t_full:t_full + 1, S5_HALF:S5_CPLX]
            for b in range(DEC_BATCH):
                cr = h0_ref[d, b:b + 1, 0:S5_HALF]
                ci = h0_ref[d, b:b + 1, S5_HALF:S5_CPLX]
                order = range(nchunk) if d == 0 else range(nchunk - 1, -1, -1)
                for c in order:
                    row = b * nchunk + c
                    carry[d, row:row + 1, 0:S5_HALF] = cr
                    carry[d, row:row + 1, S5_HALF:S5_CPLX] = ci
                    fr = fin_ref[d, S5_LAT0 + row:S5_LAT0 + row + 1, 0:S5_HALF]
                    fi = fin_ref[d, S5_LAT0 + row:S5_LAT0 + row + 1, S5_HALF:S5_CPLX]
                    mr, mi = _cmul(pr, pi, cr, ci)
                    cr, ci = fr + mr, fi + mi

    ybuf[...] = yf_ref[0] + yb_ref[0] + u_ref[...] * d_ref[...]
    for d in range(2):
        cr = carry[d, :, 0:S5_HALF]
        ci = carry[d, :, S5_HALF:S5_CPLX]
        for t in range(S5_TB):
            pr = p_ref[d, pl.ds(k * S5_TB + t, 1), 0:S5_HALF]
            pi = p_ref[d, pl.ds(k * S5_TB + t, 1), S5_HALF:S5_CPLX]
            mr, mi = _cmul(pr, pi, cr, ci)
            hc[t * S5_NLAT:(t + 1) * S5_NLAT, 0:S5_HALF] = mr.astype(jnp.bfloat16)
            hc[t * S5_NLAT:(t + 1) * S5_NLAT, S5_HALF:S5_CPLX] = mi.astype(jnp.bfloat16)
        corr = _bdot(hc[...], c_ref[d])
        ybuf[:, S5_LAT0:S5_LAT0 + S5_NLAT, :] += corr.reshape(S5_TB, S5_NLAT, SSM_WIDTH)

    y = ybuf[...].reshape(S5_TB * S5_ROWS, SSM_WIDTH)
    yg = 0.5 * y * (1.0 + jnp.tanh(math.sqrt(2.0 / math.pi) * (y + 0.044715 * (y * y * y))))
    z = _bdot(yg.astype(jnp.bfloat16), wg_ref[...]) + bg_ref[...]
    out = yg / (1.0 + jnp.exp(-z))
    o_ref[...] = out.reshape(S5_TB, S5_ROWS, SSM_WIDTH).astype(jnp.bfloat16)


def _s5_finish(u_tm3, y_dir4, powers, finals, h0, cblk, ssm_d, w_glu_bf, b_glu):
    blk3 = (S5_TB, S5_ROWS, SSM_WIDTH)
    return pl.pallas_call(
        _s5fin_kernel,
        grid=(S5_NBLK,),
        in_specs=[pl.BlockSpec(blk3, lambda k: (k, 0, 0)),
                  pl.BlockSpec((1,) + blk3, lambda k: (0, k, 0, 0)),
                  pl.BlockSpec((1,) + blk3, lambda k: (1, k, 0, 0)),
                  pl.BlockSpec((2, S5_CHUNK, S5_CPLX), lambda k: (0, 0, 0)),
                  pl.BlockSpec((2, S5_ROWS, S5_CPLX), lambda k: (0, 0, 0)),
                  pl.BlockSpec((2, DEC_BATCH, S5_CPLX), lambda k: (0, 0, 0)),
                  pl.BlockSpec((2, S5_CPLX, SSM_WIDTH), lambda k: (0, 0, 0)),
                  pl.BlockSpec((1, SSM_WIDTH), lambda k: (0, 0)),
                  pl.BlockSpec((SSM_WIDTH, SSM_WIDTH), lambda k: (0, 0)),
                  pl.BlockSpec((1, SSM_WIDTH), lambda k: (0, 0))],
        out_specs=pl.BlockSpec(blk3, lambda k: (k, 0, 0)),
        out_shape=jax.ShapeDtypeStruct((S5_CHUNK, S5_ROWS, SSM_WIDTH), jnp.bfloat16),
        scratch_shapes=[pltpu.VMEM((2, S5_NLAT, S5_CPLX), jnp.float32),
                        pltpu.VMEM((S5_TB * S5_NLAT, S5_CPLX), jnp.bfloat16),
                        pltpu.VMEM(blk3, jnp.float32)],
        compiler_params=_cparams("arbitrary"),
        name="s5_finish",
    )(u_tm3, y_dir4, y_dir4, powers, finals, h0, cblk, ssm_d, w_glu_bf, b_glu)


def _pair_attention(q, kvs, biases):
    lane = lax.broadcasted_iota(jnp.int32, (1, PAIR), 1)
    outs = []
    for a in range(2):
        sel = (lane < HEAD_DIM) if a == 0 else (lane >= HEAD_DIM)
        qa = jnp.where(sel, q, jnp.zeros_like(q))
        ss = []
        for (kk, _), bias in zip(kvs, biases):
            s = _bdot_t(qa, kk)
            if bias is not None:
                s = s + bias[a]
            ss.append(s)
        m = ss[0].max(axis=-1, keepdims=True)
        for s in ss[1:]:
            m = jnp.maximum(m, s.max(axis=-1, keepdims=True))
        den = None
        acc = None
        for s, (_, vv) in zip(ss, kvs):
            p = jnp.exp(s - m)
            l = p.sum(axis=-1, keepdims=True)
            o = _bdot(p.astype(jnp.bfloat16), vv)
            den = l if den is None else den + l
            acc = o if acc is None else acc + o
        outs.append(acc / den)
    return jnp.where(lane < HEAD_DIM, outs[0], outs[1])


def _pair_attention_vones(q, kvs):
    lane = lax.broadcasted_iota(jnp.int32, (1, PAIR), 1)
    outs = []
    for a in range(2):
        sel = (lane < HEAD_DIM) if a == 0 else (lane >= HEAD_DIM)
        qa = jnp.where(sel, q, jnp.zeros_like(q))
        ss = [_bdot_t(qa, kv[0]) for kv in kvs]
        m = ss[0].max(axis=-1, keepdims=True)
        for s in ss[1:]:
            m = jnp.maximum(m, s.max(axis=-1, keepdims=True))
        acc = None
        for s, kv in zip(ss, kvs):
            o = _bdot(jnp.exp((s - m).astype(jnp.bfloat16)), kv[1 + a])
            acc = o if acc is None else acc + o
        den = pltpu.roll(acc, HEAD_DIM, 1)
        outs.append(acc / den)
    return jnp.where(lane < HEAD_DIM, outs[0], outs[1])


ATTN_WIDTH = NA_WIDTH + GQA_WIDTH
NPAIR = NA_WIDTH // PAIR


def _attn_ctx_kernel(qkv_ref, o_ref):
    def blk(j):
        return qkv_ref[:, j * PAIR:(j + 1) * PAIR]

    for p in range(NPAIR):
        o = _pair_attention(blk(QN_BLK + p), [(blk(KN_BLK + p), blk(VN_BLK + p))], [None])
        o_ref[:, p * PAIR:(p + 1) * PAIR] = o.astype(jnp.bfloat16)
    for p in range(NPAIR):
        o = _pair_attention_vones(blk(QG_BLK + p), [(blk(KG_BLK), blk(VG1_BLK), blk(VG1_BLK + 1))])
        o_ref[:, (NPAIR + p) * PAIR:(NPAIR + p + 1) * PAIR] = o.astype(jnp.bfloat16)


def _attn_ctx(qkv):
    return pl.pallas_call(
        _attn_ctx_kernel,
        grid=(BATCH,),
        in_specs=[pl.BlockSpec((SEQ, QKV_WIDTH), lambda b: (b, 0))],
        out_specs=pl.BlockSpec((SEQ, ATTN_WIDTH), lambda b: (b, 0)),
        out_shape=jax.ShapeDtypeStruct((T_CTX, ATTN_WIDTH), jnp.bfloat16),
        compiler_params=_cparams("arbitrary"),
        name="attn_ctx",
    )(qkv)


TQ = 256


def _gqa_lat_kernel(q_ref, k_ref, v0_ref, v1_ref, ck_ref, cv_ref, o_ref):
    kvs = [(ck_ref[0], cv_ref[0, 0], cv_ref[0, 1]), (k_ref[...], v0_ref[...], v1_ref[...])]
    o_ref[...] = _pair_attention_vones(q_ref[...], kvs).astype(jnp.bfloat16)


def _gqa_lat(qkv, ck, cv_ones):
    nq = DEC_SEQ // TQ
    ctx_blocks = T_CTX // DEC_SEQ
    return pl.pallas_call(
        _gqa_lat_kernel,
        grid=(DEC_BATCH, NPAIR, nq),
        in_specs=[pl.BlockSpec((TQ, PAIR), lambda b, p, i: (T_CTX // TQ + b * nq + i, QG_BLK + p)),
                  pl.BlockSpec((DEC_SEQ, PAIR), lambda b, p, i: (ctx_blocks + b, KG_BLK)),
                  pl.BlockSpec((DEC_SEQ, PAIR), lambda b, p, i: (ctx_blocks + b, VG1_BLK)),
                  pl.BlockSpec((DEC_SEQ, PAIR), lambda b, p, i: (ctx_blocks + b, VG1_BLK + 1)),
                  pl.BlockSpec((1, PAST_LEN, PAIR), lambda b, p, i: (b, 0, 0)),
                  pl.BlockSpec((1, 2, PAST_LEN, PAIR), lambda b, p, i: (b, 0, 0, 0))],
        out_specs=pl.BlockSpec((TQ, PAIR), lambda b, p, i: (b * nq + i, p)),
        out_shape=jax.ShapeDtypeStruct((T_LAT, GQA_WIDTH), jnp.bfloat16),
        compiler_params=_cparams("arbitrary", "arbitrary", "arbitrary"),
        name="gqa_lat",
    )(qkv, qkv, qkv, qkv, ck, cv_ones)


NA_BLK = 4
NA_NBLK = GRID_H // NA_BLK
NA_SLAB_ROWS = NA_KH + NA_BLK - 1
NA_SLAB = NA_SLAB_ROWS * GRID_W
NA_QROWS = NA_BLK * GRID_W
NA_BLOCK_TYPES = (0, 2, NA_NBLK - 1)


def _na_slab_start(j):
    return jnp.clip(j * NA_BLK - NA_KH // 2, 0, GRID_H - NA_SLAB_ROWS)


def _na_lat_kernel(q_ref, k_ref, v_ref, ck_ref, cv_ref, bias_ref, o_ref):
    ck, cv = ck_ref[0, 0], cv_ref[0, 0]
    lane = lax.broadcasted_iota(jnp.int32, (1, PAIR), 1)

    def block(j, carry):
        q0 = pl.multiple_of(j * NA_QROWS, NA_QROWS)
        k0 = pl.multiple_of(_na_slab_start(j) * GRID_W, GRID_W)
        btype = jnp.where(j == 0, 0, jnp.where(j == NA_NBLK - 1, 2, 1))
        q = q_ref[pl.ds(q0, NA_QROWS), :]
        lhs = jnp.concatenate([jnp.where(lane < HEAD_DIM, q, jnp.zeros_like(q)),
                               jnp.where(lane >= HEAD_DIM, q, jnp.zeros_like(q))], axis=0)
        ks = k_ref[pl.ds(k0, NA_SLAB), :]
        vs = v_ref[pl.ds(k0, NA_SLAB), :]
        s_lat = _bdot_t(lhs, ks) + bias_ref[0, 0, btype]
        s_ctx = _bdot_t(lhs, ck)
        m = jnp.maximum(s_lat.max(axis=-1, keepdims=True), s_ctx.max(axis=-1, keepdims=True))
        p_lat = jnp.exp((s_lat - m).astype(jnp.bfloat16))
        p_ctx = jnp.exp((s_ctx - m).astype(jnp.bfloat16))
        den = (_bdot(p_lat, jnp.ones((NA_SLAB, PAIR), jnp.bfloat16))
               + _bdot(p_ctx, jnp.ones((PAST_LEN, PAIR), jnp.bfloat16)))
        o = (_bdot(p_lat, vs) + _bdot(p_ctx, cv)) / den
        out = jnp.where(lane < HEAD_DIM, o[0:NA_QROWS], o[NA_QROWS:2 * NA_QROWS])
        o_ref[pl.ds(q0, NA_QROWS), :] = out.astype(jnp.bfloat16)
        return carry

    lax.fori_loop(0, NA_NBLK, block, 0)


def _na_lat(layer, qkv, ck, cv, bias):
    ctx_blocks = T_CTX // DEC_SEQ
    return pl.pallas_call(
        _na_lat_kernel,
        grid=(DEC_BATCH, NPAIR),
        in_specs=[pl.BlockSpec((DEC_SEQ, PAIR), lambda b, p: (ctx_blocks + b, QN_BLK + p)),
                  pl.BlockSpec((DEC_SEQ, PAIR), lambda b, p: (ctx_blocks + b, KN_BLK + p)),
                  pl.BlockSpec((DEC_SEQ, PAIR), lambda b, p: (ctx_blocks + b, VN_BLK + p)),
                  pl.BlockSpec((1, 1, PAST_LEN, PAIR), lambda b, p: (b, p, 0, 0)),
                  pl.BlockSpec((1, 1, PAST_LEN, PAIR), lambda b, p: (b, p, 0, 0)),
                  pl.BlockSpec((1, 1, len(NA_BLOCK_TYPES), 2 * NA_QROWS, NA_SLAB),
                               lambda b, p: (layer, p, 0, 0, 0))],
        out_specs=pl.BlockSpec((DEC_SEQ, PAIR), lambda b, p: (b, p)),
        out_shape=jax.ShapeDtypeStruct((T_LAT, NA_WIDTH), jnp.bfloat16),
        compiler_params=_cparams("arbitrary", "arbitrary"),
        name="na_lat",
    )(qkv, qkv, qkv, ck, cv, bias)


def _na_bias_table(rpb):
    nt = len(NA_BLOCK_TYPES)
    nl = rpb.shape[0]
    j = np.array(NA_BLOCK_TYPES)[:, None]
    r = j * NA_BLK + np.arange(NA_BLK)[None, :]
    slab0 = np.clip(j * NA_BLK - NA_KH // 2, 0, GRID_H - NA_SLAB_ROWS)
    win0 = np.clip(r - NA_KH // 2, 0, GRID_H - NA_KH)
    key_row = slab0[:, :, None] + np.arange(NA_SLAB_ROWS)[None, None, :]
    row_in = (key_row >= win0[:, :, None]) & (key_row < win0[:, :, None] + NA_KH)
    dr = np.clip(key_row - r[:, :, None] + (NA_KH - 1), 0, 2 * NA_KH - 2)
    cq = np.arange(GRID_W)
    ck = np.arange(GRID_W)
    cs = np.clip(cq - NA_KW // 2, 0, GRID_W - NA_KW)
    col_in = (ck[None, :] >= cs[:, None]) & (ck[None, :] < cs[:, None] + NA_KW)
    dc = np.clip(ck[None, :] - cq[:, None] + NA_KW - 1, 0, 2 * NA_KW - 2)
    sel_c = jnp.asarray(np.eye(2 * NA_KW - 1, dtype=np.float32)[dc])
    cols = jnp.einsum("lhdc,qkc->lhdqk", rpb.astype(jnp.float32), sel_c, precision=lax.Precision.HIGHEST)
    cols = jnp.where(jnp.asarray(col_in), cols, NEG_INF)
    n_off = 2 * NA_KH - 1
    cols = cols.reshape(nl, NA_HEADS // 2, 2, n_off, GRID_W, GRID_W)
    masked = jnp.full((nl, NA_HEADS // 2, 2, 1, GRID_W, GRID_W), NEG_INF, jnp.float32)
    cols = jnp.concatenate([cols, masked], axis=3)
    which = jnp.asarray(np.where(row_in, dr, n_off).reshape(-1), jnp.int32)

    def assemble(which_ref, cols_ref, o_ref):
        t = pl.program_id(2)
        for e in range(2):
            for n in range(NA_BLK):
                r0 = (e * NA_BLK + n) * GRID_W
                for i in range(NA_SLAB_ROWS):
                    d = which_ref[(t * NA_BLK + n) * NA_SLAB_ROWS + i]
                    o_ref[0, 0, 0, r0:r0 + GRID_W, i * GRID_W:(i + 1) * GRID_W] = cols_ref[0, 0, e, d]

    grid_spec = pltpu.PrefetchScalarGridSpec(
        num_scalar_prefetch=1,
        grid=(nl, NA_HEADS // 2, nt),
        in_specs=[pl.BlockSpec((1, 1, 2, n_off + 1, GRID_W, GRID_W), lambda l, p, t, w: (l, p, 0, 0, 0, 0))],
        out_specs=pl.BlockSpec((1, 1, 1, 2 * NA_QROWS, NA_SLAB), lambda l, p, t, w: (l, p, t, 0, 0)))
    return pl.pallas_call(
        assemble,
        grid_spec=grid_spec,
        out_shape=jax.ShapeDtypeStruct((nl, NA_HEADS // 2, nt, 2 * NA_QROWS, NA_SLAB), jnp.float32),
        compiler_params=_cparams("arbitrary", "arbitrary", "arbitrary"),
        name="na_bias",
    )(which, cols)


GATE_LANE0 = N_GROUPS


def _router_logits(h2, wr_hi, wr_lo, br):
    h_hi = h2.astype(jnp.bfloat16)
    h_lo = (h2 - h_hi.astype(jnp.float32)).astype(jnp.bfloat16)
    return _bdot(h_hi, wr_hi) + _bdot(h_lo, wr_hi) + _bdot(h_hi, wr_lo) + br


def _top_group(logits):
    lane_i = lax.broadcasted_iota(jnp.int32, logits.shape, 1)
    is_g = lane_i < N_GROUPS
    lg = jnp.where(is_g, logits, NEG_INF)
    mg = lg.max(axis=-1, keepdims=True)
    return jnp.where(is_g & (lg == mg), lane_i.astype(jnp.float32), float(LANES)).min(axis=-1, keepdims=True)


def _gates_in_group(logits, group):
    lane_i = lax.broadcasted_iota(jnp.int32, logits.shape, 1)
    lane = lane_i.astype(jnp.float32)
    big = float(LANES)
    is_g = lane_i < N_GROUPS
    lg = jnp.where(is_g, logits, NEG_INF)
    mg = lg.max(axis=-1, keepdims=True)
    own = jnp.where(lane_i == group, logits, 0.0).sum(axis=-1, keepdims=True)
    pg_top = jnp.exp(own - mg) / jnp.where(is_g, jnp.exp(lg - mg), 0.0).sum(axis=-1, keepdims=True)
    lane0 = GATE_LANE0 + group * EXPERTS_PER_GROUP
    sel = (lane_i >= lane0) & (lane_i < lane0 + EXPERTS_PER_GROUP)
    le = jnp.where(sel, logits, NEG_INF)
    me = le.max(axis=-1, keepdims=True)
    ex = jnp.where(sel, jnp.exp(le - me), 0.0)
    pe = ex / ex.sum(axis=-1, keepdims=True)
    p1 = pe.max(axis=-1, keepdims=True)
    i1 = jnp.where(sel & (pe == p1), lane, big).min(axis=-1, keepdims=True)
    rest = sel & (lane != i1)
    pr = jnp.where(rest, pe, -1.0)
    p2 = pr.max(axis=-1, keepdims=True)
    i2 = jnp.where(rest & (pr == p2), lane, big).min(axis=-1, keepdims=True)
    tot = p1 + p2
    within = jnp.where(lane == i1, p1 / tot, 0.0) + jnp.where(lane == i2, p2 / tot, 0.0)
    return pg_top * within


TILE_ROWS = D_MODEL // LANES


def _store_token_tiles(ref, x):
    n = x.shape[0]
    for c in range(TILE_ROWS):
        ref[pl.ds(c, n, stride=TILE_ROWS), :] = x[:, c * LANES:(c + 1) * LANES]


def _load_token_tiles(ref, n):
    return jnp.concatenate([ref[pl.ds(c, n, stride=TILE_ROWS), :] for c in range(TILE_ROWS)], axis=1)


def _outproj_kernel(x_ref, ys_ref, yc_ref, yn_ref, yg_ref, mod_ref, g_ref, w_ref, wr_hi_ref, wr_lo_ref, br_ref,
                    x1_ref, h2t_ref, grp_ref):
    is_ctx = pl.program_id(0) < T_CTX // TM
    y_na = jnp.where(is_ctx, yc_ref[:, 0:NA_WIDTH], yn_ref[...])
    y_g = jnp.where(is_ctx, yc_ref[:, NA_WIDTH:ATTN_WIDTH], yg_ref[...])
    mix = (_bdot(ys_ref[...], w_ref[0:SSM_WIDTH, :])
           + _bdot(y_na, w_ref[SSM_WIDTH:SSM_WIDTH + NA_WIDTH, :])
           + _bdot(y_g, w_ref[SSM_WIDTH + NA_WIDTH:, :]))
    gate1 = mod_ref[0, :, 2 * D_MODEL:3 * D_MODEL]
    shift2 = mod_ref[0, :, 3 * D_MODEL:4 * D_MODEL]
    scale2 = mod_ref[0, :, 4 * D_MODEL:5 * D_MODEL]
    x1 = x_ref[...] + gate1 * mix
    x1_ref[...] = x1
    ms = jnp.mean(x1 * x1, axis=-1, keepdims=True)
    h2 = (x1 * lax.rsqrt(ms + EPS) * g_ref[...]) * (1.0 + scale2) + shift2
    _store_token_tiles(h2t_ref, h2)
    logits = _router_logits(h2, wr_hi_ref[0], wr_lo_ref[0], br_ref[0])
    grp_ref[...] = jnp.broadcast_to(_top_group(logits), (TM, LANES))


def _out_proj(layer, x, y_ssm, y_ctx, y_na_lat, y_g_lat, mods3, norm_g, w_out_bf, wr_hi, wr_lo, b_router):
    n_ctx = T_CTX // TM
    return pl.pallas_call(
        _outproj_kernel,
        grid=(T_ALL // TM,),
        in_specs=[pl.BlockSpec((TM, D_MODEL), lambda i: (i, 0)),
                  pl.BlockSpec((TM, SSM_WIDTH), lambda i: (i, 0)),
                  pl.BlockSpec((TM, ATTN_WIDTH), lambda i: (jnp.minimum(i, n_ctx - 1), 0)),
                  pl.BlockSpec((TM, NA_WIDTH), lambda i: (jnp.maximum(i - n_ctx, 0), 0)),
                  pl.BlockSpec((TM, GQA_WIDTH), lambda i: (jnp.maximum(i - n_ctx, 0), 0)),
                  pl.BlockSpec((1, 1, 6 * D_MODEL), lambda i: (_modset(i), 0, 0)),
                  pl.BlockSpec((1, D_MODEL), lambda i: (0, 0)),
                  pl.BlockSpec((D_MODEL, D_MODEL), lambda i: (0, 0)),
                  pl.BlockSpec((1, D_MODEL, LANES), lambda i: (layer, 0, 0)),
                  pl.BlockSpec((1, D_MODEL, LANES), lambda i: (layer, 0, 0)),
                  pl.BlockSpec((1, 1, LANES), lambda i: (layer, 0, 0))],
        out_specs=[pl.BlockSpec((TM, D_MODEL), lambda i: (i, 0)),
                   pl.BlockSpec((TM * TILE_ROWS, LANES), lambda i: (i, 0)),
                   pl.BlockSpec((TM, LANES), lambda i: (i, 0))],
        out_shape=(jax.ShapeDtypeStruct((T_ALL, D_MODEL), jnp.float32),
                   jax.ShapeDtypeStruct((T_ALL * TILE_ROWS, LANES), jnp.float32),
                   jax.ShapeDtypeStruct((T_ALL, LANES), jnp.float32)),
        compiler_params=_cparams("arbitrary"),
        name="out_proj",
    )(x, y_ssm, y_ctx, y_na_lat, y_g_lat, mods3, norm_g, w_out_bf, wr_hi, wr_lo, b_router)


TMOE = 256
NT_MOE = T_ALL // TMOE + N_GROUPS
MOE_SLOTS = NT_MOE * TMOE


def _moe_dispatch(group_f):
    i32 = jnp.int32
    g = group_f.astype(i32)
    onehot = (g[:, None] == jnp.arange(N_GROUPS, dtype=i32)[None, :]).astype(i32)
    csum = jnp.cumsum(onehot, axis=0)
    counts = csum[-1]
    rank = jnp.sum(onehot * csum, axis=1) - 1
    tiles = (counts + TMOE - 1) // TMOE
    tile_end = jnp.cumsum(tiles)
    tile_start = tile_end - tiles
    slot = jnp.sum(onehot * tile_start[None, :], axis=1) * TMOE + rank
    tok = jnp.full((MOE_SLOTS,), -1, i32).at[slot].set(jnp.arange(T_ALL, dtype=i32))
    pad = tok < 0
    pad_rank = jnp.cumsum(pad.astype(i32)) - 1
    row_of_slot = jnp.where(pad, T_ALL + pad_rank, tok) * TILE_ROWS
    tile_group = jnp.minimum(
        jnp.sum(jnp.arange(NT_MOE, dtype=i32)[:, None] >= tile_end[None, :], axis=1), N_GROUPS - 1).astype(i32)
    return tile_group, row_of_slot


def _moe_kernel(tg_ref, row_ref, h_hbm, wr_hi_ref, wr_lo_ref, br_ref, wg_ref, wu_ref, wd_ref, y_hbm,
                gbuf, ybuf, wg_bf, wu_bf, wd_bf, gsem, ssem):
    i = pl.program_id(0)
    slot = lax.rem(i, 2)
    other = 1 - slot
    last = NT_MOE - 1
    nxt = jnp.minimum(i + 1, last)
    prv = jnp.maximum(i - 1, 0)
    buf_rows = TMOE * TILE_ROWS

    def gather_tok(row, buf_slot, r):
        return pltpu.make_async_copy(h_hbm.at[pl.ds(row, TILE_ROWS)],
                                     gbuf.at[buf_slot, pl.ds(r * TILE_ROWS, TILE_ROWS)], gsem.at[buf_slot])

    def scatter_tok(row, buf_slot, r):
        return pltpu.make_async_copy(ybuf.at[buf_slot, pl.ds(r * TILE_ROWS, TILE_ROWS)],
                                     y_hbm.at[pl.ds(row, TILE_ROWS)], ssem.at[buf_slot])

    def wait_gather(buf_slot):
        pltpu.make_async_copy(h_hbm.at[pl.ds(0, buf_rows)], gbuf.at[buf_slot], gsem.at[buf_slot]).wait()

    def wait_scatter(buf_slot):
        pltpu.make_async_copy(ybuf.at[buf_slot], y_hbm.at[pl.ds(0, buf_rows)], ssem.at[buf_slot]).wait()

    def start_gather_tok(tile, buf_slot, r):
        row = row_ref[tile * TMOE + r]
        gather_tok(jnp.where(row < T_ALL * TILE_ROWS, row, 0), buf_slot, r).start()

    @pl.when(i == 0)
    def _():
        ybuf[1] = jnp.zeros((buf_rows, LANES), jnp.float32)
        for r in range(TMOE):
            start_gather_tok(0, 0, r)

    first_of_group = (i == 0) | (tg_ref[i] != tg_ref[prv])

    @pl.when(first_of_group)
    def _():
        wg_bf[...] = wg_ref[0].astype(jnp.bfloat16)
        wu_bf[...] = wu_ref[0].astype(jnp.bfloat16)
        wd_bf[...] = wd_ref[0].astype(jnp.bfloat16)

    wait_gather(slot)

    h2 = _load_token_tiles(gbuf.at[slot], TMOE)
    h = h2.astype(jnp.bfloat16)
    routing = _gates_in_group(_router_logits(h2, wr_hi_ref[0], wr_lo_ref[0], br_ref[0]), tg_ref[i])
    lane = lax.broadcasted_iota(jnp.int32, (1, LANES), 1)
    lane0 = GATE_LANE0 + tg_ref[i] * EXPERTS_PER_GROUP
    rows_per_chunk = TMOE // EXPERTS_PER_GROUP
    y = None
    for j in range(EXPERTS_PER_GROUP):
        for r in range(j * rows_per_chunk, (j + 1) * rows_per_chunk):
            start_gather_tok(nxt, other, r)
            dst = jnp.where(i > 0, row_ref[prv * TMOE + r], (MOE_SLOTS + r) * TILE_ROWS)
            scatter_tok(dst, other, r).start()
        gate = jnp.where(lane == lane0 + j, routing, 0.0).sum(axis=-1, keepdims=True)
        aj = _bdot(h, wg_bf[j])
        bj = _bdot(h, wu_bf[j])
        act = ((aj / (1.0 + jnp.exp(-aj))) * bj * gate).astype(jnp.bfloat16)
        yj = _bdot(act, wd_bf[j])
        y = yj if y is None else y + yj

    @pl.when(i > 0)
    def _():
        wait_scatter(slot)

    _store_token_tiles(ybuf.at[slot], y)

    @pl.when(i == last)
    def _():
        for r in range(TMOE):
            scatter_tok(row_ref[last * TMOE + r], slot, r).start()
        wait_scatter(other)
        wait_scatter(slot)
        wait_gather(other)


def _moe(layer, h2_tiles, tile_group, row_of_slot, wr_hi, wr_lo, b_router, w_gate, w_up, w_down):
    def wspec(shape):
        return pl.BlockSpec((1, EXPERTS_PER_GROUP) + shape, lambda i, tg, rows: (layer, tg[i], 0, 0))

    def rspec(rows):
        return pl.BlockSpec((1, rows, LANES), lambda i, tg, rows_: (layer, 0, 0))

    grid_spec = pltpu.PrefetchScalarGridSpec(
        num_scalar_prefetch=2,
        grid=(NT_MOE,),
        in_specs=[pl.BlockSpec(memory_space=pl.ANY),
                  rspec(D_MODEL), rspec(D_MODEL), rspec(1),
                  wspec((D_MODEL, EXPERT_HIDDEN)), wspec((D_MODEL, EXPERT_HIDDEN)),
                  wspec((EXPERT_HIDDEN, D_MODEL))],
        out_specs=pl.BlockSpec(memory_space=pl.ANY),
        scratch_shapes=[pltpu.VMEM((2, TMOE * TILE_ROWS, LANES), jnp.float32),
                        pltpu.VMEM((2, TMOE * TILE_ROWS, LANES), jnp.float32),
                        pltpu.VMEM((EXPERTS_PER_GROUP, D_MODEL, EXPERT_HIDDEN), jnp.bfloat16),
                        pltpu.VMEM((EXPERTS_PER_GROUP, D_MODEL, EXPERT_HIDDEN), jnp.bfloat16),
                        pltpu.VMEM((EXPERTS_PER_GROUP, EXPERT_HIDDEN, D_MODEL), jnp.bfloat16),
                        pltpu.SemaphoreType.DMA((2,)),
                        pltpu.SemaphoreType.DMA((2,))])
    return pl.pallas_call(
        _moe_kernel,
        grid_spec=grid_spec,
        out_shape=jax.ShapeDtypeStruct(((MOE_SLOTS + TMOE) * TILE_ROWS, LANES), jnp.float32),
        compiler_params=_cparams("arbitrary"),
        name="moe",
    )(tile_group, row_of_slot, h2_tiles, wr_hi, wr_lo, b_router, w_gate, w_up, w_down)


def _final_kernel(x1_ref, y_ref, mod_ref, g_ref, o_ref):
    x = x1_ref[...] + mod_ref[0, :, 5 * D_MODEL:6 * D_MODEL] * _load_token_tiles(y_ref, TM)
    ms = jnp.mean(x * x, axis=-1, keepdims=True)
    o_ref[...] = x * lax.rsqrt(ms + EPS) * g_ref[...]


def _final_norm(x1, y, mods3, g):
    return pl.pallas_call(
        _final_kernel,
        grid=(T_ALL // TM,),
        in_specs=[pl.BlockSpec((TM, D_MODEL), lambda i: (i, 0)),
                  pl.BlockSpec((TM * TILE_ROWS, LANES), lambda i: (i, 0)),
                  pl.BlockSpec((1, 1, 6 * D_MODEL), lambda i: (_modset(i), 0, 0)),
                  pl.BlockSpec((1, D_MODEL), lambda i: (0, 0))],
        out_specs=pl.BlockSpec((TM, D_MODEL), lambda i: (i, 0)),
        out_shape=jax.ShapeDtypeStruct((T_ALL, D_MODEL), jnp.float32),
        compiler_params=_cparams("arbitrary"),
        name="final_norm",
    )(x1, y, mods3, g)


_QG_ORDER = np.array([0, 3, 1, 4, 2, 5])


def _permute_qg_columns(w_in):
    c1 = SSM_WIDTH + 3 * NA_WIDTH
    qg = w_in[..., c1:c1 + GQA_WIDTH].reshape(*w_in.shape[:-1], GQA_HEADS, HEAD_DIM)
    qg = qg[..., _QG_ORDER, :].reshape(*w_in.shape[:-1], GQA_WIDTH)
    return jnp.concatenate([w_in[..., :c1], qg, w_in[..., c1 + GQA_WIDTH:]], axis=-1)


def _permute_yg_rows(w_out):
    r0 = SSM_WIDTH + NA_WIDTH
    yg = w_out[:, r0:, :].reshape(DEPTH, GQA_HEADS, HEAD_DIM, D_MODEL)[:, _QG_ORDER]
    return jnp.concatenate([w_out[:, :r0, :], yg.reshape(DEPTH, GQA_WIDTH, D_MODEL)], axis=1)


def _rope_tables():
    t = jnp.arange(DEC_SEQ)
    row = (t // GRID_W).astype(jnp.float32)
    col = (t % GRID_W).astype(jnp.float32)
    freqs = ROPE_THETA ** (-jnp.arange(ROPE_F, dtype=jnp.float32) / ROPE_F)
    ang_r = row[:, None] * freqs
    ang_c = col[:, None] * freqs
    ang = jnp.concatenate([ang_r, ang_r, ang_c, ang_c], axis=1)
    cos = jnp.cos(ang)
    sin = jnp.sin(ang)
    sign = jnp.tile(jnp.concatenate([-jnp.ones(ROPE_F), jnp.ones(ROPE_F)]), 2).astype(jnp.float32)
    sin = sin * sign
    cos = jnp.tile(cos, (1, 2))
    sin = jnp.tile(sin, (1, 2))
    cos = jnp.concatenate([jnp.ones((TM, LANES), jnp.float32), cos], axis=0)
    sin = jnp.concatenate([jnp.zeros((TM, LANES), jnp.float32), sin], axis=0)
    return cos, sin


def _pair_layout(cache):
    b, l, h, n, dh = cache.shape
    c = cache.reshape(b, l, h // 2, 2, n, dh).transpose(1, 0, 2, 4, 3, 5)
    return c.reshape(l, b, h // 2, n, 2 * dh).astype(jnp.bfloat16)


def _block_diag_b(bbar):
    bb = bbar.reshape(DEPTH, 2, SSM_GROUP_CH, SSM_GROUPS, SSM_STATE)
    eye = jnp.eye(SSM_GROUPS, dtype=bbar.dtype)
    out = eye[None, None, :, None, :, None] * bb.transpose(0, 1, 3, 2, 4)[:, :, :, :, None, :]
    return out.reshape(DEPTH, 2, SSM_WIDTH, S5_HALF)


def _block_diag_c(c):
    eye = jnp.eye(SSM_GROUPS, dtype=c.dtype)
    out = eye[None, None, :, None, :, None] * c.transpose(0, 1, 2, 4, 3)[:, :, :, :, None, :]
    return out.reshape(DEPTH, 2, S5_HALF, SSM_WIDTH)


def kernel(x_prompt, x_sample, cache_na_k, cache_na_v, cache_gqa_k, cache_gqa_v, state_ssm, c, c_ctx, norm_mix_g, norm_ffn_g, w_ada, b_ada, w_in, ssm_a_re, ssm_a_im, ssm_log_dt, ssm_b_re, ssm_b_im, ssm_c_re, ssm_c_im, ssm_d, w_glu, b_glu, na_rpb, q_norm_g, k_norm_g, w_out, moe_w_group, moe_b_group, moe_w_expert, moe_b_expert, moe_w_gate, moe_w_up, moe_w_down, final_norm_g):
    f32, bf16 = jnp.float32, jnp.bfloat16
    x = jnp.concatenate([x_prompt.reshape(T_CTX, D_MODEL), x_sample.reshape(T_LAT, D_MODEL)], axis=0)

    cvec8 = jnp.concatenate([c_ctx[None, :], c, jnp.zeros((SUBLANES - N_MODSETS, D_MODEL), f32)], axis=0)
    mods_all = _adaln_all(cvec8, w_ada, b_ada)

    w_in_bf = _permute_qg_columns(w_in).astype(bf16)
    w_out_bf = _permute_yg_rows(w_out).astype(bf16)
    w_glu_bf = w_glu.astype(bf16)
    w_router = jnp.concatenate(
        [moe_w_group, moe_w_expert, jnp.zeros((DEPTH, D_MODEL, LANES - N_GROUPS - N_EXPERTS), f32)], axis=-1)
    wr_hi = w_router.astype(bf16)
    wr_lo = (w_router - wr_hi.astype(f32)).astype(bf16)
    b_router = jnp.concatenate(
        [moe_b_group, moe_b_expert, jnp.zeros((DEPTH, LANES - N_GROUPS - N_EXPERTS), f32)], axis=-1)[:, None, :]
    q_gain = jnp.tile(q_norm_g, (1, 2))
    k_gain = jnp.tile(k_norm_g, (1, 2))
    lane = np.arange(LANES)
    gsum = jnp.asarray((lane[:, None] // HEAD_DIM) == (lane[None, :] // HEAD_DIM), bf16)
    rope_cos, rope_sin = _rope_tables()

    abar_re, abar_im, bbar_re, bbar_im = _s5_prep(ssm_a_re, ssm_a_im, ssm_log_dt, ssm_b_re, ssm_b_im)
    abar = jnp.concatenate([abar_re, abar_im], axis=-1).reshape(DEPTH, 2, 1, S5_CPLX)
    bblk = jnp.concatenate([_block_diag_b(bbar_re), _block_diag_b(bbar_im)], axis=-1).astype(bf16)
    cblk = jnp.concatenate([_block_diag_c(ssm_c_re), -_block_diag_c(ssm_c_im)], axis=2).astype(bf16)
    h0 = state_ssm.transpose(1, 2, 0, 5, 3, 4).reshape(DEPTH, 2, DEC_BATCH, S5_CPLX)

    ck_na, cv_na = _pair_layout(cache_na_k), _pair_layout(cache_na_v)
    ck_g, cv_g = _pair_layout(cache_gqa_k)[:, :, 0], _pair_layout(cache_gqa_v)[:, :, 0]
    first_half = jnp.asarray(np.arange(LANES) < HEAD_DIM)
    cv_g = jnp.stack([jnp.where(first_half, cv_g, 1.0), jnp.where(first_half, 1.0, cv_g)],
                     axis=2).astype(bf16)
    na_bias = _na_bias_table(na_rpb)

    kv32_l, fin_l = [], []
    moe_residual = None
    for l in range(DEPTH):
        mods3 = mods_all[l, :N_MODSETS].reshape(N_MODSETS, 1, 6 * D_MODEL)
        u, qkv, kv32, x = _in_proj(x, moe_residual, mods3, norm_mix_g[l][None], w_in_bf[l], q_gain[l][None],
                                   k_gain[l][None], gsum, rope_cos, rope_sin)
        kv32_l.append(kv32)

        u_tm = jnp.pad(u.reshape(S5_SEQS, S5_CHUNK, SSM_WIDTH).transpose(1, 0, 2),
                       ((0, 0), (0, S5_ROWS - S5_SEQS), (0, 0)))
        y_dir, powers, finals = _s5_scan(u_tm.reshape(S5_CHUNK * S5_ROWS, SSM_WIDTH), bblk[l], abar[l], cblk[l])
        fin_l.append(finals[:, :BATCH])
        y_tm = _s5_finish(u_tm, y_dir.reshape(2, S5_CHUNK, S5_ROWS, SSM_WIDTH), powers, finals, h0[l], cblk[l],
                          ssm_d[l][None], w_glu_bf[l], b_glu[l][None])
        y_ssm = y_tm[:, :S5_SEQS].transpose(1, 0, 2).reshape(T_ALL, SSM_WIDTH)

        y_ctx = _attn_ctx(qkv)
        y_g_lat = _gqa_lat(qkv, ck_g[l], cv_g[l])
        y_na_lat = _na_lat(l, qkv, ck_na[l], cv_na[l], na_bias)

        x1, h2_tiles, group = _out_proj(l, x, y_ssm, y_ctx, y_na_lat, y_g_lat, mods3, norm_ffn_g[l][None],
                                        w_out_bf[l], wr_hi, wr_lo, b_router)
        tile_group, row_of_slot = _moe_dispatch(group[:, 0])
        y_moe = _moe(l, h2_tiles, tile_group, row_of_slot, wr_hi, wr_lo, b_router,
                     moe_w_gate, moe_w_up, moe_w_down)
        x, moe_residual = x1, (y_moe, mods3)

    y = _final_norm(x, moe_residual[0], moe_residual[1], final_norm_g[None])
    y_prompt = y[:T_CTX].reshape(BATCH, SEQ, D_MODEL)
    y_sample = y[T_CTX:].reshape(DEC_BATCH, DEC_SEQ, D_MODEL)

    kv32 = jnp.stack(kv32_l, axis=0).reshape(DEPTH, BATCH, SEQ, KV32_WIDTH)

    def heads(t, n_heads):
        return t.reshape(DEPTH, BATCH, SEQ, n_heads, HEAD_DIM).transpose(1, 0, 3, 2, 4)

    new_na_k = heads(kv32[..., 0:NA_WIDTH], NA_HEADS)
    new_na_v = heads(kv32[..., NA_WIDTH:2 * NA_WIDTH], NA_HEADS)
    new_gqa_k = heads(kv32[..., 2 * NA_WIDTH:2 * NA_WIDTH + KV_WIDTH], GQA_KV_HEADS)
    new_gqa_v = heads(kv32[..., 2 * NA_WIDTH + KV_WIDTH:], GQA_KV_HEADS)
    fin = jnp.stack(fin_l, axis=0).reshape(DEPTH, 2, BATCH, 2, SSM_GROUPS, SSM_STATE)
    new_state = fin.transpose(2, 0, 1, 4, 5, 3)
    return (y_prompt, y_sample, new_na_k, new_na_v, new_gqa_k, new_gqa_v, new_state)
```

```python
import functools
import math

import jax
import jax.numpy as jnp
import numpy as np
from jax import lax
from jax.experimental import pallas as pl
from jax.experimental.pallas import tpu as pltpu

D_MODEL = 1024
BATCH = 16
SEQ = 256
DEPTH = 4
DEC_BATCH = 2
DEC_SEQ = 4096
PAST_LEN = 512
GRID_W = 64
GRID_H = DEC_SEQ // GRID_W
HEAD_DIM = 64
SSM_WIDTH = 256
SSM_GROUP_CH = 16
SSM_GROUPS = 16
SSM_STATE = 64
NA_HEADS = 6
NA_WIDTH = 384
NA_KH = 8
NA_KW = 16
GQA_HEADS = 6
GQA_KV_HEADS = 2
GQA_REP = 3
GQA_WIDTH = 384
KV_WIDTH = 128
IN_WIDTH = 2048
ROPE_THETA = 10000.0
ROPE_F = 16
N_GROUPS = 4
EXPERTS_PER_GROUP = 4
N_EXPERTS = 16
EXPERT_HIDDEN = 256
EPS = 1e-6
NEG_INF = -1e30

T_CTX = BATCH * SEQ
T_LAT = DEC_BATCH * DEC_SEQ
T_ALL = T_CTX + T_LAT
N_MODSETS = 1 + DEC_BATCH
LANES = 128
SUBLANES = 8
PAIR = 2 * HEAD_DIM
ATTN_SCALE = HEAD_DIM ** -0.5

S5_CHUNK = 256
S5_SEQS = T_ALL // S5_CHUNK
S5_ROWS = 56
S5_POW_ROW = 48
S5_LAT0 = BATCH
S5_NLAT = S5_SEQS - BATCH
S5_TB = 32
S5_NBLK = S5_CHUNK // S5_TB
S5_CPLX = 2 * SSM_GROUPS * SSM_STATE
S5_HALF = SSM_GROUPS * SSM_STATE

TM = 512
VMEM_LIMIT = 56 * 1024 * 1024


def _cparams(*sem):
    return pltpu.CompilerParams(dimension_semantics=sem, vmem_limit_bytes=VMEM_LIMIT)


def _modset(i):
    return jnp.where(i < T_CTX // TM, 0, 1 + (i - T_CTX // TM) // (DEC_SEQ // TM))


def _bdot(a, b):
    return jnp.dot(a, b, preferred_element_type=jnp.float32)


def _bdot_t(a, b):
    return lax.dot_general(a, b, (((1,), (1,)), ((), ())), preferred_element_type=jnp.float32)


def _ada_kernel(c_ref, w_ref, b_ref, o_ref):
    c = c_ref[...]
    s = c / (1.0 + jnp.exp(-c))
    o_ref[0] = _bdot(s.astype(jnp.bfloat16), w_ref[0].astype(jnp.bfloat16)) + b_ref[0]


def _adaln_all(cvec8, w_ada, b_ada):
    tn = 1024
    return pl.pallas_call(
        _ada_kernel,
        grid=(DEPTH, 6 * D_MODEL // tn),
        in_specs=[pl.BlockSpec((SUBLANES, D_MODEL), lambda l, j: (0, 0)),
                  pl.BlockSpec((1, D_MODEL, tn), lambda l, j: (l, 0, j)),
                  pl.BlockSpec((1, 1, tn), lambda l, j: (l, 0, j))],
        out_specs=pl.BlockSpec((1, SUBLANES, tn), lambda l, j: (l, 0, j)),
        out_shape=jax.ShapeDtypeStruct((DEPTH, SUBLANES, 6 * D_MODEL), jnp.float32),
        compiler_params=_cparams("arbitrary", "arbitrary"),
        name="adaln",
    )(cvec8, w_ada, b_ada.reshape(DEPTH, 1, 6 * D_MODEL))


def _s5prep_kernel(ar_ref, ai_ref, ldt_ref, br_ref, bi_ref, abar_re, abar_im, bbar_re, bbar_im):
    ar, ai = ar_ref[0], ai_ref[0]
    dt = jnp.exp(ldt_ref[0])
    mag = jnp.exp(ar * dt)
    are = mag * jnp.cos(ai * dt)
    aim = mag * jnp.sin(ai * dt)
    zr, zi = are - 1.0, aim
    den = ar * ar + ai * ai
    cre = (zr * ar + zi * ai) / den
    cim = (zi * ar - zr * ai) / den
    abar_re[0] = are
    abar_im[0] = aim
    br, bi = br_ref[0], bi_ref[0]
    bbar_re[0] = cre * br - cim * bi
    bbar_im[0] = cre * bi + cim * br


def _s5_prep(a_re, a_im, log_dt, b_re, b_im):
    n = DEPTH * 2
    ar = a_re.reshape(n, 1, S5_HALF)
    ai = a_im.reshape(n, 1, S5_HALF)
    ldt = jnp.broadcast_to(log_dt[..., None], (DEPTH, 2, SSM_GROUPS, SSM_STATE)).reshape(n, 1, S5_HALF)
    br = b_re.transpose(0, 1, 4, 2, 3).reshape(n, SSM_GROUP_CH, S5_HALF)
    bi = b_im.transpose(0, 1, 4, 2, 3).reshape(n, SSM_GROUP_CH, S5_HALF)
    vec = jax.ShapeDtypeStruct((n, 1, S5_HALF), jnp.float32)
    mat = jax.ShapeDtypeStruct((n, SSM_GROUP_CH, S5_HALF), jnp.float32)
    vspec = pl.BlockSpec((1, 1, S5_HALF), lambda j: (j, 0, 0))
    mspec = pl.BlockSpec((1, SSM_GROUP_CH, S5_HALF), lambda j: (j, 0, 0))
    return pl.pallas_call(
        _s5prep_kernel,
        grid=(n,),
        in_specs=[vspec, vspec, vspec, mspec, mspec],
        out_specs=[vspec, vspec, mspec, mspec],
        out_shape=(vec, vec, mat, mat),
        compiler_params=_cparams("arbitrary"),
        name="s5prep",
    )(ar, ai, ldt, br, bi)


def _head_rms(blk, gsum, gain):
    sq = blk * blk
    hi = sq.astype(jnp.bfloat16)
    lo = (sq - hi.astype(jnp.float32)).astype(jnp.bfloat16)
    ms = (_bdot(hi, gsum) + _bdot(lo, gsum)) * (1.0 / HEAD_DIM)
    return blk * lax.rsqrt(ms + EPS) * gain


def _rope(blk, cos, sin_signed, first_half):
    partner = jnp.where(first_half, pltpu.roll(blk, LANES - ROPE_F, 1), pltpu.roll(blk, ROPE_F, 1))
    return blk * cos + partner * sin_signed


def _inproj_kernel(has_moe_residual, *refs):
    if has_moe_residual:
        (x_ref, y_ref, pmod_ref, mod_ref, g_ref, w_ref, qg_ref, kg_ref, gsum_ref, cos_ref, sin_ref,
         u_ref, qkv_ref, kv32_ref, xo_ref) = refs
        x = x_ref[...] + pmod_ref[0, :, 5 * D_MODEL:6 * D_MODEL] * _load_token_tiles(y_ref, TM)
        xo_ref[...] = x
    else:
        (x_ref, mod_ref, g_ref, w_ref, qg_ref, kg_ref, gsum_ref, cos_ref, sin_ref,
         u_ref, qkv_ref, kv32_ref) = refs
        x = x_ref[...]
    i = pl.program_id(0)
    shift = mod_ref[0, :, 0:D_MODEL]
    scale = mod_ref[0, :, D_MODEL:2 * D_MODEL]
    ms = jnp.mean(x * x, axis=-1, keepdims=True)
    h = (x * lax.rsqrt(ms + EPS) * g_ref[...]) * (1.0 + scale) + shift
    proj = _bdot(h.astype(jnp.bfloat16), w_ref[...])
    u_ref[...] = proj[:, 0:SSM_WIDTH]

    c0 = SSM_WIDTH
    qn = proj[:, c0:c0 + NA_WIDTH] * ATTN_SCALE
    kn = proj[:, c0 + NA_WIDTH:c0 + 2 * NA_WIDTH]
    vn = proj[:, c0 + 2 * NA_WIDTH:c0 + 3 * NA_WIDTH]
    c1 = c0 + 3 * NA_WIDTH
    vg = proj[:, c1 + GQA_WIDTH + KV_WIDTH:c1 + GQA_WIDTH + 2 * KV_WIDTH]

    gsum = gsum_ref[...]
    cos, sin = cos_ref[...], sin_ref[...]
    lane = lax.broadcasted_iota(jnp.int32, (1, LANES), 1)
    first_half = (lane & (2 * ROPE_F - 1)) < ROPE_F
    qg_blocks = []
    for p in range(GQA_WIDTH // PAIR):
        blk = proj[:, c1 + p * PAIR:c1 + (p + 1) * PAIR]
        blk = _rope(_head_rms(blk, gsum, qg_ref[...]), cos, sin, first_half)
        qg_blocks.append(blk * ATTN_SCALE)
    kg = proj[:, c1 + GQA_WIDTH:c1 + GQA_WIDTH + KV_WIDTH]
    kg = _rope(_head_rms(kg, gsum, kg_ref[...]), cos, sin, first_half)

    o = 0
    vg_ones = [jnp.where(lane < HEAD_DIM, vg, 1.0), jnp.where(lane >= HEAD_DIM, vg, 1.0)]
    for piece in [qn, kn, vn] + qg_blocks + [kg] + vg_ones:
        w = piece.shape[1]
        qkv_ref[:, o:o + w] = piece.astype(jnp.bfloat16)
        o += w

    @pl.when(i < T_CTX // TM)
    def _():
        kv32_ref[:, 0:NA_WIDTH] = kn
        kv32_ref[:, NA_WIDTH:2 * NA_WIDTH] = vn
        kv32_ref[:, 2 * NA_WIDTH:2 * NA_WIDTH + KV_WIDTH] = kg
        kv32_ref[:, 2 * NA_WIDTH + KV_WIDTH:2 * NA_WIDTH + 2 * KV_WIDTH] = vg


QKV_WIDTH = 3 * NA_WIDTH + GQA_WIDTH + 3 * KV_WIDTH
KV32_WIDTH = 2 * NA_WIDTH + 2 * KV_WIDTH
QN_BLK, KN_BLK, VN_BLK, QG_BLK, KG_BLK, VG1_BLK = 0, 3, 6, 9, 12, 13


def _rope_block(i):
    n_ctx = T_CTX // TM
    return jnp.where(i < n_ctx, 0, 1 + (i - n_ctx) % (DEC_SEQ // TM))


def _in_proj(x, moe_residual, mods3, norm_g, w_in_bf, q_gain, k_gain, gsum, rope_cos, rope_sin):
    n_ctx = T_CTX // TM
    row_spec = pl.BlockSpec((TM, D_MODEL), lambda i: (i, 0))
    mod_spec = pl.BlockSpec((1, 1, 6 * D_MODEL), lambda i: (_modset(i), 0, 0))
    args, in_specs = [x], [row_spec]
    out_specs = [pl.BlockSpec((TM, SSM_WIDTH), lambda i: (i, 0)),
                 pl.BlockSpec((TM, QKV_WIDTH), lambda i: (i, 0)),
                 pl.BlockSpec((TM, KV32_WIDTH), lambda i: (jnp.minimum(i, n_ctx - 1), 0))]
    out_shape = [jax.ShapeDtypeStruct((T_ALL, SSM_WIDTH), jnp.float32),
                 jax.ShapeDtypeStruct((T_ALL, QKV_WIDTH), jnp.bfloat16),
                 jax.ShapeDtypeStruct((T_CTX, KV32_WIDTH), jnp.float32)]
    if moe_residual is not None:
        args += list(moe_residual)
        in_specs += [pl.BlockSpec((TM * TILE_ROWS, LANES), lambda i: (i, 0)), mod_spec]
        out_specs.append(row_spec)
        out_shape.append(jax.ShapeDtypeStruct((T_ALL, D_MODEL), jnp.float32))
    args += [mods3, norm_g, w_in_bf, q_gain, k_gain, gsum, rope_cos, rope_sin]
    in_specs += [mod_spec,
                 pl.BlockSpec((1, D_MODEL), lambda i: (0, 0)),
                 pl.BlockSpec((D_MODEL, IN_WIDTH), lambda i: (0, 0)),
                 pl.BlockSpec((1, LANES), lambda i: (0, 0)),
                 pl.BlockSpec((1, LANES), lambda i: (0, 0)),
                 pl.BlockSpec((LANES, LANES), lambda i: (0, 0)),
                 pl.BlockSpec((TM, LANES), lambda i: (_rope_block(i), 0)),
                 pl.BlockSpec((TM, LANES), lambda i: (_rope_block(i), 0))]
    outs = pl.pallas_call(
        functools.partial(_inproj_kernel, moe_residual is not None),
        grid=(T_ALL // TM,),
        in_specs=in_specs,
        out_specs=out_specs,
        out_shape=out_shape,
        compiler_params=_cparams("arbitrary"),
        name="in_proj",
    )(*args)
    return outs if moe_residual is not None else (*outs, x)


def _s5scan_kernel(u_ref, b_ref, a_ref, c_ref, y_ref, p_ref, fin_ref, hbuf):
    d = pl.program_id(0)
    k = pl.program_id(1)
    rows = S5_TB * S5_ROWS
    a_re = jnp.broadcast_to(a_ref[0, :, 0:S5_HALF], (SUBLANES, S5_HALF))
    a_im = jnp.broadcast_to(a_ref[0, :, S5_HALF:S5_CPLX], (SUBLANES, S5_HALF))
    ub = u_ref[...].astype(jnp.bfloat16)

    def run(forward):
        data0 = S5_ROWS if forward else 0
        prev0 = 0 if forward else rows
        last0 = rows if forward else 0

        @pl.when(k == 0)
        def _():
            r = lax.broadcasted_iota(jnp.int32, (S5_ROWS, S5_CPLX), 0)
            l = lax.broadcasted_iota(jnp.int32, (S5_ROWS, S5_CPLX), 1)
            hbuf[prev0:prev0 + S5_ROWS, :] = jnp.where((r == S5_POW_ROW) & (l < S5_HALF), 1.0, 0.0)

        @pl.when(k > 0)
        def _():
            hbuf[prev0:prev0 + S5_ROWS, :] = hbuf[last0:last0 + S5_ROWS, :]

        hbuf[data0:data0 + rows, :] = _bdot(ub, b_ref[0])

        def step(i, carry):
            t = i if forward else S5_TB - 1 - i
            cur = pl.multiple_of(data0 + t * S5_ROWS, SUBLANES)
            prv = pl.multiple_of(cur - S5_ROWS if forward else cur + S5_ROWS, SUBLANES)
            for st in range(S5_ROWS // SUBLANES):
                rc = pl.ds(cur + st * SUBLANES, SUBLANES)
                rp = pl.ds(prv + st * SUBLANES, SUBLANES)
                hr = hbuf[rp, 0:S5_HALF]
                hi = hbuf[rp, S5_HALF:S5_CPLX]
                nr = a_re * hr - a_im * hi + hbuf[rc, 0:S5_HALF]
                ni = a_re * hi + a_im * hr + hbuf[rc, S5_HALF:S5_CPLX]
                hbuf[rc, 0:S5_HALF] = nr
                hbuf[rc, S5_HALF:S5_CPLX] = ni
            return carry

        lax.fori_loop(0, S5_TB, step, 0, unroll=True)
        y_ref[0] = _bdot(hbuf[data0:data0 + rows, :].astype(jnp.bfloat16), c_ref[0])
        for t in range(S5_TB):
            r0 = data0 + t * S5_ROWS + S5_POW_ROW
            p_ref[0, t:t + 1, :] = hbuf[r0:r0 + 1, :]

        @pl.when(k == S5_NBLK - 1)
        def _():
            fin_ref[0] = hbuf[last0:last0 + S5_ROWS, :]

    @pl.when(d == 0)
    def _():
        run(True)

    @pl.when(d == 1)
    def _():
        run(False)


def _s5_scan(u_tm, bblk, abar, cblk):
    rows = S5_TB * S5_ROWS

    def tblk(d, k):
        return jnp.where(d == 0, k, S5_NBLK - 1 - k)

    return pl.pallas_call(
        _s5scan_kernel,
        grid=(2, S5_NBLK),
        in_specs=[pl.BlockSpec((rows, SSM_WIDTH), lambda d, k: (tblk(d, k), 0)),
                  pl.BlockSpec((1, SSM_WIDTH, S5_CPLX), lambda d, k: (d, 0, 0)),
                  pl.BlockSpec((1, 1, S5_CPLX), lambda d, k: (d, 0, 0)),
                  pl.BlockSpec((1, S5_CPLX, SSM_WIDTH), lambda d, k: (d, 0, 0))],
        out_specs=[pl.BlockSpec((1, rows, SSM_WIDTH), lambda d, k: (d, tblk(d, k), 0)),
                   pl.BlockSpec((1, S5_TB, S5_CPLX), lambda d, k: (d, tblk(d, k), 0)),
                   pl.BlockSpec((1, S5_ROWS, S5_CPLX), lambda d, k: (d, 0, 0))],
        out_shape=(jax.ShapeDtypeStruct((2, S5_CHUNK * S5_ROWS, SSM_WIDTH), jnp.float32),
                   jax.ShapeDtypeStruct((2, S5_CHUNK, S5_CPLX), jnp.float32),
                   jax.ShapeDtypeStruct((2, S5_ROWS, S5_CPLX), jnp.float32)),
        scratch_shapes=[pltpu.VMEM((rows + S5_ROWS, S5_CPLX), jnp.float32)],
        compiler_params=_cparams("arbitrary", "arbitrary"),
        name="s5_scan",
    )(u_tm, bblk, abar, cblk)


def _cmul(pr, pi, cr, ci):
    return pr * cr - pi * ci, pr * ci + pi * cr


def _s5fin_kernel(u_ref, yf_ref, yb_ref, p_ref, fin_ref, h0_ref, c_ref, d_ref, wg_ref, bg_ref,
                  o_ref, carry, hc, ybuf):
    k = pl.program_id(0)
    nchunk = DEC_SEQ // S5_CHUNK

    @pl.when(k == 0)
    def _():
        for d in range(2):
            t_full = S5_CHUNK - 1 if d == 0 else 0
            pr = p_ref[d, t_full:t_full + 1, 0:S5_HALF]
            pi = p_ref[d, t_full:t_full + 1, S5_HALF:S5_CPLX]
            for b in range(DEC_BATCH):
                cr = h0_ref[d, b:b + 1, 0:S5_HALF]
                ci = h0_ref[d, b:b + 1, S5_HALF:S5_CPLX]
                order = range(nchunk) if d == 0 else range(nchunk - 1, -1, -1)
                for c in order:
                    row = b * nchunk + c
                    carry[d, row:row + 1, 0:S5_HALF] = cr
                    carry[d, row:row + 1, S5_HALF:S5_CPLX] = ci
                    fr = fin_ref[d, S5_LAT0 + row:S5_LAT0 + row + 1, 0:S5_HALF]
                    fi = fin_ref[d, S5_LAT0 + row:S5_LAT0 + row + 1, S5_HALF:S5_CPLX]
                    mr, mi = _cmul(pr, pi, cr, ci)
                    cr, ci = fr + mr, fi + mi

    ybuf[...] = yf_ref[0] + yb_ref[0] + u_ref[...] * d_ref[...]
    for d in range(2):
        cr = carry[d, :, 0:S5_HALF]
        ci = carry[d, :, S5_HALF:S5_CPLX]
        for t in range(S5_TB):
            pr = p_ref[d, pl.ds(k * S5_TB + t, 1), 0:S5_HALF]
            pi = p_ref[d, pl.ds(k * S5_TB + t, 1), S5_HALF:S5_CPLX]
            mr, mi = _cmul(pr, pi, cr, ci)
            hc[t * S5_NLAT:(t + 1) * S5_NLAT, 0:S5_HALF] = mr.astype(jnp.bfloat16)
            hc[t * S5_NLAT:(t + 1) * S5_NLAT, S5_HALF:S5_CPLX] = mi.astype(jnp.bfloat16)
        corr = _bdot(hc[...], c_ref[d])
        ybuf[:, S5_LAT0:S5_LAT0 + S5_NLAT, :] += corr.reshape(S5_TB, S5_NLAT, SSM_WIDTH)

    y = ybuf[...].reshape(S5_TB * S5_ROWS, SSM_WIDTH)
    yg = 0.5 * y * (1.0 + jnp.tanh(math.sqrt(2.0 / math.pi) * (y + 0.044715 * (y * y * y))))
    z = _bdot(yg.astype(jnp.bfloat16), wg_ref[...]) + bg_ref[...]
    out = yg / (1.0 + jnp.exp(-z))
    o_ref[...] = out.reshape(S5_TB, S5_ROWS, SSM_WIDTH).astype(jnp.bfloat16)


def _s5_finish(u_tm3, y_dir4, powers, finals, h0, cblk, ssm_d, w_glu_bf, b_glu):
    blk3 = (S5_TB, S5_ROWS, SSM_WIDTH)
    return pl.pallas_call(
        _s5fin_kernel,
        grid=(S5_NBLK,),
        in_specs=[pl.BlockSpec(blk3, lambda k: (k, 0, 0)),
                  pl.BlockSpec((1,) + blk3, lambda k: (0, k, 0, 0)),
                  pl.BlockSpec((1,) + blk3, lambda k: (1, k, 0, 0)),
                  pl.BlockSpec((2, S5_CHUNK, S5_CPLX), lambda k: (0, 0, 0)),
                  pl.BlockSpec((2, S5_ROWS, S5_CPLX), lambda k: (0, 0, 0)),
                  pl.BlockSpec((2, DEC_BATCH, S5_CPLX), lambda k: (0, 0, 0)),
                  pl.BlockSpec((2, S5_CPLX, SSM_WIDTH), lambda k: (0, 0, 0)),
                  pl.BlockSpec((1, SSM_WIDTH), lambda k: (0, 0)),
                  pl.BlockSpec((SSM_WIDTH, SSM_WIDTH), lambda k: (0, 0)),
                  pl.BlockSpec((1, SSM_WIDTH), lambda k: (0, 0))],
        out_specs=pl.BlockSpec(blk3, lambda k: (k, 0, 0)),
        out_shape=jax.ShapeDtypeStruct((S5_CHUNK, S5_ROWS, SSM_WIDTH), jnp.bfloat16),
        scratch_shapes=[pltpu.VMEM((2, S5_NLAT, S5_CPLX), jnp.float32),
                        pltpu.VMEM((S5_TB * S5_NLAT, S5_CPLX), jnp.bfloat16),
                        pltpu.VMEM(blk3, jnp.float32)],
        compiler_params=_cparams("arbitrary"),
        name="s5_finish",
    )(u_tm3, y_dir4, y_dir4, powers, finals, h0, cblk, ssm_d, w_glu_bf, b_glu)


def _pair_attention(q, kvs, biases):
    lane = lax.broadcasted_iota(jnp.int32, (1, PAIR), 1)
    outs = []
    for a in range(2):
        sel = (lane < HEAD_DIM) if a == 0 else (lane >= HEAD_DIM)
        qa = jnp.where(sel, q, jnp.zeros_like(q))
        ss = []
        for (kk, _), bias in zip(kvs, biases):
            s = _bdot_t(qa, kk)
            if bias is not None:
                s = s + bias[a]
            ss.append(s)
        m = ss[0].max(axis=-1, keepdims=True)
        for s in ss[1:]:
            m = jnp.maximum(m, s.max(axis=-1, keepdims=True))
        den = None
        acc = None
        for s, (_, vv) in zip(ss, kvs):
            p = jnp.exp(s - m)
            l = p.sum(axis=-1, keepdims=True)
            o = _bdot(p.astype(jnp.bfloat16), vv)
            den = l if den is None else den + l
            acc = o if acc is None else acc + o
        outs.append(acc / den)
    return jnp.where(lane < HEAD_DIM, outs[0], outs[1])


def _pair_attention_vones(q, kvs):
    lane = lax.broadcasted_iota(jnp.int32, (1, PAIR), 1)
    outs = []
    for a in range(2):
        sel = (lane < HEAD_DIM) if a == 0 else (lane >= HEAD_DIM)
        qa = jnp.where(sel, q, jnp.zeros_like(q))
        ss = [_bdot_t(qa, kv[0]) for kv in kvs]
        m = ss[0].max(axis=-1, keepdims=True)
        for s in ss[1:]:
            m = jnp.maximum(m, s.max(axis=-1, keepdims=True))
        acc = None
        for s, kv in zip(ss, kvs):
            o = _bdot(jnp.exp((s - m).astype(jnp.bfloat16)), kv[1 + a])
            acc = o if acc is None else acc + o
        den = pltpu.roll(acc, HEAD_DIM, 1)
        outs.append(acc / den)
    return jnp.where(lane < HEAD_DIM, outs[0], outs[1])


ATTN_WIDTH = NA_WIDTH + GQA_WIDTH
NPAIR = NA_WIDTH // PAIR


def _attn_ctx_kernel(qkv_ref, o_ref):
    def blk(j):
        return qkv_ref[:, j * PAIR:(j + 1) * PAIR]

    for p in range(NPAIR):
        o = _pair_attention(blk(QN_BLK + p), [(blk(KN_BLK + p), blk(VN_BLK + p))], [None])
        o_ref[:, p * PAIR:(p + 1) * PAIR] = o.astype(jnp.bfloat16)
    for p in range(NPAIR):
        o = _pair_attention_vones(blk(QG_BLK + p), [(blk(KG_BLK), blk(VG1_BLK), blk(VG1_BLK + 1))])
        o_ref[:, (NPAIR + p) * PAIR:(NPAIR + p + 1) * PAIR] = o.astype(jnp.bfloat16)


def _attn_ctx(qkv):
    return pl.pallas_call(
        _attn_ctx_kernel,
        grid=(BATCH,),
        in_specs=[pl.BlockSpec((SEQ, QKV_WIDTH), lambda b: (b, 0))],
        out_specs=pl.BlockSpec((SEQ, ATTN_WIDTH), lambda b: (b, 0)),
        out_shape=jax.ShapeDtypeStruct((T_CTX, ATTN_WIDTH), jnp.bfloat16),
        compiler_params=_cparams("arbitrary"),
        name="attn_ctx",
    )(qkv)


TQ = 256


def _gqa_lat_kernel(q_ref, k_ref, v0_ref, v1_ref, ck_ref, cv_ref, o_ref):
    kvs = [(ck_ref[0], cv_ref[0, 0], cv_ref[0, 1]), (k_ref[...], v0_ref[...], v1_ref[...])]
    o_ref[...] = _pair_attention_vones(q_ref[...], kvs).astype(jnp.bfloat16)


def _gqa_lat(qkv, ck, cv_ones):
    nq = DEC_SEQ // TQ
    ctx_blocks = T_CTX // DEC_SEQ
    return pl.pallas_call(
        _gqa_lat_kernel,
        grid=(DEC_BATCH, NPAIR, nq),
        in_specs=[pl.BlockSpec((TQ, PAIR), lambda b, p, i: (T_CTX // TQ + b * nq + i, QG_BLK + p)),
                  pl.BlockSpec((DEC_SEQ, PAIR), lambda b, p, i: (ctx_blocks + b, KG_BLK)),
                  pl.BlockSpec((DEC_SEQ, PAIR), lambda b, p, i: (ctx_blocks + b, VG1_BLK)),
                  pl.BlockSpec((DEC_SEQ, PAIR), lambda b, p, i: (ctx_blocks + b, VG1_BLK + 1)),
                  pl.BlockSpec((1, PAST_LEN, PAIR), lambda b, p, i: (b, 0, 0)),
                  pl.BlockSpec((1, 2, PAST_LEN, PAIR), lambda b, p, i: (b, 0, 0, 0))],
        out_specs=pl.BlockSpec((TQ, PAIR), lambda b, p, i: (b * nq + i, p)),
        out_shape=jax.ShapeDtypeStruct((T_LAT, GQA_WIDTH), jnp.bfloat16),
        compiler_params=_cparams("arbitrary", "arbitrary", "arbitrary"),
        name="gqa_lat",
    )(qkv, qkv, qkv, qkv, ck, cv_ones)


NA_BLK = 4
NA_NBLK = GRID_H // NA_BLK
NA_SLAB_ROWS = NA_KH + NA_BLK - 1
NA_SLAB = NA_SLAB_ROWS * GRID_W
NA_QROWS = NA_BLK * GRID_W
NA_BLOCK_TYPES = (0, 2, NA_NBLK - 1)


def _na_slab_start(j):
    return jnp.clip(j * NA_BLK - NA_KH // 2, 0, GRID_H - NA_SLAB_ROWS)


def _na_lat_kernel(q_ref, k_ref, v_ref, ck_ref, cv_ref, bias_ref, o_ref):
    ck, cv = ck_ref[0, 0], cv_ref[0, 0]
    lane = lax.broadcasted_iota(jnp.int32, (1, PAIR), 1)

    def block(j, carry):
        q0 = pl.multiple_of(j * NA_QROWS, NA_QROWS)
        k0 = pl.multiple_of(_na_slab_start(j) * GRID_W, GRID_W)
        btype = jnp.where(j == 0, 0, jnp.where(j == NA_NBLK - 1, 2, 1))
        q = q_ref[pl.ds(q0, NA_QROWS), :]
        lhs = jnp.concatenate([jnp.where(lane < HEAD_DIM, q, jnp.zeros_like(q)),
                               jnp.where(lane >= HEAD_DIM, q, jnp.zeros_like(q))], axis=0)
        ks = k_ref[pl.ds(k0, NA_SLAB), :]
        vs = v_ref[pl.ds(k0, NA_SLAB), :]
        s_lat = _bdot_t(lhs, ks) + bias_ref[0, 0, btype]
        s_ctx = _bdot_t(lhs, ck)
        m = jnp.maximum(s_lat.max(axis=-1, keepdims=True), s_ctx.max(axis=-1, keepdims=True))
        p_lat = jnp.exp((s_lat - m).astype(jnp.bfloat16))
        p_ctx = jnp.exp((s_ctx - m).astype(jnp.bfloat16))
        den = (_bdot(p_lat, jnp.ones((NA_SLAB, PAIR), jnp.bfloat16))
               + _bdot(p_ctx, jnp.ones((PAST_LEN, PAIR), jnp.bfloat16)))
        o = (_bdot(p_lat, vs) + _bdot(p_ctx, cv)) / den
        out = jnp.where(lane < HEAD_DIM, o[0:NA_QROWS], o[NA_QROWS:2 * NA_QROWS])
        o_ref[pl.ds(q0, NA_QROWS), :] = out.astype(jnp.bfloat16)
        return carry

    lax.fori_loop(0, NA_NBLK, block, 0)


def _na_lat(layer, qkv, ck, cv, bias):
    ctx_blocks = T_CTX // DEC_SEQ
    return pl.pallas_call(
        _na_lat_kernel,
        grid=(DEC_BATCH, NPAIR),
        in_specs=[pl.BlockSpec((DEC_SEQ, PAIR), lambda b, p: (ctx_blocks + b, QN_BLK + p)),
                  pl.BlockSpec((DEC_SEQ, PAIR), lambda b, p: (ctx_blocks + b, KN_BLK + p)),
                  pl.BlockSpec((DEC_SEQ, PAIR), lambda b, p: (ctx_blocks + b, VN_BLK + p)),
                  pl.BlockSpec((1, 1, PAST_LEN, PAIR), lambda b, p: (b, p, 0, 0)),
                  pl.BlockSpec((1, 1, PAST_LEN, PAIR), lambda b, p: (b, p, 0, 0)),
                  pl.BlockSpec((1, 1, len(NA_BLOCK_TYPES), 2 * NA_QROWS, NA_SLAB),
                               lambda b, p: (layer, p, 0, 0, 0))],
        out_specs=pl.BlockSpec((DEC_SEQ, PAIR), lambda b, p: (b, p)),
        out_shape=jax.ShapeDtypeStruct((T_LAT, NA_WIDTH), jnp.bfloat16),
        compiler_params=_cparams("arbitrary", "arbitrary"),
        name="na_lat",
    )(qkv, qkv, qkv, ck, cv, bias)


def _na_bias_table(rpb):
    nt = len(NA_BLOCK_TYPES)
    nl = rpb.shape[0]
    j = np.array(NA_BLOCK_TYPES)[:, None]
    r = j * NA_BLK + np.arange(NA_BLK)[None, :]
    slab0 = np.clip(j * NA_BLK - NA_KH // 2, 0, GRID_H - NA_SLAB_ROWS)
    win0 = np.clip(r - NA_KH // 2, 0, GRID_H - NA_KH)
    key_row = slab0[:, :, None] + np.arange(NA_SLAB_ROWS)[None, None, :]
    row_in = (key_row >= win0[:, :, None]) & (key_row < win0[:, :, None] + NA_KH)
    dr = np.clip(key_row - r[:, :, None] + (NA_KH - 1), 0, 2 * NA_KH - 2)
    cq = np.arange(GRID_W)
    ck = np.arange(GRID_W)
    cs = np.clip(cq - NA_KW // 2, 0, GRID_W - NA_KW)
    col_in = (ck[None, :] >= cs[:, None]) & (ck[None, :] < cs[:, None] + NA_KW)
    dc = np.clip(ck[None, :] - cq[:, None] + NA_KW - 1, 0, 2 * NA_KW - 2)
    sel_c = jnp.asarray(np.eye(2 * NA_KW - 1, dtype=np.float32)[dc])
    cols = jnp.einsum("lhdc,qkc->lhdqk", rpb.astype(jnp.float32), sel_c, precision=lax.Precision.HIGHEST)
    cols = jnp.where(jnp.asarray(col_in), cols, NEG_INF)
    n_off = 2 * NA_KH - 1
    cols = cols.reshape(nl, NA_HEADS // 2, 2, n_off, GRID_W, GRID_W)
    masked = jnp.full((nl, NA_HEADS // 2, 2, 1, GRID_W, GRID_W), NEG_INF, jnp.float32)
    cols = jnp.concatenate([cols, masked], axis=3)
    which = jnp.asarray(np.where(row_in, dr, n_off).reshape(-1), jnp.int32)

    def assemble(which_ref, cols_ref, o_ref):
        t = pl.program_id(2)
        for e in range(2):
            for n in range(NA_BLK):
                r0 = (e * NA_BLK + n) * GRID_W
                for i in range(NA_SLAB_ROWS):
                    d = which_ref[(t * NA_BLK + n) * NA_SLAB_ROWS + i]
                    o_ref[0, 0, 0, r0:r0 + GRID_W, i * GRID_W:(i + 1) * GRID_W] = cols_ref[0, 0, e, d]

    grid_spec = pltpu.PrefetchScalarGridSpec(
        num_scalar_prefetch=1,
        grid=(nl, NA_HEADS // 2, nt),
        in_specs=[pl.BlockSpec((1, 1, 2, n_off + 1, GRID_W, GRID_W), lambda l, p, t, w: (l, p, 0, 0, 0, 0))],
        out_specs=pl.BlockSpec((1, 1, 1, 2 * NA_QROWS, NA_SLAB), lambda l, p, t, w: (l, p, t, 0, 0)))
    return pl.pallas_call(
        assemble,
        grid_spec=grid_spec,
        out_shape=jax.ShapeDtypeStruct((nl, NA_HEADS // 2, nt, 2 * NA_QROWS, NA_SLAB), jnp.float32),
        compiler_params=_cparams("arbitrary", "arbitrary", "arbitrary"),
        name="na_bias",
    )(which, cols)


GATE_LANE0 = N_GROUPS


def _router_logits(h2, wr_hi, wr_lo, br):
    h_hi = h2.astype(jnp.bfloat16)
    h_lo = (h2 - h_hi.astype(jnp.float32)).astype(jnp.bfloat16)
    return _bdot(h_hi, wr_hi) + _bdot(h_lo, wr_hi) + _bdot(h_hi, wr_lo) + br


def _top_group(logits):
    lane_i = lax.broadcasted_iota(jnp.int32, logits.shape, 1)
    is_g = lane_i < N_GROUPS
    lg = jnp.where(is_g, logits, NEG_INF)
    mg = lg.max(axis=-1, keepdims=True)
    return jnp.where(is_g & (lg == mg), lane_i.astype(jnp.float32), float(LANES)).min(axis=-1, keepdims=True)


def _gates_in_group(logits, group):
    lane_i = lax.broadcasted_iota(jnp.int32, logits.shape, 1)
    lane = lane_i.astype(jnp.float32)
    big = float(LANES)
    is_g = lane_i < N_GROUPS
    lg = jnp.where(is_g, logits, NEG_INF)
    mg = lg.max(axis=-1, keepdims=True)
    own = jnp.where(lane_i == group, logits, 0.0).sum(axis=-1, keepdims=True)
    pg_top = jnp.exp(own - mg) / jnp.where(is_g, jnp.exp(lg - mg), 0.0).sum(axis=-1, keepdims=True)
    lane0 = GATE_LANE0 + group * EXPERTS_PER_GROUP
    sel = (lane_i >= lane0) & (lane_i < lane0 + EXPERTS_PER_GROUP)
    le = jnp.where(sel, logits, NEG_INF)
    me = le.max(axis=-1, keepdims=True)
    ex = jnp.where(sel, jnp.exp(le - me), 0.0)
    pe = ex / ex.sum(axis=-1, keepdims=True)
    p1 = pe.max(axis=-1, keepdims=True)
    i1 = jnp.where(sel & (pe == p1), lane, big).min(axis=-1, keepdims=True)
    rest = sel & (lane != i1)
    pr = jnp.where(rest, pe, -1.0)
    p2 = pr.max(axis=-1, keepdims=True)
    i2 = jnp.where(rest & (pr == p2), lane, big).min(axis=-1, keepdims=True)
    tot = p1 + p2
    within = jnp.where(lane == i1, p1 / tot, 0.0) + jnp.where(lane == i2, p2 / tot, 0.0)
    return pg_top * within


TILE_ROWS = D_MODEL // LANES


def _store_token_tiles(ref, x):
    n = x.shape[0]
    for c in range(TILE_ROWS):
        ref[pl.ds(c, n, stride=TILE_ROWS), :] = x[:, c * LANES:(c + 1) * LANES]


def _load_token_tiles(ref, n):
    return jnp.concatenate([ref[pl.ds(c, n, stride=TILE_ROWS), :] for c in range(TILE_ROWS)], axis=1)


def _outproj_kernel(x_ref, ys_ref, yc_ref, yn_ref, yg_ref, mod_ref, g_ref, w_ref, wr_hi_ref, wr_lo_ref, br_ref,
                    x1_ref, h2t_ref, grp_ref):
    is_ctx = pl.program_id(0) < T_CTX // TM
    y_na = jnp.where(is_ctx, yc_ref[:, 0:NA_WIDTH], yn_ref[...])
    y_g = jnp.where(is_ctx, yc_ref[:, NA_WIDTH:ATTN_WIDTH], yg_ref[...])
    mix = (_bdot(ys_ref[...], w_ref[0:SSM_WIDTH, :])
           + _bdot(y_na, w_ref[SSM_WIDTH:SSM_WIDTH + NA_WIDTH, :])
           + _bdot(y_g, w_ref[SSM_WIDTH + NA_WIDTH:, :]))
    gate1 = mod_ref[0, :, 2 * D_MODEL:3 * D_MODEL]
    shift2 = mod_ref[0, :, 3 * D_MODEL:4 * D_MODEL]
    scale2 = mod_ref[0, :, 4 * D_MODEL:5 * D_MODEL]
    x1 = x_ref[...] + gate1 * mix
    x1_ref[...] = x1
    ms = jnp.mean(x1 * x1, axis=-1, keepdims=True)
    h2 = (x1 * lax.rsqrt(ms + EPS) * g_ref[...]) * (1.0 + scale2) + shift2
    _store_token_tiles(h2t_ref, h2)
    logits = _router_logits(h2, wr_hi_ref[0], wr_lo_ref[0], br_ref[0])
    grp_ref[...] = jnp.broadcast_to(_top_group(logits), (TM, LANES))


def _out_proj(layer, x, y_ssm, y_ctx, y_na_lat, y_g_lat, mods3, norm_g, w_out_bf, wr_hi, wr_lo, b_router):
    n_ctx = T_CTX // TM
    return pl.pallas_call(
        _outproj_kernel,
        grid=(T_ALL // TM,),
        in_specs=[pl.BlockSpec((TM, D_MODEL), lambda i: (i, 0)),
                  pl.BlockSpec((TM, SSM_WIDTH), lambda i: (i, 0)),
                  pl.BlockSpec((TM, ATTN_WIDTH), lambda i: (jnp.minimum(i, n_ctx - 1), 0)),
                  pl.BlockSpec((TM, NA_WIDTH), lambda i: (jnp.maximum(i - n_ctx, 0), 0)),
                  pl.BlockSpec((TM, GQA_WIDTH), lambda i: (jnp.maximum(i - n_ctx, 0), 0)),
                  pl.BlockSpec((1, 1, 6 * D_MODEL), lambda i: (_modset(i), 0, 0)),
                  pl.BlockSpec((1, D_MODEL), lambda i: (0, 0)),
                  pl.BlockSpec((D_MODEL, D_MODEL), lambda i: (0, 0)),
                  pl.BlockSpec((1, D_MODEL, LANES), lambda i: (layer, 0, 0)),
                  pl.BlockSpec((1, D_MODEL, LANES), lambda i: (layer, 0, 0)),
                  pl.BlockSpec((1, 1, LANES), lambda i: (layer, 0, 0))],
        out_specs=[pl.BlockSpec((TM, D_MODEL), lambda i: (i, 0)),
                   pl.BlockSpec((TM * TILE_ROWS, LANES), lambda i: (i, 0)),
                   pl.BlockSpec((TM, LANES), lambda i: (i, 0))],
        out_shape=(jax.ShapeDtypeStruct((T_ALL, D_MODEL), jnp.float32),
                   jax.ShapeDtypeStruct((T_ALL * TILE_ROWS, LANES), jnp.float32),
                   jax.ShapeDtypeStruct((T_ALL, LANES), jnp.float32)),
        compiler_params=_cparams("arbitrary"),
        name="out_proj",
    )(x, y_ssm, y_ctx, y_na_lat, y_g_lat, mods3, norm_g, w_out_bf, wr_hi, wr_lo, b_router)


TMOE = 256
NT_MOE = T_ALL // TMOE + N_GROUPS
MOE_SLOTS = NT_MOE * TMOE


def _moe_dispatch(group_f):
    i32 = jnp.int32
    g = group_f.astype(i32)
    onehot = (g[:, None] == jnp.arange(N_GROUPS, dtype=i32)[None, :]).astype(i32)
    csum = jnp.cumsum(onehot, axis=0)
    counts = csum[-1]
    rank = jnp.sum(onehot * csum, axis=1) - 1
    tiles = (counts + TMOE - 1) // TMOE
    tile_end = jnp.cumsum(tiles)
    tile_start = tile_end - tiles
    slot = jnp.sum(onehot * tile_start[None, :], axis=1) * TMOE + rank
    tok = jnp.full((MOE_SLOTS,), -1, i32).at[slot].set(jnp.arange(T_ALL, dtype=i32))
    pad = tok < 0
    pad_rank = jnp.cumsum(pad.astype(i32)) - 1
    row_of_slot = jnp.where(pad, T_ALL + pad_rank, tok) * TILE_ROWS
    tile_group = jnp.minimum(
        jnp.sum(jnp.arange(NT_MOE, dtype=i32)[:, None] >= tile_end[None, :], axis=1), N_GROUPS - 1).astype(i32)
    return tile_group, row_of_slot


def _moe_kernel(tg_ref, row_ref, h_hbm, wr_hi_ref, wr_lo_ref, br_ref, wg_ref, wu_ref, wd_ref, y_hbm,
                gbuf, ybuf, wg_bf, wu_bf, wd_bf, gsem, ssem):
    i = pl.program_id(0)
    slot = lax.rem(i, 2)
    other = 1 - slot
    last = NT_MOE - 1
    nxt = jnp.minimum(i + 1, last)
    prv = jnp.maximum(i - 1, 0)
    buf_rows = TMOE * TILE_ROWS

    def gather_tok(row, buf_slot, r):
        return pltpu.make_async_copy(h_hbm.at[pl.ds(row, TILE_ROWS)],
                                     gbuf.at[buf_slot, pl.ds(r * TILE_ROWS, TILE_ROWS)], gsem.at[buf_slot])

    def scatter_tok(row, buf_slot, r):
        return pltpu.make_async_copy(ybuf.at[buf_slot, pl.ds(r * TILE_ROWS, TILE_ROWS)],
                                     y_hbm.at[pl.ds(row, TILE_ROWS)], ssem.at[buf_slot])

    def wait_gather(buf_slot):
        pltpu.make_async_copy(h_hbm.at[pl.ds(0, buf_rows)], gbuf.at[buf_slot], gsem.at[buf_slot]).wait()

    def wait_scatter(buf_slot):
        pltpu.make_async_copy(ybuf.at[buf_slot], y_hbm.at[pl.ds(0, buf_rows)], ssem.at[buf_slot]).wait()

    def start_gather_tok(tile, buf_slot, r):
        row = row_ref[tile * TMOE + r]
        gather_tok(jnp.where(row < T_ALL * TILE_ROWS, row, 0), buf_slot, r).start()

    @pl.when(i == 0)
    def _():
        ybuf[1] = jnp.zeros((buf_rows, LANES), jnp.float32)
        for r in range(TMOE):
            start_gather_tok(0, 0, r)

    first_of_group = (i == 0) | (tg_ref[i] != tg_ref[prv])

    @pl.when(first_of_group)
    def _():
        wg_bf[...] = wg_ref[0].astype(jnp.bfloat16)
        wu_bf[...] = wu_ref[0].astype(jnp.bfloat16)
        wd_bf[...] = wd_ref[0].astype(jnp.bfloat16)

    wait_gather(slot)

    h2 = _load_token_tiles(gbuf.at[slot], TMOE)
    h = h2.astype(jnp.bfloat16)
    routing = _gates_in_group(_router_logits(h2, wr_hi_ref[0], wr_lo_ref[0], br_ref[0]), tg_ref[i])
    lane = lax.broadcasted_iota(jnp.int32, (1, LANES), 1)
    lane0 = GATE_LANE0 + tg_ref[i] * EXPERTS_PER_GROUP
    rows_per_chunk = TMOE // EXPERTS_PER_GROUP
    y = None
    for j in range(EXPERTS_PER_GROUP):
        for r in range(j * rows_per_chunk, (j + 1) * rows_per_chunk):
            start_gather_tok(nxt, other, r)
            dst = jnp.where(i > 0, row_ref[prv * TMOE + r], (MOE_SLOTS + r) * TILE_ROWS)
            scatter_tok(dst, other, r).start()
        gate = jnp.where(lane == lane0 + j, routing, 0.0).sum(axis=-1, keepdims=True)
        aj = _bdot(h, wg_bf[j])
        bj = _bdot(h, wu_bf[j])
        act = ((aj / (1.0 + jnp.exp(-aj))) * bj * gate).astype(jnp.bfloat16)
        yj = _bdot(act, wd_bf[j])
        y = yj if y is None else y + yj

    @pl.when(i > 0)
    def _():
        wait_scatter(slot)

    _store_token_tiles(ybuf.at[slot], y)

    @pl.when(i == last)
    def _():
        for r in range(TMOE):
            scatter_tok(row_ref[last * TMOE + r], slot, r).start()
        wait_scatter(other)
        wait_scatter(slot)
        wait_gather(other)


def _moe(layer, h2_tiles, tile_group, row_of_slot, wr_hi, wr_lo, b_router, w_gate, w_up, w_down):
    def wspec(shape):
        return pl.BlockSpec((1, EXPERTS_PER_GROUP) + shape, lambda i, tg, rows: (layer, tg[i], 0, 0))

    def rspec(rows):
        return pl.BlockSpec((1, rows, LANES), lambda i, tg, rows_: (layer, 0, 0))

    grid_spec = pltpu.PrefetchScalarGridSpec(
        num_scalar_prefetch=2,
        grid=(NT_MOE,),
        in_specs=[pl.BlockSpec(memory_space=pl.ANY),
                  rspec(D_MODEL), rspec(D_MODEL), rspec(1),
                  wspec((D_MODEL, EXPERT_HIDDEN)), wspec((D_MODEL, EXPERT_HIDDEN)),
                  wspec((EXPERT_HIDDEN, D_MODEL))],
        out_specs=pl.BlockSpec(memory_space=pl.ANY),
        scratch_shapes=[pltpu.VMEM((2, TMOE * TILE_ROWS, LANES), jnp.float32),
                        pltpu.VMEM((2, TMOE * TILE_ROWS, LANES), jnp.float32),
                        pltpu.VMEM((EXPERTS_PER_GROUP, D_MODEL, EXPERT_HIDDEN), jnp.bfloat16),
                        pltpu.VMEM((EXPERTS_PER_GROUP, D_MODEL, EXPERT_HIDDEN), jnp.bfloat16),
                        pltpu.VMEM((EXPERTS_PER_GROUP, EXPERT_HIDDEN, D_MODEL), jnp.bfloat16),
                        pltpu.SemaphoreType.DMA((2,)),
                        pltpu.SemaphoreType.DMA((2,))])
    return pl.pallas_call(
        _moe_kernel,
        grid_spec=grid_spec,
        out_shape=jax.ShapeDtypeStruct(((MOE_SLOTS + TMOE) * TILE_ROWS, LANES), jnp.float32),
        compiler_params=_cparams("arbitrary"),
        name="moe",
    )(tile_group, row_of_slot, h2_tiles, wr_hi, wr_lo, b_router, w_gate, w_up, w_down)


def _final_kernel(x1_ref, y_ref, mod_ref, g_ref, o_ref):
    x = x1_ref[...] + mod_ref[0, :, 5 * D_MODEL:6 * D_MODEL] * _load_token_tiles(y_ref, TM)
    ms = jnp.mean(x * x, axis=-1, keepdims=True)
    o_ref[...] = x * lax.rsqrt(ms + EPS) * g_ref[...]


def _final_norm(x1, y, mods3, g):
    return pl.pallas_call(
        _final_kernel,
        grid=(T_ALL // TM,),
        in_specs=[pl.BlockSpec((TM, D_MODEL), lambda i: (i, 0)),
                  pl.BlockSpec((TM * TILE_ROWS, LANES), lambda i: (i, 0)),
                  pl.BlockSpec((1, 1, 6 * D_MODEL), lambda i: (_modset(i), 0, 0)),
                  pl.BlockSpec((1, D_MODEL), lambda i: (0, 0))],
        out_specs=pl.BlockSpec((TM, D_MODEL), lambda i: (i, 0)),
        out_shape=jax.ShapeDtypeStruct((T_ALL, D_MODEL), jnp.float32),
        compiler_params=_cparams("arbitrary"),
        name="final_norm",
    )(x1, y, mods3, g)


_QG_ORDER = np.array([0, 3, 1, 4, 2, 5])


def _permute_qg_columns(w_in):
    c1 = SSM_WIDTH + 3 * NA_WIDTH
    qg = w_in[..., c1:c1 + GQA_WIDTH].reshape(*w_in.shape[:-1], GQA_HEADS, HEAD_DIM)
    qg = qg[..., _QG_ORDER, :].reshape(*w_in.shape[:-1], GQA_WIDTH)
    return jnp.concatenate([w_in[..., :c1], qg, w_in[..., c1 + GQA_WIDTH:]], axis=-1)


def _permute_yg_rows(w_out):
    r0 = SSM_WIDTH + NA_WIDTH
    yg = w_out[:, r0:, :].reshape(DEPTH, GQA_HEADS, HEAD_DIM, D_MODEL)[:, _QG_ORDER]
    return jnp.concatenate([w_out[:, :r0, :], yg.reshape(DEPTH, GQA_WIDTH, D_MODEL)], axis=1)


def _rope_tables():
    t = jnp.arange(DEC_SEQ)
    row = (t // GRID_W).astype(jnp.float32)
    col = (t % GRID_W).astype(jnp.float32)
    freqs = ROPE_THETA ** (-jnp.arange(ROPE_F, dtype=jnp.float32) / ROPE_F)
    ang_r = row[:, None] * freqs
    ang_c = col[:, None] * freqs
    ang = jnp.concatenate([ang_r, ang_r, ang_c, ang_c], axis=1)
    cos = jnp.cos(ang)
    sin = jnp.sin(ang)
    sign = jnp.tile(jnp.concatenate([-jnp.ones(ROPE_F), jnp.ones(ROPE_F)]), 2).astype(jnp.float32)
    sin = sin * sign
    cos = jnp.tile(cos, (1, 2))
    sin = jnp.tile(sin, (1, 2))
    cos = jnp.concatenate([jnp.ones((TM, LANES), jnp.float32), cos], axis=0)
    sin = jnp.concatenate([jnp.zeros((TM, LANES), jnp.float32), sin], axis=0)
    return cos, sin


def _pair_layout(cache):
    b, l, h, n, dh = cache.shape
    c = cache.reshape(b, l, h // 2, 2, n, dh).transpose(1, 0, 2, 4, 3, 5)
    return c.reshape(l, b, h // 2, n, 2 * dh).astype(jnp.bfloat16)


def _block_diag_b(bbar):
    bb = bbar.reshape(DEPTH, 2, SSM_GROUP_CH, SSM_GROUPS, SSM_STATE)
    eye = jnp.eye(SSM_GROUPS, dtype=bbar.dtype)
    out = eye[None, None, :, None, :, None] * bb.transpose(0, 1, 3, 2, 4)[:, :, :, :, None, :]
    return out.reshape(DEPTH, 2, SSM_WIDTH, S5_HALF)


def _block_diag_c(c):
    eye = jnp.eye(SSM_GROUPS, dtype=c.dtype)
    out = eye[None, None, :, None, :, None] * c.transpose(0, 1, 2, 4, 3)[:, :, :, :, None, :]
    return out.reshape(DEPTH, 2, S5_HALF, SSM_WIDTH)


def kernel(x_prompt, x_sample, cache_na_k, cache_na_v, cache_gqa_k, cache_gqa_v, state_ssm, c, c_ctx, norm_mix_g, norm_ffn_g, w_ada, b_ada, w_in, ssm_a_re, ssm_a_im, ssm_log_dt, ssm_b_re, ssm_b_im, ssm_c_re, ssm_c_im, ssm_d, w_glu, b_glu, na_rpb, q_norm_g, k_norm_g, w_out, moe_w_group, moe_b_group, moe_w_expert, moe_b_expert, moe_w_gate, moe_w_up, moe_w_down, final_norm_g):
    f32, bf16 = jnp.float32, jnp.bfloat16
    x = jnp.concatenate([x_prompt.reshape(T_CTX, D_MODEL), x_sample.reshape(T_LAT, D_MODEL)], axis=0)

    cvec8 = jnp.concatenate([c_ctx[None, :], c, jnp.zeros((SUBLANES - N_MODSETS, D_MODEL), f32)], axis=0)
    mods_all = _adaln_all(cvec8, w_ada, b_ada)

    w_in_bf = _permute_qg_columns(w_in).astype(bf16)
    w_out_bf = _permute_yg_rows(w_out).astype(bf16)
    w_glu_bf = w_glu.astype(bf16)
    w_router = jnp.concatenate(
        [moe_w_group, moe_w_expert, jnp.zeros((DEPTH, D_MODEL, LANES - N_GROUPS - N_EXPERTS), f32)], axis=-1)
    wr_hi = w_router.astype(bf16)
    wr_lo = (w_router - wr_hi.astype(f32)).astype(bf16)
    b_router = jnp.concatenate(
        [moe_b_group, moe_b_expert, jnp.zeros((DEPTH, LANES - N_GROUPS - N_EXPERTS), f32)], axis=-1)[:, None, :]
    q_gain = jnp.tile(q_norm_g, (1, 2))
    k_gain = jnp.tile(k_norm_g, (1, 2))
    lane = np.arange(LANES)
    gsum = jnp.asarray((lane[:, None] // HEAD_DIM) == (lane[None, :] // HEAD_DIM), bf16)
    rope_cos, rope_sin = _rope_tables()

    abar_re, abar_im, bbar_re, bbar_im = _s5_prep(ssm_a_re, ssm_a_im, ssm_log_dt, ssm_b_re, ssm_b_im)
    abar = jnp.concatenate([abar_re, abar_im], axis=-1).reshape(DEPTH, 2, 1, S5_CPLX)
    bblk = jnp.concatenate([_block_diag_b(bbar_re), _block_diag_b(bbar_im)], axis=-1).astype(bf16)
    cblk = jnp.concatenate([_block_diag_c(ssm_c_re), -_block_diag_c(ssm_c_im)], axis=2).astype(bf16)
    h0 = state_ssm.transpose(1, 2, 0, 5, 3, 4).reshape(DEPTH, 2, DEC_BATCH, S5_CPLX)

    ck_na, cv_na = _pair_layout(cache_na_k), _pair_layout(cache_na_v)
    ck_g, cv_g = _pair_layout(cache_gqa_k)[:, :, 0], _pair_layout(cache_gqa_v)[:, :, 0]
    first_half = jnp.asarray(np.arange(LANES) < HEAD_DIM)
    cv_g = jnp.stack([jnp.where(first_half, cv_g, 1.0), jnp.where(first_half, 1.0, cv_g)],
                     axis=2).astype(bf16)
    na_bias = _na_bias_table(na_rpb)

    kv32_l, fin_l = [], []
    moe_residual = None
    for l in range(DEPTH):
        mods3 = mods_all[l, :N_MODSETS].reshape(N_MODSETS, 1, 6 * D_MODEL)
        u, qkv, kv32, x = _in_proj(x, moe_residual, mods3, norm_mix_g[l][None], w_in_bf[l], q_gain[l][None],
                                   k_gain[l][None], gsum, rope_cos, rope_sin)
        kv32_l.append(kv32)

        u_tm = jnp.pad(u.reshape(S5_SEQS, S5_CHUNK, SSM_WIDTH).transpose(1, 0, 2),
                       ((0, 0), (0, S5_ROWS - S5_SEQS), (0, 0)))
        y_dir, powers, finals = _s5_scan(u_tm.reshape(S5_CHUNK * S5_ROWS, SSM_WIDTH), bblk[l], abar[l], cblk[l])
        fin_l.append(finals[:, :BATCH])
        y_tm = _s5_finish(u_tm, y_dir.reshape(2, S5_CHUNK, S5_ROWS, SSM_WIDTH), powers, finals, h0[l], cblk[l],
                          ssm_d[l][None], w_glu_bf[l], b_glu[l][None])
        y_ssm = y_tm[:, :S5_SEQS].transpose(1, 0, 2).reshape(T_ALL, SSM_WIDTH)

        y_ctx = _attn_ctx(qkv)
        y_g_lat = _gqa_lat(qkv, ck_g[l], cv_g[l])
        y_na_lat = _na_lat(l, qkv, ck_na[l], cv_na[l], na_bias)

        x1, h2_tiles, group = _out_proj(l, x, y_ssm, y_ctx, y_na_lat, y_g_lat, mods3, norm_ffn_g[l][None],
                                        w_out_bf[l], wr_hi, wr_lo, b_router)
        tile_group, row_of_slot = _moe_dispatch(group[:, 0])
        y_moe = _moe(l, h2_tiles, tile_group, row_of_slot, wr_hi, wr_lo, b_router,
                     moe_w_gate, moe_w_up, moe_w_down)
        x, moe_residual = x1, (y_moe, mods3)

    y = _final_norm(x, moe_residual[0], moe_residual[1], final_norm_g[None])
    y_prompt = y[:T_CTX].reshape(BATCH, SEQ, D_MODEL)
    y_sample = y[T_CTX:].reshape(DEC_BATCH, DEC_SEQ, D_MODEL)

    kv32 = jnp.stack(kv32_l, axis=0).reshape(DEPTH, BATCH, SEQ, KV32_WIDTH)

    def heads(t, n_heads):
        return t.reshape(DEPTH, BATCH, SEQ, n_heads, HEAD_DIM).transpose(1, 0, 3, 2, 4)

    new_na_k = heads(kv32[..., 0:NA_WIDTH], NA_HEADS)
    new_na_v = heads(kv32[..., NA_WIDTH:2 * NA_WIDTH], NA_HEADS)
    new_gqa_k = heads(kv32[..., 2 * NA_WIDTH:2 * NA_WIDTH + KV_WIDTH], GQA_KV_HEADS)
    new_gqa_v = heads(kv32[..., 2 * NA_WIDTH + KV_WIDTH:], GQA_KV_HEADS)
    fin = jnp.stack(fin_l, axis=0).reshape(DEPTH, 2, BATCH, 2, SSM_GROUPS, SSM_STATE)
    new_state = fin.transpose(2, 0, 1, 4, 5, 3)
    return (y_prompt, y_sample, new_na_k, new_na_v, new_gqa_k, new_gqa_v, new_state)
```
